```python
import jax, jax.numpy as jnp
from jax import lax
import numpy as np

D_MODEL = 1024
BATCH = 16
SEQ = 2048
DEPTH = 4

N_MIXERS = 2
N_RET_LAYERS = (DEPTH + 1) // 2
N_NSA_LAYERS = DEPTH // 2

D_FF = 2816
RMS_EPS = 1e-6

RET_HEADS = 4
RET_DK = D_MODEL // RET_HEADS
RET_DV = 2 * RET_DK
RET_CHUNK = 128
RET_ROPE_BASE = 10000.0
RET_IN = RET_HEADS * (2 * RET_DK + 2 * RET_DV)

NSA_HEADS = 16
NSA_GROUPS = 4
NSA_HPG = NSA_HEADS // NSA_GROUPS
NSA_DH = 64
NSA_KV = NSA_GROUPS * NSA_DH
CMP_LEN = 32
CMP_STRIDE = 16
CMP_HID = 256
SLC_LEN = 64
N_SEL = 8
WINDOW = 512
NSA_QBLOCK = 32
NSA_IN = NSA_HEADS * NSA_DH + 6 * NSA_KV + 3 * NSA_HEADS

NUM_BUCKETS = 32
MAX_DISTANCE = 128

NEG_INF = -1e30
SEL_BIG = 1e9

kernel_name = "hybrid_retention_nsa_macaron"


def rmsnorm(x, g):
    xf = x.astype(jnp.float32)
    y = xf * lax.rsqrt(jnp.mean(xf * xf, axis=-1, keepdims=True) + RMS_EPS)
    return (y * g.astype(jnp.float32)).astype(x.dtype)


def headnorm(a, g):
    af = a.astype(jnp.float32)
    return af * lax.rsqrt(jnp.mean(af * af, axis=-1, keepdims=True) + RMS_EPS) * g.astype(jnp.float32)


def swiglu_ffn(x, norm_g, w_gu, w_down):
    h = rmsnorm(x, norm_g) @ w_gu
    a, b = jnp.split(h, 2, axis=-1)
    return (jax.nn.silu(a) * b) @ w_down


def t5_bucket(dist):
    n = jnp.maximum(dist, 0)
    max_exact = NUM_BUCKETS // 2
    nf = jnp.maximum(n, max_exact).astype(jnp.float32)
    large = max_exact + (jnp.log(nf / max_exact) / np.float32(np.log(MAX_DISTANCE / max_exact))
                         * (NUM_BUCKETS - max_exact)).astype(jnp.int32)
    large = jnp.minimum(large, NUM_BUCKETS - 1)
    return jnp.where(n < max_exact, n, large)


def rotary(x, pos):
    half = x.shape[-1] // 2
    inv_freq = RET_ROPE_BASE ** (-jnp.arange(half, dtype=jnp.float32) / half)
    ang = pos[:, None] * inv_freq[None, :]
    cos, sin = jnp.cos(ang), jnp.sin(ang)
    x1, x2 = x[..., :half], x[..., half:]
    return jnp.concatenate([x1 * cos - x2 * sin, x1 * sin + x2 * cos], axis=-1)


def retention(xn, w_in, w_out):
    B, T, _ = xn.shape
    H, dk, dv, C = RET_HEADS, RET_DK, RET_DV, RET_CHUNK
    N = T // C
    h = xn @ w_in
    q, k, v, g = jnp.split(h, [H * dk, 2 * H * dk, 2 * H * dk + H * dv], axis=-1)
    q = q.reshape(B, T, H, dk).transpose(0, 2, 1, 3).astype(jnp.float32)
    k = k.reshape(B, T, H, dk).transpose(0, 2, 1, 3).astype(jnp.float32)
    v = v.reshape(B, T, H, dv).transpose(0, 2, 1, 3).astype(jnp.float32)
    pos = jnp.arange(T, dtype=jnp.float32)
    q = rotary(q, pos)
    k = rotary(k, pos) * (dk ** -0.5)
    log_gamma = jnp.log(1.0 - 2.0 ** (-5.0 - jnp.arange(H, dtype=jnp.float32)))
    i = jnp.arange(C)
    diff = i[:, None] - i[None, :]
    intra = jnp.where(diff >= 0, jnp.exp(log_gamma[:, None, None] * jnp.maximum(diff, 0)), 0.0)
    q_dec = jnp.exp(log_gamma[:, None] * (i + 1))[:, :, None]
    k_dec = jnp.exp(log_gamma[:, None] * (C - 1 - i))[:, :, None]
    chunk_dec = jnp.exp(log_gamma * C)[:, None, None]

    def to_chunks(a):
        return a.reshape(B, H, N, C, a.shape[-1]).transpose(2, 0, 1, 3, 4)

    def step(S, qkv):
        qc, kc, vc = qkv
        s = jnp.einsum('bhid,bhjd->bhij', qc, kc) * intra
        o = (jnp.einsum('bhij,bhjv->bhiv', s, vc)
             + jnp.einsum('bhid,bhdv->bhiv', qc * q_dec, S))
        S = chunk_dec * S + jnp.einsum('bhjd,bhjv->bhdv', kc * k_dec, vc)
        return S, o

    S0 = jnp.zeros((B, H, dk, dv), jnp.float32)
    _, o = lax.scan(step, S0, (to_chunks(q), to_chunks(k), to_chunks(v)))
    o = o.transpose(1, 2, 0, 3, 4).reshape(B, H, T, dv)
    o = o * lax.rsqrt(jnp.mean(o * o, axis=-1, keepdims=True) + RMS_EPS)
    o = o.transpose(0, 2, 1, 3).reshape(B, T, H * dv)
    o = jax.nn.silu(g.astype(jnp.float32)) * o
    return o.astype(xn.dtype) @ w_out


def masked_softmax(s, valid):
    p = jax.nn.softmax(jnp.where(valid, s, NEG_INF), axis=-1)
    return jnp.where(valid, p, 0.0)


def nsa(xn, w_in, w_out, q_gain, k_gain, cmp_pos, cmp_w1, cmp_w2, rel_bias):
    B, T, _ = xn.shape
    H, G, P, dh, QB = NSA_HEADS, NSA_GROUPS, NSA_HPG, NSA_DH, NSA_QBLOCK
    n_cmp = (T - CMP_LEN) // CMP_STRIDE + 1
    n_slc = T // SLC_LEN
    n_sel = min(N_SEL, n_slc)
    sizes = [H * dh] + [NSA_KV] * 6 + [3 * H]
    parts = jnp.split(xn @ w_in, list(np.cumsum(sizes)[:-1]), axis=-1)
    q_, kc_, vc_, ks_, vs_, kw_, vw_, gate_ = parts

    def heads(a, n):
        return a.reshape(B, T, n, dh).transpose(0, 2, 1, 3)

    q = headnorm(heads(q_, H), q_gain).reshape(B, G, P, T, dh)
    cidx = np.arange(n_cmp)[:, None] * CMP_STRIDE + np.arange(CMP_LEN)[None, :]

    def compress(a, j):
        blocks = a[:, :, cidx] + cmp_pos[j]
        flat = blocks.reshape(B, G, n_cmp, CMP_LEN * dh)
        return jax.nn.gelu(flat @ cmp_w1[j]) @ cmp_w2[j]

    k_cmp = headnorm(compress(heads(kc_, G), 0), k_gain[0])
    v_cmp = compress(heads(vc_, G), 1)
    k_slc = headnorm(heads(ks_, G), k_gain[1]).reshape(B, G, n_slc, SLC_LEN, dh)
    v_slc = heads(vs_, G).reshape(B, G, n_slc, SLC_LEN, dh)
    pad = ((0, 0), (0, 0), (WINDOW, 0), (0, 0))
    k_win = jnp.pad(headnorm(heads(kw_, G), k_gain[2]), pad)
    v_win = jnp.pad(heads(vw_, G), pad)
    gates = jax.nn.sigmoid(gate_.astype(jnp.float32)).reshape(B, T, 3, G, P).transpose(0, 2, 3, 4, 1)

    ci = np.arange(n_cmp) * CMP_STRIDE
    sj = np.arange(n_slc) * SLC_LEN
    overlap = jnp.asarray(((ci[:, None] < sj[None, :] + SLC_LEN) &
                           (ci[:, None] + CMP_LEN > sj[None, :])).astype(np.float32))
    cmp_end = jnp.arange(n_cmp) * CMP_STRIDE + CMP_LEN - 1
    tbl = rel_bias.astype(jnp.float32)
    tbl_hp = tbl.reshape(NUM_BUCKETS, G, P)
    tbl_g = tbl_hp.transpose(1, 0, 2)
    bi = jnp.arange(B)[:, None, None, None]
    gi = jnp.arange(G)[None, :, None, None]
    scale = dh ** -0.5
    blk_ids = jnp.arange(n_slc)

    def dense_bias(dist):
        return tbl_hp[t5_bucket(dist)].transpose(2, 3, 0, 1)

    def block(qb):
        t0 = qb * QB
        qpos = t0 + jnp.arange(QB)
        qblk = lax.dynamic_slice_in_dim(q, t0, QB, axis=3)
        gblk = lax.dynamic_slice_in_dim(gates, t0, QB, axis=4)
        s = (jnp.einsum('bgpqd,bgnd->bgpqn', qblk, k_cmp) * scale
             + dense_bias(qpos[:, None] - cmp_end[None, :]))
        p_c = masked_softmax(s, cmp_end[None, :] <= qpos[:, None])
        o_c = jnp.einsum('bgpqn,bgnd->bgpqd', p_c, v_cmp)
        imp = jnp.einsum('bgpqn,nj->bgqj', p_c, overlap)
        cur = qpos // SLC_LEN
        forced = ((blk_ids[None, :] == 0) | (blk_ids[None, :] == cur[:, None])
                  | (blk_ids[None, :] == cur[:, None] - 1))
        imp = jnp.where(forced, SEL_BIG, jnp.where(blk_ids[None, :] <= cur[:, None], imp, -SEL_BIG))
        _, sel = lax.top_k(imp, n_sel)
        ks = k_slc[bi, gi, sel].reshape(B, G, QB, n_sel * SLC_LEN, dh)
        vs = v_slc[bi, gi, sel].reshape(B, G, QB, n_sel * SLC_LEN, dh)
        kpos = (sel[..., None] * SLC_LEN + jnp.arange(SLC_LEN)).reshape(B, G, QB, n_sel * SLC_LEN)
        dist = qpos[None, None, :, None] - kpos
        bias_s = jnp.moveaxis(tbl_g[gi, t5_bucket(dist)], -1, 2)
        s = jnp.einsum('bgpqd,bgqkd->bgpqk', qblk, ks) * scale + bias_s
        p_s = masked_softmax(s, (dist >= 0)[:, :, None])
        o_s = jnp.einsum('bgpqk,bgqkd->bgpqd', p_s, vs)
        kw = lax.dynamic_slice_in_dim(k_win, t0, QB + WINDOW, axis=2)
        vw = lax.dynamic_slice_in_dim(v_win, t0, QB + WINDOW, axis=2)
        wpos = t0 - WINDOW + jnp.arange(QB + WINDOW)
        dist_w = qpos[:, None] - wpos[None, :]
        valid_w = (dist_w >= 0) & (dist_w < WINDOW) & (wpos[None, :] >= 0)
        s = jnp.einsum('bgpqd,bgkd->bgpqk', qblk, kw) * scale + dense_bias(dist_w)
        p_w = masked_softmax(s, valid_w)
        o_w = jnp.einsum('bgpqk,bgkd->bgpqd', p_w, vw)
        return (gblk[:, 0, ..., None] * o_c + gblk[:, 1, ..., None] * o_s
                + gblk[:, 2, ..., None] * o_w)

    o = lax.map(block, jnp.arange(T // QB))
    o = o.transpose(1, 0, 4, 2, 3, 5).reshape(B, T, H * dh)
    return o.astype(xn.dtype) @ w_out


def setup_inputs(seed: int = 0) -> dict:
    key = jax.random.key(seed)
    ks = jax.random.split(key, 20)
    f32 = jnp.float32

    def w(k, shape, fan_in):
        return jax.random.normal(k, shape, f32) * (fan_in ** -0.5)

    def gain(k, shape):
        return 1.0 + 0.02 * jax.random.normal(k, shape, f32)

    return {
        "x": jax.random.normal(ks[0], (BATCH, SEQ, D_MODEL), f32),
        "ffn1_norm": gain(ks[1], (DEPTH, D_MODEL)),
        "ffn1_w_gu": w(ks[2], (DEPTH, D_MODEL, 2 * D_FF), D_MODEL),
        "ffn1_w_down": w(ks[3], (DEPTH, D_FF, D_MODEL), D_FF),
        "mix_norm": gain(ks[4], (DEPTH, D_MODEL)),
        "ffn2_norm": gain(ks[5], (DEPTH, D_MODEL)),
        "ffn2_w_gu": w(ks[6], (DEPTH, D_MODEL, 2 * D_FF), D_MODEL),
        "ffn2_w_down": w(ks[7], (DEPTH, D_FF, D_MODEL), D_FF),
        "ret_w_in": w(ks[8], (N_RET_LAYERS, D_MODEL, RET_IN), D_MODEL),
        "ret_w_out": w(ks[9], (N_RET_LAYERS, RET_HEADS * RET_DV, D_MODEL), RET_HEADS * RET_DV),
        "nsa_w_in": w(ks[10], (N_NSA_LAYERS, D_MODEL, NSA_IN), D_MODEL),
        "nsa_w_out": w(ks[11], (N_NSA_LAYERS, NSA_HEADS * NSA_DH, D_MODEL), NSA_HEADS * NSA_DH),
        "nsa_q_gain": gain(ks[12], (N_NSA_LAYERS, NSA_DH)),
        "nsa_k_gain": gain(ks[13], (N_NSA_LAYERS, 3, NSA_DH)),
        "nsa_cmp_pos": 0.1 * jax.random.normal(ks[14], (N_NSA_LAYERS, 2, CMP_LEN, NSA_DH), f32),
        "nsa_cmp_w1": w(ks[15], (N_NSA_LAYERS, 2, CMP_LEN * NSA_DH, CMP_HID), CMP_LEN * NSA_DH),
        "nsa_cmp_w2": w(ks[16], (N_NSA_LAYERS, 2, CMP_HID, NSA_DH), CMP_HID),
        "rel_bias": 0.5 * jax.random.normal(ks[17], (NUM_BUCKETS, NSA_HEADS), f32),
    }


def reference(x, ffn1_norm, ffn1_w_gu, ffn1_w_down, mix_norm, ffn2_norm, ffn2_w_gu, ffn2_w_down,
              ret_w_in, ret_w_out, nsa_w_in, nsa_w_out, nsa_q_gain, nsa_k_gain,
              nsa_cmp_pos, nsa_cmp_w1, nsa_cmp_w2, rel_bias):
    for layer in range(DEPTH):
        x = x + 0.5 * swiglu_ffn(x, ffn1_norm[layer], ffn1_w_gu[layer], ffn1_w_down[layer])
        xn = rmsnorm(x, mix_norm[layer])
        j = layer // N_MIXERS
        if layer % N_MIXERS == 0:
            m = retention(xn, ret_w_in[j], ret_w_out[j])
        else:
            m = nsa(xn, nsa_w_in[j], nsa_w_out[j], nsa_q_gain[j], nsa_k_gain[j],
                    nsa_cmp_pos[j], nsa_cmp_w1[j], nsa_cmp_w2[j], rel_bias)
        x = x + m
        x = x + 0.5 * swiglu_ffn(x, ffn2_norm[layer], ffn2_w_gu[layer], ffn2_w_down[layer])
    return x
```

```python
import functools

import numpy as np
import jax
import jax.numpy as jnp
from jax import lax
from jax.experimental import pallas as pl
from jax.experimental.pallas import tpu as pltpu

F32 = jnp.float32
BF16 = jnp.bfloat16

RMS_EPS = 1e-6
NEG = -1e30
SEL_BIG = 1e9

LANES = 128
VMEM_LIMIT = 56 * 1024 * 1024

RET_HEADS = 4
RET_DK = 256
RET_DV = 512
RET_CHUNK = 128
RET_ROPE_BASE = 10000.0

NSA_HEADS = 16
NSA_GROUPS = 4
NSA_HPG = 4
NSA_DH = 64
CMP_LEN = 32
CMP_STRIDE = 16
CMP_HID = 256
SLC_LEN = 64
N_SEL = 8
WINDOW = 512
NUM_BUCKETS = 32
MAX_DISTANCE = 128
TQ = 128
NSA_COLS = 1024 + 3 * 4 * 128 + 4 * 128


def _dot(a, b):
    return jnp.dot(a, b, preferred_element_type=F32)


def _dot_nt(a, b):
    return lax.dot_general(a, b, (((1,), (1,)), ((), ())), preferred_element_type=F32)


def _dot_tn(a, b):
    return lax.dot_general(a, b, (((0,), (0,)), ((), ())), preferred_element_type=F32)


def _rms(x, g):
    ms = jnp.mean(x * x, axis=-1, keepdims=True)
    return x * lax.rsqrt(ms + RMS_EPS) * g


def _cparams(sem):
    return pltpu.CompilerParams(dimension_semantics=sem, vmem_limit_bytes=VMEM_LIMIT)


def _resident(shape):
    n = len(shape)
    return pl.BlockSpec(shape, lambda *_: (0,) * n, pipeline_mode=pl.Buffered(1))


def _ffn_kernel(x_ref, g_ref, wgu_ref, wd_ref, o_ref, xn_ref, acc_ref, *, nf):
    x = x_ref[...]
    xn_ref[...] = _rms(x, g_ref[...]).astype(BF16)
    acc_ref[...] = jnp.zeros_like(acc_ref)

    def body(j, carry):
        xn = xn_ref[...]
        a = _dot(xn, wgu_ref[0, j])
        b = _dot(xn, wgu_ref[1, j])
        h = (a * jax.nn.sigmoid(a) * b).astype(BF16)
        acc_ref[...] += _dot(h, wd_ref[j])
        return carry

    lax.fori_loop(0, nf, body, 0)
    o_ref[...] = x_ref[...] + 0.5 * acc_ref[...]


def _ffn(x2, g, wgu, wd, tm=512):
    n, d = x2.shape
    _, nf, _, tf = wgu.shape
    return pl.pallas_call(
        functools.partial(_ffn_kernel, nf=nf),
        grid=(n // tm,),
        in_specs=[
            pl.BlockSpec((tm, d), lambda i: (i, 0)),
            _resident((1, d)),
            _resident((2, nf, d, tf)),
            _resident((nf, tf, d)),
        ],
        out_specs=pl.BlockSpec((tm, d), lambda i: (i, 0)),
        out_shape=jax.ShapeDtypeStruct((n, d), F32),
        scratch_shapes=[pltpu.VMEM((tm, d), BF16), pltpu.VMEM((tm, d), F32)],
        compiler_params=_cparams(("parallel",)),
    )(x2, g, wgu, wd)


def _norm_matmul_kernel(x_ref, g_ref, w_ref, o_ref, *, tn):
    xn = _rms(x_ref[...], g_ref[...]).astype(BF16)
    for j in range(w_ref.shape[1] // tn):
        o_ref[:, j * tn:(j + 1) * tn] = _dot(xn, w_ref[:, j * tn:(j + 1) * tn]).astype(o_ref.dtype)


def _norm_matmul(x2, g, w, tn, tm=512):
    n, d = x2.shape
    nout = w.shape[1]
    return pl.pallas_call(
        functools.partial(_norm_matmul_kernel, tn=tn),
        grid=(n // tm,),
        in_specs=[pl.BlockSpec((tm, d), lambda i: (i, 0)), _resident((1, d)), _resident((d, nout))],
        out_specs=pl.BlockSpec((tm, nout), lambda i: (i, 0)),
        out_shape=jax.ShapeDtypeStruct((n, nout), BF16),
        compiler_params=_cparams(("parallel",)),
    )(x2, g, w)


def _matmul_res_kernel(o_ref, w_ref, x_ref, y_ref):
    y_ref[...] = x_ref[...] + _dot(o_ref[...], w_ref[...])


def _matmul_res(o2, w, x2, tm=512):
    n, k = o2.shape
    d = w.shape[1]
    return pl.pallas_call(
        _matmul_res_kernel,
        grid=(n // tm,),
        in_specs=[pl.BlockSpec((tm, k), lambda i: (i, 0)), _resident((k, d)),
                  pl.BlockSpec((tm, d), lambda i: (i, 0))],
        out_specs=pl.BlockSpec((tm, d), lambda i: (i, 0)),
        out_shape=jax.ShapeDtypeStruct((n, d), F32),
        compiler_params=_cparams(("parallel",)),
    )(o2, w, x2)


def _ret_kernel(cd_ref, q_ref, k_ref, v_ref, g_ref, cos_ref, sin_ref, intra_ref, qd_ref, kd_ref,
                o_ref, s_ref, *, n_chunks):
    c_len = RET_CHUNK
    half = RET_DK // 2
    cd = cd_ref[pl.program_id(1)]
    intra = intra_ref[0]
    qd = qd_ref[0]
    kd = kd_ref[0]
    s_ref[...] = jnp.zeros_like(s_ref)

    def chunk(c, carry):
        r0 = pl.multiple_of(c * c_len, c_len)
        cos = cos_ref[pl.ds(r0, c_len), :]
        sin = sin_ref[pl.ds(r0, c_len), :]

        def rot(x):
            x1, x2 = x[:, :half], x[:, half:]
            return jnp.concatenate([x1 * cos - x2 * sin, x1 * sin + x2 * cos], axis=-1)

        qr = rot(q_ref[0, pl.ds(r0, c_len), :].astype(F32))
        kr = rot(k_ref[0, pl.ds(r0, c_len), :].astype(F32)) * (RET_DK ** -0.5)
        vc = v_ref[0, pl.ds(r0, c_len), :]
        s = _dot_nt(qr.astype(BF16), kr.astype(BF16)) * intra
        state = s_ref[...]
        o = _dot(s.astype(BF16), vc) + _dot((qr * qd).astype(BF16), state.astype(BF16))
        s_ref[...] = cd * state + _dot_tn((kr * kd).astype(BF16), vc)
        o = o * lax.rsqrt(jnp.mean(o * o, axis=-1, keepdims=True) + RMS_EPS)
        gate = g_ref[0, pl.ds(r0, c_len), :].astype(F32)
        o_ref[0, pl.ds(r0, c_len), :] = (gate * jax.nn.sigmoid(gate) * o).astype(o_ref.dtype)
        return carry

    lax.fori_loop(0, n_chunks, chunk, 0)


def _retention_core(h3):
    b, t, _ = h3.shape
    hh, dk, dv, c = RET_HEADS, RET_DK, RET_DV, RET_CHUNK
    half = dk // 2
    pos = jnp.arange(t, dtype=F32)
    inv_freq = RET_ROPE_BASE ** (-jnp.arange(half, dtype=F32) / half)
    ang = pos[:, None] * inv_freq[None, :]
    cos, sin = jnp.cos(ang), jnp.sin(ang)
    log_gamma = jnp.log(1.0 - 2.0 ** (-5.0 - jnp.arange(hh, dtype=F32)))
    i = jnp.arange(c)
    diff = i[:, None] - i[None, :]
    intra = jnp.where(diff >= 0, jnp.exp(log_gamma[:, None, None] * jnp.maximum(diff, 0)), 0.0)
    q_dec = jnp.exp(log_gamma[:, None] * (i + 1))[:, :, None]
    k_dec = jnp.exp(log_gamma[:, None] * (c - 1 - i))[:, :, None]
    chunk_dec = jnp.exp(log_gamma * c)
    nq = hh * dk // dk
    return pl.pallas_call(
        functools.partial(_ret_kernel, n_chunks=t // c),
        grid=(b, hh),
        in_specs=[
            pl.BlockSpec(memory_space=pltpu.SMEM),
            pl.BlockSpec((1, t, dk), lambda bi, hi: (bi, 0, hi)),
            pl.BlockSpec((1, t, dk), lambda bi, hi: (bi, 0, nq + hi)),
            pl.BlockSpec((1, t, dv), lambda bi, hi: (bi, 0, nq + hi)),
            pl.BlockSpec((1, t, dv), lambda bi, hi: (bi, 0, 2 * nq + hi)),
            _resident((t, half)),
            _resident((t, half)),
            pl.BlockSpec((1, c, c), lambda bi, hi: (hi, 0, 0)),
            pl.BlockSpec((1, c, 1), lambda bi, hi: (hi, 0, 0)),
            pl.BlockSpec((1, c, 1), lambda bi, hi: (hi, 0, 0)),
        ],
        out_specs=pl.BlockSpec((1, t, dv), lambda bi, hi: (bi, 0, hi)),
        out_shape=jax.ShapeDtypeStruct((b, t, hh * dv), BF16),
        scratch_shapes=[pltpu.VMEM((dk, dv), F32)],
        compiler_params=_cparams(("parallel", "arbitrary")),
    )(chunk_dec, h3, h3, h3, h3, cos, sin, intra, q_dec, k_dec)


def _k_headnorm(x, gain_row):
    klane = lax.broadcasted_iota(jnp.int32, x.shape, 1) < NSA_DH
    ms = jnp.sum(jnp.where(klane, x * x, 0.0), axis=-1, keepdims=True) * (1.0 / NSA_DH)
    return jnp.where(klane, x * lax.rsqrt(ms + RMS_EPS) * gain_row, x)


def _cmp_kernel(a_ref, pos_ref, w1_ref, w2_ref, kg_ref, o_ref, af_ref, *, t):
    n_blk = t // CMP_STRIDE
    af_ref[0:t, :] = a_ref[0].astype(F32)
    af_ref[t:t + CMP_LEN, :] = jnp.zeros((CMP_LEN, LANES), F32)
    acc = jnp.zeros((n_blk, 2 * CMP_HID), F32)
    for l in range(CMP_LEN):
        rows = af_ref[pl.ds(l, n_blk, stride=CMP_STRIDE), :]
        acc = acc + _dot((rows + pos_ref[l:l + 1, :]).astype(BF16), w1_ref[l])
    hid = jax.nn.gelu(acc, approximate=True)
    out = _dot(hid.astype(BF16), w2_ref[...])
    o_ref[0, 0] = _k_headnorm(out, kg_ref[...]).astype(o_ref.dtype)


def _compress(h3, pos, w1cat, w2cat, kgain_row):
    b, t, _ = h3.shape
    n_blk = t // CMP_STRIDE
    return pl.pallas_call(
        functools.partial(_cmp_kernel, t=t),
        grid=(b, NSA_GROUPS),
        in_specs=[
            pl.BlockSpec((1, t, LANES), lambda bi, gi: (bi, 0, 8 + gi)),
            _resident((CMP_LEN, LANES)),
            _resident((CMP_LEN, LANES, 2 * CMP_HID)),
            _resident((2 * CMP_HID, LANES)),
            _resident((1, LANES)),
        ],
        out_specs=pl.BlockSpec((1, 1, n_blk, LANES), lambda bi, gi: (bi, gi, 0, 0)),
        out_shape=jax.ShapeDtypeStruct((b, NSA_GROUPS, n_blk, LANES), BF16),
        scratch_shapes=[pltpu.VMEM((t + CMP_LEN, LANES), F32)],
        compiler_params=_cparams(("parallel", "parallel")),
    )(h3, pos, w1cat, w2cat, kgain_row)


def _softmax_update(state, s, kv):
    m, l, acc = state
    m_new = jnp.maximum(m, jnp.max(s, axis=-1, keepdims=True))
    alpha = jnp.exp(m - m_new)
    p = jnp.exp(s - m_new)
    return (m_new, alpha * l + jnp.sum(p, axis=-1, keepdims=True),
            alpha * acc + _dot(p.astype(BF16), kv))


def _nsa_attn_kernel(q_ref, kvc_ref, kvs_ref, kvw_ref, gate_ref, qg_ref, kg_ref, bt_ref, cb_ref,
                     ov_ref, ex_ref, o_ref, ks_ref, kw_ref, am_ref, *, t):
    hpg = NSA_HPG
    qt = pl.program_id(2)
    q0 = qt * TQ
    lane = lax.broadcasted_iota(jnp.int32, (1, LANES), 1)
    klane = lane < NSA_DH

    @pl.when(qt == 0)
    def _():
        rows = 256

        def norm_chunk(c, carry):
            r0 = pl.multiple_of(c * rows, rows)
            ks_ref[pl.ds(r0, rows), :] = _k_headnorm(
                kvs_ref[0, pl.ds(r0, rows), :].astype(F32), kg_ref[0:1, :]).astype(BF16)
            kw_ref[pl.ds(r0, rows), :] = _k_headnorm(
                kvw_ref[0, pl.ds(r0, rows), :].astype(F32), kg_ref[1:2, :]).astype(BF16)
            return carry

        lax.fori_loop(0, t // rows, norm_chunk, 0)

    xq = q_ref[0].astype(F32)
    lane_q = lax.broadcasted_iota(jnp.int32, (1, hpg * NSA_DH), 1)
    xq2 = xq * xq
    qz = []
    for p in range(hpg):
        seg = (lane_q >= NSA_DH * p) & (lane_q < NSA_DH * (p + 1))
        ms = jnp.sum(jnp.where(seg, xq2, 0.0), axis=-1, keepdims=True) * (1.0 / NSA_DH)
        c0 = LANES * (p // 2)
        half = xq[:, c0:c0 + LANES] * lax.rsqrt(ms + RMS_EPS) * qg_ref[:, c0:c0 + LANES]
        if p % 2 == 1:
            half = pltpu.roll(half, NSA_DH, 1)
        qz.append(jnp.where(klane, half, 0.0).astype(BF16))

    iq = lax.broadcasted_iota(jnp.int32, (TQ, LANES), 0)
    jj = lax.broadcasted_iota(jnp.int32, (TQ, LANES), 1)

    valid_c = (CMP_STRIDE * jj + (CMP_LEN - 1)) <= (q0 + iq)
    kvc = kvc_ref[0, 0]
    psum = jnp.zeros((TQ, LANES), F32)
    o_cmp = []
    for p in range(hpg):
        s = jnp.where(valid_c, _dot_nt(qz[p], kvc) + cb_ref[0, p], NEG)
        m = jnp.max(s, axis=-1, keepdims=True)
        e = jnp.where(valid_c, jnp.exp(s - m), 0.0)
        l = jnp.sum(e, axis=-1, keepdims=True)
        pc = e / jnp.where(l > 0.0, l, 1.0)
        psum = psum + pc
        o_cmp.append(_dot(pc.astype(BF16), kvc))
    p_hi = psum.astype(BF16)
    p_lo = (psum - p_hi.astype(F32)).astype(BF16)
    imp = _dot(p_hi, ov_ref[...]) + _dot(p_lo, ov_ref[...])

    n_slc = t // SLC_LEN
    imp_t = imp.T[:n_slc, :]
    jrow = lax.broadcasted_iota(jnp.int32, (n_slc, TQ), 0)
    cur = (q0 + lax.broadcasted_iota(jnp.int32, (n_slc, TQ), 1)) // SLC_LEN
    forced = (jrow == 0) | (jrow == cur) | (jrow == cur - 1)
    vals = jnp.where(forced, SEL_BIG, jnp.where(jrow <= cur, imp_t, -SEL_BIG))
    cnt = jnp.zeros((n_slc, TQ), F32)
    for jp in range(n_slc):
        row = vals[jp:jp + 1, :]
        ahead = (row > vals) | ((row == vals) & (jrow > jp))
        cnt = cnt + jnp.where(ahead, 1.0, 0.0)
    sel_t = jnp.where(cnt < float(N_SEL), 1.0, 0.0)
    sel = jnp.concatenate([sel_t, jnp.zeros((LANES - n_slc, TQ), F32)], axis=0).T
    am_ref[...] = (_dot(sel.astype(BF16), ex_ref[...]) - 1.0) * 1e30

    causal = jnp.where(jj <= iq, 0.0, NEG)
    has_near = jnp.where(qt >= 1, 0.0, NEG)
    kn = pl.multiple_of(jnp.maximum(qt - 1, 0) * TQ, TQ)
    init = (jnp.full((TQ, 1), NEG, F32), jnp.zeros((TQ, 1), F32), jnp.zeros((TQ, LANES), F32))

    kv_d = ks_ref[pl.ds(pl.multiple_of(q0, TQ), TQ), :]
    kv_n = ks_ref[pl.ds(kn, TQ), :]
    add_d = am_ref[:, pl.ds(pl.multiple_of(q0, TQ), TQ)] + causal
    add_n = am_ref[:, pl.ds(kn, TQ)] + has_near
    st = []
    for p in range(hpg):
        s0 = _softmax_update(init, _dot_nt(qz[p], kv_d) + (add_d + bt_ref[0, p, 0]), kv_d)
        st.append(_softmax_update(s0, _dot_nt(qz[p], kv_n) + (add_n + bt_ref[0, p, 1]), kv_n))

    def sel_far(kt, carry):
        k0 = pl.multiple_of(kt * TQ, TQ)
        kv = ks_ref[pl.ds(k0, TQ), :]
        am = am_ref[:, pl.ds(k0, TQ)]
        return tuple(_softmax_update(carry[p], _dot_nt(qz[p], kv) + am, kv) for p in range(hpg))

    st_sel = lax.fori_loop(0, jnp.maximum(qt - 1, 0), sel_far, tuple(st))

    kv_d = kw_ref[pl.ds(pl.multiple_of(q0, TQ), TQ), :]
    kv_n = kw_ref[pl.ds(kn, TQ), :]
    n_win = WINDOW // TQ
    kl = pl.multiple_of(jnp.maximum(qt - n_win, 0) * TQ, TQ)
    kv_l = kw_ref[pl.ds(kl, TQ), :]
    add_l = jnp.where((jj > iq) & (qt >= n_win), 0.0, NEG)
    st = []
    for p in range(hpg):
        s0 = _softmax_update(init, _dot_nt(qz[p], kv_d) + (causal + bt_ref[0, p, 0]), kv_d)
        s1 = _softmax_update(s0, _dot_nt(qz[p], kv_n) + (has_near + bt_ref[0, p, 1]), kv_n)
        st.append(_softmax_update(s1, _dot_nt(qz[p], kv_l) + add_l, kv_l))

    def win_far(kt, carry):
        kv = kw_ref[pl.ds(pl.multiple_of(kt * TQ, TQ), TQ), :]
        return tuple(_softmax_update(carry[p], _dot_nt(qz[p], kv), kv) for p in range(hpg))

    st_win = lax.fori_loop(jnp.maximum(qt - n_win + 1, 0), jnp.maximum(qt - 1, 0), win_far, tuple(st))

    gl = jax.nn.sigmoid(gate_ref[0].astype(F32))
    outs = []
    for p in range(hpg):
        o_s = st_sel[p][2] / st_sel[p][1]
        o_w = st_win[p][2] / st_win[p][1]
        outs.append(gl[:, p:p + 1] * o_cmp[p] + gl[:, hpg + p:hpg + p + 1] * o_s
                    + gl[:, 2 * hpg + p:2 * hpg + p + 1] * o_w)
    packed = [jnp.where(klane, pltpu.roll(outs[2 * i], NSA_DH, 1), outs[2 * i + 1]) for i in range(hpg // 2)]
    o_ref[0] = jnp.concatenate(packed, axis=-1).astype(o_ref.dtype)


def _nsa_attention(h3, kvc, qgain, kgain, bias_tiles, cmp_bias, overlap, expand):
    b, t, _ = h3.shape
    g, hpg = NSA_GROUPS, NSA_HPG
    n_blk = t // CMP_STRIDE
    qw = hpg * NSA_DH
    return pl.pallas_call(
        functools.partial(_nsa_attn_kernel, t=t),
        grid=(b, g, t // TQ),
        in_specs=[
            pl.BlockSpec((1, TQ, qw), lambda bi, gi, qi: (bi, qi, gi)),
            pl.BlockSpec((1, 1, n_blk, LANES), lambda bi, gi, qi: (bi, gi, 0, 0)),
            pl.BlockSpec((1, t, LANES), lambda bi, gi, qi: (bi, 0, 12 + gi)),
            pl.BlockSpec((1, t, LANES), lambda bi, gi, qi: (bi, 0, 16 + gi)),
            pl.BlockSpec((1, TQ, LANES), lambda bi, gi, qi: (bi, qi, 20 + gi)),
            pl.BlockSpec((1, qw), lambda bi, gi, qi: (0, 0)),
            pl.BlockSpec((2, LANES), lambda bi, gi, qi: (0, 0)),
            pl.BlockSpec((1, hpg, 2, TQ, TQ), lambda bi, gi, qi: (gi, 0, 0, 0, 0)),
            pl.BlockSpec((1, hpg, TQ, LANES), lambda bi, gi, qi: (gi, 0, qi, 0)),
            pl.BlockSpec((LANES, LANES), lambda bi, gi, qi: (0, 0)),
            pl.BlockSpec((LANES, t), lambda bi, gi, qi: (0, 0)),
        ],
        out_specs=pl.BlockSpec((1, TQ, qw), lambda bi, gi, qi: (bi, qi, gi)),
        out_shape=jax.ShapeDtypeStruct((b, t, NSA_HEADS * NSA_DH), BF16),
        scratch_shapes=[pltpu.VMEM((t, LANES), BF16), pltpu.VMEM((t, LANES), BF16),
                        pltpu.VMEM((TQ, t), F32)],
        compiler_params=_cparams(("parallel", "parallel", "arbitrary")),
    )(h3, kvc, h3, h3, h3, qgain, kgain, bias_tiles, cmp_bias, overlap, expand)


def _t5_bucket_np(dist):
    n = np.maximum(dist, 0)
    max_exact = NUM_BUCKETS // 2
    nf = np.maximum(n, max_exact).astype(np.float32)
    large = max_exact + (np.log(nf / max_exact) / np.float32(np.log(MAX_DISTANCE / max_exact))
                         * (NUM_BUCKETS - max_exact)).astype(np.int32)
    large = np.minimum(large, NUM_BUCKETS - 1)
    return np.where(n < max_exact, n, large)


def _nsa_constants(t):
    n_cmp = (t - CMP_LEN) // CMP_STRIDE + 1
    n_slc = t // SLC_LEN
    n_blk = t // CMP_STRIDE
    iq = np.arange(TQ)
    near = np.stack([iq[:, None] - iq[None, :], TQ + iq[:, None] - iq[None, :]])
    assert _t5_bucket_np(np.array([TQ + 1]))[0] == NUM_BUCKETS - 1
    near_bucket = _t5_bucket_np(near)
    cmp_end = np.arange(n_blk) * CMP_STRIDE + CMP_LEN - 1
    cmp_bucket = _t5_bucket_np(np.arange(t)[:, None] - cmp_end[None, :])
    ci = np.arange(n_blk) * CMP_STRIDE
    sj = np.arange(n_slc) * SLC_LEN
    ov = ((ci[:, None] < sj[None, :] + SLC_LEN) & (ci[:, None] + CMP_LEN > sj[None, :])).astype(np.float32)
    ov[n_cmp:, :] = 0.0
    overlap = np.zeros((LANES, LANES), np.float32)
    overlap[:n_blk, :n_slc] = ov
    expand = np.zeros((LANES, t), np.float32)
    expand[np.arange(t) // SLC_LEN, np.arange(t)] = 1.0
    return near_bucket, cmp_bucket, overlap, expand


def _nsa_w_in_columns():
    kv0 = NSA_HEADS * NSA_DH
    kvw = NSA_GROUPS * NSA_DH
    cols = list(range(kv0))
    for br in range(3):
        for g in range(NSA_GROUPS):
            k_src = kv0 + (2 * br) * kvw + g * NSA_DH
            v_src = kv0 + (2 * br + 1) * kvw + g * NSA_DH
            cols += list(range(k_src, k_src + NSA_DH)) + list(range(v_src, v_src + NSA_DH))
    gate0 = kv0 + 6 * kvw
    for g in range(NSA_GROUPS):
        blk = [-1] * LANES
        for br in range(3):
            for p in range(NSA_HPG):
                blk[br * NSA_HPG + p] = gate0 + br * NSA_HEADS + g * NSA_HPG + p
        cols += blk
    return np.asarray(cols, np.int32)


def _retention_layer(x2, b, t, norm_g, w_in, w_out):
    h = _norm_matmul(x2, norm_g, w_in.astype(BF16), tn=1024)
    o = _retention_core(h.reshape(b, t, -1))
    return _matmul_res(o.reshape(b * t, -1), w_out.astype(BF16), x2)


def _nsa_layer(x2, b, t, norm_g, w_in, w_out, q_gain, k_gain, cmp_pos, cmp_w1, cmp_w2, bias_tiles,
               cmp_bias, overlap, expand):
    cols = _nsa_w_in_columns()
    w_k = jnp.where(cols[None, :] >= 0, jnp.take(w_in, np.maximum(cols, 0), axis=1), 0.0).astype(BF16)
    h3 = _norm_matmul(x2, norm_g, w_k, tn=1024).reshape(b, t, NSA_COLS)

    dh = NSA_DH
    w1 = cmp_w1.reshape(2, CMP_LEN, dh, CMP_HID)
    zero1 = jnp.zeros((CMP_LEN, dh, CMP_HID), F32)
    w1cat = jnp.concatenate([jnp.concatenate([w1[0], zero1], axis=-1),
                             jnp.concatenate([zero1, w1[1]], axis=-1)], axis=1).astype(BF16)
    zero2 = jnp.zeros((CMP_HID, dh), F32)
    w2cat = jnp.concatenate([jnp.concatenate([cmp_w2[0], zero2], axis=-1),
                             jnp.concatenate([zero2, cmp_w2[1]], axis=-1)], axis=0).astype(BF16)
    pos = jnp.concatenate([cmp_pos[0], cmp_pos[1]], axis=-1)
    ones = jnp.ones((dh,), F32)
    kvc = _compress(h3, pos, w1cat, w2cat, jnp.concatenate([k_gain[0], ones])[None, :])

    qgain = jnp.tile(q_gain * (dh ** -0.5), NSA_HPG)[None, :]
    kgain = jnp.stack([jnp.concatenate([k_gain[1], ones]), jnp.concatenate([k_gain[2], ones])])
    o = _nsa_attention(h3, kvc, qgain, kgain, bias_tiles, cmp_bias, overlap, expand)
    return _matmul_res(o.reshape(b * t, -1), w_out.astype(BF16), x2)


def kernel(x, ffn1_norm, ffn1_w_gu, ffn1_w_down, mix_norm, ffn2_norm, ffn2_w_gu, ffn2_w_down,
           ret_w_in, ret_w_out, nsa_w_in, nsa_w_out, nsa_q_gain, nsa_k_gain,
           nsa_cmp_pos, nsa_cmp_w1, nsa_cmp_w2, rel_bias):
    b, t, d = x.shape
    depth = ffn1_norm.shape[0]
    d_ff = ffn1_w_down.shape[1]
    tf = 256
    nf = d_ff // tf

    def ffn_weights(w_gu, w_down):
        wgu = w_gu.astype(BF16).reshape(d, 2, nf, tf).transpose(1, 2, 0, 3)
        return wgu, w_down.astype(BF16).reshape(nf, tf, d)

    near_bucket, cmp_bucket, overlap, expand = _nsa_constants(t)
    tbl = rel_bias.astype(F32)
    bias_tiles = (tbl[near_bucket] - tbl[NUM_BUCKETS - 1]).transpose(3, 0, 1, 2)
    bias_tiles = bias_tiles.reshape(NSA_GROUPS, NSA_HPG, 2, TQ, TQ)
    cmp_bias = tbl[cmp_bucket].transpose(2, 0, 1).reshape(NSA_GROUPS, NSA_HPG, t, t // CMP_STRIDE)
    overlap = jnp.asarray(overlap, BF16)
    expand = jnp.asarray(expand, BF16)

    x2 = x.reshape(b * t, d)
    for layer in range(depth):
        x2 = _ffn(x2, ffn1_norm[layer][None, :], *ffn_weights(ffn1_w_gu[layer], ffn1_w_down[layer]))
        j = layer // 2
        if layer % 2 == 0:
            x2 = _retention_layer(x2, b, t, mix_norm[layer][None, :], ret_w_in[j], ret_w_out[j])
        else:
            x2 = _nsa_layer(x2, b, t, mix_norm[layer][None, :], nsa_w_in[j], nsa_w_out[j], nsa_q_gain[j],
                            nsa_k_gain[j], nsa_cmp_pos[j], nsa_cmp_w1[j], nsa_cmp_w2[j], bias_tiles,
                            cmp_bias, overlap, expand)
        x2 = _ffn(x2, ffn2_norm[layer][None, :], *ffn_weights(ffn2_w_gu[layer], ffn2_w_down[layer]))
    return x2.reshape(b, t, d)
```

```python
import functools

import numpy as np
import jax
import jax.numpy as jnp
from jax import lax
from jax.experimental import pallas as pl
from jax.experimental.pallas import tpu as pltpu

F32 = jnp.float32
BF16 = jnp.bfloat16

RMS_EPS = 1e-6
NEG = -1e30
SEL_BIG = 1e9

LANES = 128
VMEM_LIMIT = 56 * 1024 * 1024

RET_HEADS = 4
RET_DK = 256
RET_DV = 512
RET_CHUNK = 128
RET_ROPE_BASE = 10000.0

NSA_HEADS = 16
NSA_GROUPS = 4
NSA_HPG = 4
NSA_DH = 64
CMP_LEN = 32
CMP_STRIDE = 16
CMP_HID = 256
SLC_LEN = 64
N_SEL = 8
WINDOW = 512
NUM_BUCKETS = 32
MAX_DISTANCE = 128
TQ = 128
NSA_COLS = 1024 + 3 * 4 * 128 + 4 * 128


def _dot(a, b):
    return jnp.dot(a, b, preferred_element_type=F32)


def _dot_nt(a, b):
    return lax.dot_general(a, b, (((1,), (1,)), ((), ())), preferred_element_type=F32)


def _dot_tn(a, b):
    return lax.dot_general(a, b, (((0,), (0,)), ((), ())), preferred_element_type=F32)


def _rms(x, g):
    ms = jnp.mean(x * x, axis=-1, keepdims=True)
    return x * lax.rsqrt(ms + RMS_EPS) * g


def _cparams(sem):
    return pltpu.CompilerParams(dimension_semantics=sem, vmem_limit_bytes=VMEM_LIMIT)


def _resident(shape):
    n = len(shape)
    return pl.BlockSpec(shape, lambda *_: (0,) * n, pipeline_mode=pl.Buffered(1))


def _ffn_kernel(x_ref, g_ref, wgu_ref, wd_ref, o_ref, xn_ref, acc_ref, *, nf):
    x = x_ref[...]
    xn_ref[...] = _rms(x, g_ref[...]).astype(BF16)
    acc_ref[...] = jnp.zeros_like(acc_ref)

    def body(j, carry):
        xn = xn_ref[...]
        a = _dot(xn, wgu_ref[0, j])
        b = _dot(xn, wgu_ref[1, j])
        h = (a * jax.nn.sigmoid(a) * b).astype(BF16)
        acc_ref[...] += _dot(h, wd_ref[j])
        return carry

    lax.fori_loop(0, nf, body, 0)
    o_ref[...] = x_ref[...] + 0.5 * acc_ref[...]


def _ffn(x2, g, wgu, wd, tm=512):
    n, d = x2.shape
    _, nf, _, tf = wgu.shape
    return pl.pallas_call(
        functools.partial(_ffn_kernel, nf=nf),
        grid=(n // tm,),
        in_specs=[
            pl.BlockSpec((tm, d), lambda i: (i, 0)),
            _resident((1, d)),
            _resident((2, nf, d, tf)),
            _resident((nf, tf, d)),
        ],
        out_specs=pl.BlockSpec((tm, d), lambda i: (i, 0)),
        out_shape=jax.ShapeDtypeStruct((n, d), F32),
        scratch_shapes=[pltpu.VMEM((tm, d), BF16), pltpu.VMEM((tm, d), F32)],
        compiler_params=_cparams(("parallel",)),
    )(x2, g, wgu, wd)


def _norm_matmul_kernel(x_ref, g_ref, w_ref, o_ref, *, tn):
    xn = _rms(x_ref[...], g_ref[...]).astype(BF16)
    for j in range(w_ref.shape[1] // tn):
        o_ref[:, j * tn:(j + 1) * tn] = _dot(xn, w_ref[:, j * tn:(j + 1) * tn]).astype(o_ref.dtype)


def _norm_matmul(x2, g, w, tn, tm=512):
    n, d = x2.shape
    nout = w.shape[1]
    return pl.pallas_call(
        functools.partial(_norm_matmul_kernel, tn=tn),
        grid=(n // tm,),
        in_specs=[pl.BlockSpec((tm, d), lambda i: (i, 0)), _resident((1, d)), _resident((d, nout))],
        out_specs=pl.BlockSpec((tm, nout), lambda i: (i, 0)),
        out_shape=jax.ShapeDtypeStruct((n, nout), BF16),
        compiler_params=_cparams(("parallel",)),
    )(x2, g, w)


def _matmul_res_kernel(o_ref, w_ref, x_ref, y_ref):
    y_ref[...] = x_ref[...] + _dot(o_ref[...], w_ref[...])


def _matmul_res(o2, w, x2, tm=512):
    n, k = o2.shape
    d = w.shape[1]
    return pl.pallas_call(
        _matmul_res_kernel,
        grid=(n // tm,),
        in_specs=[pl.BlockSpec((tm, k), lambda i: (i, 0)), _resident((k, d)),
                  pl.BlockSpec((tm, d), lambda i: (i, 0))],
        out_specs=pl.BlockSpec((tm, d), lambda i: (i, 0)),
        out_shape=jax.ShapeDtypeStruct((n, d), F32),
        compiler_params=_cparams(("parallel",)),
    )(o2, w, x2)


def _ret_kernel(cd_ref, q_ref, k_ref, v_ref, g_ref, cos_ref, sin_ref, intra_ref, qd_ref, kd_ref,
                o_ref, s_ref, *, n_chunks):
    c_len = RET_CHUNK
    half = RET_DK // 2
    cd = cd_ref[pl.program_id(1)]
    intra = intra_ref[0]
    qd = qd_ref[0]
    kd = kd_ref[0]
    s_ref[...] = jnp.zeros_like(s_ref)

    def chunk(c, carry):
        r0 = pl.multiple_of(c * c_len, c_len)
        cos = cos_ref[pl.ds(r0, c_len), :]
        sin = sin_ref[pl.ds(r0, c_len), :]

        def rot(x):
            x1, x2 = x[:, :half], x[:, half:]
            return jnp.concatenate([x1 * cos - x2 * sin, x1 * sin + x2 * cos], axis=-1)

        qr = rot(q_ref[0, pl.ds(r0, c_len), :].astype(F32))
        kr = rot(k_ref[0, pl.ds(r0, c_len), :].astype(F32)) * (RET_DK ** -0.5)
        vc = v_ref[0, pl.ds(r0, c_len), :]
        s = _dot_nt(qr.astype(BF16), kr.astype(BF16)) * intra
        state = s_ref[...]
        o = _dot(s.astype(BF16), vc) + _dot((qr * qd).astype(BF16), state.astype(BF16))
        s_ref[...] = cd * state + _dot_tn((kr * kd).astype(BF16), vc)
        o = o * lax.rsqrt(jnp.mean(o * o, axis=-1, keepdims=True) + RMS_EPS)
        gate = g_ref[0, pl.ds(r0, c_len), :].astype(F32)
        o_ref[0, pl.ds(r0, c_len), :] = (gate * jax.nn.sigmoid(gate) * o).astype(o_ref.dtype)
        return carry

    lax.fori_loop(0, n_chunks, chunk, 0)


def _retention_core(h3):
    b, t, _ = h3.shape
    hh, dk, dv, c = RET_HEADS, RET_DK, RET_DV, RET_CHUNK
    half = dk // 2
    pos = jnp.arange(t, dtype=F32)
    inv_freq = RET_ROPE_BASE ** (-jnp.arange(half, dtype=F32) / half)
    ang = pos[:, None] * inv_freq[None, :]
    cos, sin = jnp.cos(ang), jnp.sin(ang)
    log_gamma = jnp.log(1.0 - 2.0 ** (-5.0 - jnp.arange(hh, dtype=F32)))
    i = jnp.arange(c)
    diff = i[:, None] - i[None, :]
    intra = jnp.where(diff >= 0, jnp.exp(log_gamma[:, None, None] * jnp.maximum(diff, 0)), 0.0)
    q_dec = jnp.exp(log_gamma[:, None] * (i + 1))[:, :, None]
    k_dec = jnp.exp(log_gamma[:, None] * (c - 1 - i))[:, :, None]
    chunk_dec = jnp.exp(log_gamma * c)
    nq = hh * dk // dk
    return pl.pallas_call(
        functools.partial(_ret_kernel, n_chunks=t // c),
        grid=(b, hh),
        in_specs=[
            pl.BlockSpec(memory_space=pltpu.SMEM),
            pl.BlockSpec((1, t, dk), lambda bi, hi: (bi, 0, hi)),
            pl.BlockSpec((1, t, dk), lambda bi, hi: (bi, 0, nq + hi)),
            pl.BlockSpec((1, t, dv), lambda bi, hi: (bi, 0, nq + hi)),
            pl.BlockSpec((1, t, dv), lambda bi, hi: (bi, 0, 2 * nq + hi)),
            _resident((t, half)),
            _resident((t, half)),
            pl.BlockSpec((1, c, c), lambda bi, hi: (hi, 0, 0)),
            pl.BlockSpec((1, c, 1), lambda bi, hi: (hi, 0, 0)),
            pl.BlockSpec((1, c, 1), lambda bi, hi: (hi, 0, 0)),
        ],
        out_specs=pl.BlockSpec((1, t, dv), lambda bi, hi: (bi, 0, hi)),
        out_shape=jax.ShapeDtypeStruct((b, t, hh * dv), BF16),
        scratch_shapes=[pltpu.VMEM((dk, dv), F32)],
        compiler_params=_cparams(("parallel", "arbitrary")),
    )(chunk_dec, h3, h3, h3, h3, cos, sin, intra, q_dec, k_dec)


def _k_headnorm(x, gain_row):
    klane = lax.broadcasted_iota(jnp.int32, x.shape, 1) < NSA_DH
    ms = jnp.sum(jnp.where(klane, x * x, 0.0), axis=-1, keepdims=True) * (1.0 / NSA_DH)
    return jnp.where(klane, x * lax.rsqrt(ms + RMS_EPS) * gain_row, x)


def _cmp_kernel(a_ref, pos_ref, w1_ref, w2_ref, kg_ref, o_ref, af_ref, *, t):
    n_blk = t // CMP_STRIDE
    af_ref[0:t, :] = a_ref[0].astype(F32)
    af_ref[t:t + CMP_LEN, :] = jnp.zeros((CMP_LEN, LANES), F32)
    acc = jnp.zeros((n_blk, 2 * CMP_HID), F32)
    for l in range(CMP_LEN):
        rows = af_ref[pl.ds(l, n_blk, stride=CMP_STRIDE), :]
        acc = acc + _dot((rows + pos_ref[l:l + 1, :]).astype(BF16), w1_ref[l])
    hid = jax.nn.gelu(acc, approximate=True)
    out = _dot(hid.astype(BF16), w2_ref[...])
    o_ref[0, 0] = _k_headnorm(out, kg_ref[...]).astype(o_ref.dtype)


def _compress(h3, pos, w1cat, w2cat, kgain_row):
    b, t, _ = h3.shape
    n_blk = t // CMP_STRIDE
    return pl.pallas_call(
        functools.partial(_cmp_kernel, t=t),
        grid=(b, NSA_GROUPS),
        in_specs=[
            pl.BlockSpec((1, t, LANES), lambda bi, gi: (bi, 0, 8 + gi)),
            _resident((CMP_LEN, LANES)),
            _resident((CMP_LEN, LANES, 2 * CMP_HID)),
            _resident((2 * CMP_HID, LANES)),
            _resident((1, LANES)),
        ],
        out_specs=pl.BlockSpec((1, 1, n_blk, LANES), lambda bi, gi: (bi, gi, 0, 0)),
        out_shape=jax.ShapeDtypeStruct((b, NSA_GROUPS, n_blk, LANES), BF16),
        scratch_shapes=[pltpu.VMEM((t + CMP_LEN, LANES), F32)],
        compiler_params=_cparams(("parallel", "parallel")),
    )(h3, pos, w1cat, w2cat, kgain_row)


def _softmax_rows(s3, kv):
    hpg, tq, w = s3.shape
    m = jnp.max(s3, axis=-1, keepdims=True)
    p = jnp.exp(s3 - m)
    l = jnp.sum(p, axis=-1, keepdims=True)
    return m, l, _dot(p.reshape(hpg * tq, w).astype(BF16), kv)


def _nsa_attn_kernel(q_ref, kvc_ref, kvs_ref, kvw_ref, gate_ref, qg_ref, kg_ref, bnd_ref, bc_ref,
                     ov_ref, ex_ref, o_ref, ks_ref, kw_ref, am_ref, *, t):
    hpg = NSA_HPG
    qt = pl.program_id(2)
    q0 = pl.multiple_of(qt * TQ, TQ)
    lane = lax.broadcasted_iota(jnp.int32, (1, LANES), 1)
    klane = lane < NSA_DH

    @pl.when(qt == 0)
    def _():
        rows = 256
        ks_ref[0:TQ, :] = jnp.zeros((TQ, LANES), BF16)
        kw_ref[0:WINDOW, :] = jnp.zeros((WINDOW, LANES), BF16)
        am_ref[:, 0:TQ] = jnp.full((TQ, TQ), NEG, F32)

        def norm_chunk(c, carry):
            r0 = pl.multiple_of(c * rows, rows)
            ks_ref[pl.ds(TQ + r0, rows), :] = _k_headnorm(
                kvs_ref[0, pl.ds(r0, rows), :].astype(F32), kg_ref[0:1, :]).astype(BF16)
            kw_ref[pl.ds(WINDOW + r0, rows), :] = _k_headnorm(
                kvw_ref[0, pl.ds(r0, rows), :].astype(F32), kg_ref[1:2, :]).astype(BF16)
            return carry

        lax.fori_loop(0, t // rows, norm_chunk, 0)

    xq = q_ref[0].astype(F32)
    lane_q = lax.broadcasted_iota(jnp.int32, (1, hpg * NSA_DH), 1)
    xq2 = xq * xq
    qz = []
    for p in range(hpg):
        seg = (lane_q >= NSA_DH * p) & (lane_q < NSA_DH * (p + 1))
        ms = jnp.sum(jnp.where(seg, xq2, 0.0), axis=-1, keepdims=True) * (1.0 / NSA_DH)
        c0 = LANES * (p // 2)
        half = xq[:, c0:c0 + LANES] * lax.rsqrt(ms + RMS_EPS) * qg_ref[:, c0:c0 + LANES]
        if p % 2 == 1:
            half = pltpu.roll(half, NSA_DH, 1)
        qz.append(jnp.where(klane, half, 0.0).astype(BF16))
    q4 = jnp.concatenate(qz, axis=0)

    iq = lax.broadcasted_iota(jnp.int32, (TQ, LANES), 0)
    jj = lax.broadcasted_iota(jnp.int32, (TQ, LANES), 1)

    valid_c = ((CMP_STRIDE * jj + (CMP_LEN - 1)) <= (q0 + iq))[None]
    kvc = kvc_ref[0, 0]
    per_tile = TQ // CMP_STRIDE
    shift = (qt * per_tile + per_tile) % LANES
    cbias = pltpu.roll(bc_ref[0].reshape(hpg * TQ, LANES), shift, 1).reshape(hpg, TQ, LANES)
    s = jnp.where(valid_c, _dot_nt(q4, kvc).reshape(hpg, TQ, LANES) + cbias, NEG)
    m = jnp.max(s, axis=-1, keepdims=True)
    e = jnp.where(valid_c, jnp.exp(s - m), 0.0)
    l = jnp.sum(e, axis=-1, keepdims=True)
    pc = e / jnp.where(l > 0.0, l, 1.0)
    o_cmp = _dot(pc.reshape(hpg * TQ, LANES).astype(BF16), kvc)
    psum = pc[0]
    for p in range(1, hpg):
        psum = psum + pc[p]
    p_hi = psum.astype(BF16)
    p_lo = (psum - p_hi.astype(F32)).astype(BF16)
    imp = _dot(p_hi, ov_ref[...]) + _dot(p_lo, ov_ref[...])

    n_slc = t // SLC_LEN
    imp_t = imp.T[:n_slc, :]
    jrow = lax.broadcasted_iota(jnp.int32, (n_slc, TQ), 0)
    cur = (q0 + lax.broadcasted_iota(jnp.int32, (n_slc, TQ), 1)) // SLC_LEN
    forced = (jrow == 0) | (jrow == cur) | (jrow == cur - 1)
    vals = jnp.where(forced, SEL_BIG, jnp.where(jrow <= cur, imp_t, -SEL_BIG))
    cnt = jnp.zeros((n_slc, TQ), F32)
    for jp in range(n_slc):
        row = vals[jp:jp + 1, :]
        ahead = (row > vals) | ((row == vals) & (jrow > jp))
        cnt = cnt + jnp.where(ahead, 1.0, 0.0)
    sel_t = jnp.where(cnt < float(N_SEL), 1.0, 0.0)
    sel = jnp.concatenate([sel_t, jnp.zeros((LANES - n_slc, TQ), F32)], axis=0).T
    am_ref[:, TQ:] = (_dot(sel.astype(BF16), ex_ref[...]) - 1.0) * 1e30

    causal = jnp.where(jj <= iq, 0.0, NEG)
    has_near = jnp.where(qt >= 1, jnp.zeros((TQ, LANES), F32), NEG)
    bnd = bnd_ref[0]

    kv_nd = ks_ref[pl.ds(q0, 2 * TQ), :]
    add_nd = am_ref[:, pl.ds(q0, 2 * TQ)] + jnp.concatenate([has_near, causal], axis=1)
    s_nd = _dot_nt(q4, kv_nd).reshape(hpg, TQ, 2 * TQ) + add_nd[None] + bnd
    state = _softmax_rows(s_nd, kv_nd)

    ck = 4 * TQ
    far_end = (qt - 1) * TQ
    col = lax.broadcasted_iota(jnp.int32, (TQ, ck), 1)

    def sel_far(c, carry):
        m_old, l_old, acc = carry
        r0 = pl.multiple_of(TQ + c * ck, TQ)
        kv = ks_ref[pl.ds(r0, ck), :]
        add = jnp.where(col + c * ck < far_end, am_ref[:, pl.ds(r0, ck)], NEG)
        s3 = _dot_nt(q4, kv).reshape(hpg, TQ, ck) + add[None]
        m_new = jnp.maximum(m_old, jnp.max(s3, axis=-1, keepdims=True))
        alpha = jnp.exp(m_old - m_new)
        p = jnp.exp(s3 - m_new)
        l_new = alpha * l_old + jnp.sum(p, axis=-1, keepdims=True)
        pv = _dot(p.reshape(hpg * TQ, ck).astype(BF16), kv)
        acc = (alpha * acc.reshape(hpg, TQ, LANES)).reshape(hpg * TQ, LANES) + pv
        return m_new, l_new, acc

    _, l_sel, acc_sel = lax.fori_loop(0, (qt + 2) // 4, sel_far, state)

    ww = WINDOW + TQ
    kv_w = kw_ref[pl.ds(q0, ww), :]
    iw = lax.broadcasted_iota(jnp.int32, (TQ, ww), 0)
    jw = lax.broadcasted_iota(jnp.int32, (TQ, ww), 1)
    valid_w = (jw > iw) & (jw <= iw + WINDOW) & (jw + (q0 - WINDOW) >= 0)
    s_w = _dot_nt(q4, kv_w).reshape(hpg, TQ, ww) + jnp.where(valid_w, 0.0, NEG)[None]
    s_w = jnp.concatenate([s_w[:, :, :ww - 2 * TQ], s_w[:, :, ww - 2 * TQ:] + bnd], axis=-1)
    _, l_win, acc_win = _softmax_rows(s_w, kv_w)

    gl = jax.nn.sigmoid(gate_ref[0].astype(F32))

    def gate(br):
        return jnp.stack([gl[:, br * hpg + p:br * hpg + p + 1] for p in range(hpg)])

    out = (gate(0) * o_cmp.reshape(hpg, TQ, LANES)
           + (gate(1) / l_sel) * acc_sel.reshape(hpg, TQ, LANES)
           + (gate(2) / l_win) * acc_win.reshape(hpg, TQ, LANES))
    packed = [jnp.where(klane, pltpu.roll(out[2 * i], NSA_DH, 1), out[2 * i + 1]) for i in range(hpg // 2)]
    o_ref[0] = jnp.concatenate(packed, axis=-1).astype(o_ref.dtype)


def _nsa_attention(h3, kvc, qgain, kgain, bias_nd, bias_cmp, overlap, expand):
    b, t, _ = h3.shape
    g, hpg = NSA_GROUPS, NSA_HPG
    n_blk = t // CMP_STRIDE
    qw = hpg * NSA_DH
    return pl.pallas_call(
        functools.partial(_nsa_attn_kernel, t=t),
        grid=(b, g, t // TQ),
        in_specs=[
            pl.BlockSpec((1, TQ, qw), lambda bi, gi, qi: (bi, qi, gi)),
            pl.BlockSpec((1, 1, n_blk, LANES), lambda bi, gi, qi: (bi, gi, 0, 0)),
            pl.BlockSpec((1, t, LANES), lambda bi, gi, qi: (bi, 0, 12 + gi)),
            pl.BlockSpec((1, t, LANES), lambda bi, gi, qi: (bi, 0, 16 + gi)),
            pl.BlockSpec((1, TQ, LANES), lambda bi, gi, qi: (bi, qi, 20 + gi)),
            pl.BlockSpec((1, qw), lambda bi, gi, qi: (0, 0)),
            pl.BlockSpec((2, LANES), lambda bi, gi, qi: (0, 0)),
            pl.BlockSpec((1, hpg, TQ, 2 * TQ), lambda bi, gi, qi: (gi, 0, 0, 0)),
            pl.BlockSpec((1, hpg, TQ, LANES), lambda bi, gi, qi: (gi, 0, 0, 0)),
            pl.BlockSpec((LANES, LANES), lambda bi, gi, qi: (0, 0)),
            pl.BlockSpec((LANES, t), lambda bi, gi, qi: (0, 0)),
        ],
        out_specs=pl.BlockSpec((1, TQ, qw), lambda bi, gi, qi: (bi, qi, gi)),
        out_shape=jax.ShapeDtypeStruct((b, t, NSA_HEADS * NSA_DH), BF16),
        scratch_shapes=[pltpu.VMEM((TQ + t, LANES), BF16), pltpu.VMEM((WINDOW + t, LANES), BF16),
                        pltpu.VMEM((TQ, TQ + t), F32)],
        compiler_params=_cparams(("arbitrary", "arbitrary", "arbitrary")),
    )(h3, kvc, h3, h3, h3, qgain, kgain, bias_nd, bias_cmp, overlap, expand)


def _t5_bucket_np(dist):
    n = np.maximum(dist, 0)
    max_exact = NUM_BUCKETS // 2
    nf = np.maximum(n, max_exact).astype(np.float32)
    large = max_exact + (np.log(nf / max_exact) / np.float32(np.log(MAX_DISTANCE / max_exact))
                         * (NUM_BUCKETS - max_exact)).astype(np.int32)
    large = np.minimum(large, NUM_BUCKETS - 1)
    return np.where(n < max_exact, n, large)


def _nsa_constants(t):
    n_cmp = (t - CMP_LEN) // CMP_STRIDE + 1
    n_slc = t // SLC_LEN
    n_blk = t // CMP_STRIDE
    per_tile = TQ // CMP_STRIDE
    assert n_blk <= LANES and per_tile * (t // TQ - 1) <= LANES - per_tile
    iq = np.arange(TQ)
    rel = iq[:, None] - iq[None, :]
    assert _t5_bucket_np(np.array([TQ + 1]))[0] == NUM_BUCKETS - 1
    nd_bucket = _t5_bucket_np(np.concatenate([TQ + rel, rel], axis=1))
    blk_off = np.arange(LANES) - (LANES - per_tile)
    bc_bucket = _t5_bucket_np(iq[:, None] - CMP_STRIDE * blk_off[None, :] - (CMP_LEN - 1))
    ci = np.arange(n_blk) * CMP_STRIDE
    sj = np.arange(n_slc) * SLC_LEN
    ov = ((ci[:, None] < sj[None, :] + SLC_LEN) & (ci[:, None] + CMP_LEN > sj[None, :])).astype(np.float32)
    ov[n_cmp:, :] = 0.0
    overlap = np.zeros((LANES, LANES), np.float32)
    overlap[:n_blk, :n_slc] = ov
    expand = np.zeros((LANES, t), np.float32)
    expand[np.arange(t) // SLC_LEN, np.arange(t)] = 1.0
    return nd_bucket, bc_bucket, overlap, expand


def _nsa_w_in_columns():
    kv0 = NSA_HEADS * NSA_DH
    kvw = NSA_GROUPS * NSA_DH
    cols = list(range(kv0))
    for br in range(3):
        for g in range(NSA_GROUPS):
            k_src = kv0 + (2 * br) * kvw + g * NSA_DH
            v_src = kv0 + (2 * br + 1) * kvw + g * NSA_DH
            cols += list(range(k_src, k_src + NSA_DH)) + list(range(v_src, v_src + NSA_DH))
    gate0 = kv0 + 6 * kvw
    for g in range(NSA_GROUPS):
        blk = [-1] * LANES
        for br in range(3):
            for p in range(NSA_HPG):
                blk[br * NSA_HPG + p] = gate0 + br * NSA_HEADS + g * NSA_HPG + p
        cols += blk
    return np.asarray(cols, np.int32)


def _retention_layer(x2, b, t, norm_g, w_in, w_out):
    h = _norm_matmul(x2, norm_g, w_in.astype(BF16), tn=1024)
    o = _retention_core(h.reshape(b, t, -1))
    return _matmul_res(o.reshape(b * t, -1), w_out.astype(BF16), x2)


def _nsa_layer(x2, b, t, norm_g, w_in, w_out, q_gain, k_gain, cmp_pos, cmp_w1, cmp_w2, bias_nd,
               bias_cmp, overlap, expand):
    cols = _nsa_w_in_columns()
    w_k = jnp.where(cols[None, :] >= 0, jnp.take(w_in, np.maximum(cols, 0), axis=1), 0.0).astype(BF16)
    h3 = _norm_matmul(x2, norm_g, w_k, tn=1024).reshape(b, t, NSA_COLS)

    dh = NSA_DH
    w1 = cmp_w1.reshape(2, CMP_LEN, dh, CMP_HID)
    zero1 = jnp.zeros((CMP_LEN, dh, CMP_HID), F32)
    w1cat = jnp.concatenate([jnp.concatenate([w1[0], zero1], axis=-1),
                             jnp.concatenate([zero1, w1[1]], axis=-1)], axis=1).astype(BF16)
    zero2 = jnp.zeros((CMP_HID, dh), F32)
    w2cat = jnp.concatenate([jnp.concatenate([cmp_w2[0], zero2], axis=-1),
                             jnp.concatenate([zero2, cmp_w2[1]], axis=-1)], axis=0).astype(BF16)
    pos = jnp.concatenate([cmp_pos[0], cmp_pos[1]], axis=-1)
    ones = jnp.ones((dh,), F32)
    kvc = _compress(h3, pos, w1cat, w2cat, jnp.concatenate([k_gain[0], ones])[None, :])

    qgain = jnp.tile(q_gain * (dh ** -0.5), NSA_HPG)[None, :]
    kgain = jnp.stack([jnp.concatenate([k_gain[1], ones]), jnp.concatenate([k_gain[2], ones])])
    o = _nsa_attention(h3, kvc, qgain, kgain, bias_nd, bias_cmp, overlap, expand)
    return _matmul_res(o.reshape(b * t, -1), w_out.astype(BF16), x2)


def kernel(x, ffn1_norm, ffn1_w_gu, ffn1_w_down, mix_norm, ffn2_norm, ffn2_w_gu, ffn2_w_down,
           ret_w_in, ret_w_out, nsa_w_in, nsa_w_out, nsa_q_gain, nsa_k_gain,
           nsa_cmp_pos, nsa_cmp_w1, nsa_cmp_w2, rel_bias):
    b, t, d = x.shape
    depth = ffn1_norm.shape[0]
    d_ff = ffn1_w_down.shape[1]
    tf = 256
    nf = d_ff // tf

    def ffn_weights(w_gu, w_down):
        wgu = w_gu.astype(BF16).reshape(d, 2, nf, tf).transpose(1, 2, 0, 3)
        return wgu, w_down.astype(BF16).reshape(nf, tf, d)

    nd_bucket, bc_bucket, overlap, expand = _nsa_constants(t)
    tbl = rel_bias.astype(F32)
    bias_nd = (tbl[nd_bucket] - tbl[NUM_BUCKETS - 1]).transpose(2, 0, 1)
    bias_nd = bias_nd.reshape(NSA_GROUPS, NSA_HPG, TQ, 2 * TQ)
    bias_cmp = tbl[bc_bucket].transpose(2, 0, 1).reshape(NSA_GROUPS, NSA_HPG, TQ, LANES)
    overlap = jnp.asarray(overlap, BF16)
    expand = jnp.asarray(expand, BF16)

    x2 = x.reshape(b * t, d)
    for layer in range(depth):
        x2 = _ffn(x2, ffn1_norm[layer][None, :], *ffn_weights(ffn1_w_gu[layer], ffn1_w_down[layer]))
        j = layer // 2
        if layer % 2 == 0:
            x2 = _retention_layer(x2, b, t, mix_norm[layer][None, :], ret_w_in[j], ret_w_out[j])
        else:
            x2 = _nsa_layer(x2, b, t, mix_norm[layer][None, :], nsa_w_in[j], nsa_w_out[j], nsa_q_gain[j],
                            nsa_k_gain[j], nsa_cmp_pos[j], nsa_cmp_w1[j], nsa_cmp_w2[j], bias_nd,
                            bias_cmp, overlap, expand)
        x2 = _ffn(x2, ffn2_norm[layer][None, :], *ffn_weights(ffn2_w_gu[layer], ffn2_w_down[layer]))
    return x2.reshape(b, t, d)
```

```python
import functools

import numpy as np
import jax
import jax.numpy as jnp
from jax import lax
from jax.experimental import pallas as pl
from jax.experimental.pallas import tpu as pltpu

F32 = jnp.float32
BF16 = jnp.bfloat16

RMS_EPS = 1e-6
NEG = -1e30
SEL_BIG = 1e9
LOG2E = 1.4426950408889634

LANES = 128
VMEM_LIMIT = 56 * 1024 * 1024

RET_HEADS = 4
RET_DK = 256
RET_DV = 512
RET_CHUNK = 128
RET_ROPE_BASE = 10000.0

NSA_HEADS = 16
NSA_GROUPS = 4
NSA_HPG = 4
NSA_DH = 64
CMP_LEN = 32
CMP_STRIDE = 16
CMP_HID = 256
SLC_LEN = 64
N_SEL = 8
WINDOW = 512
NUM_BUCKETS = 32
MAX_DISTANCE = 128
TQ = 256
CK = 512
NSA_COLS = 1024 + 3 * 4 * 128 + 4 * 128


def _dot(a, b):
    return jnp.dot(a, b, preferred_element_type=F32)


def _dot_nt(a, b):
    return lax.dot_general(a, b, (((1,), (1,)), ((), ())), preferred_element_type=F32)


def _dot_tn(a, b):
    return lax.dot_general(a, b, (((0,), (0,)), ((), ())), preferred_element_type=F32)


def _rms(x, g):
    ms = jnp.mean(x * x, axis=-1, keepdims=True)
    return x * lax.rsqrt(ms + RMS_EPS) * g


def _cparams(sem):
    return pltpu.CompilerParams(dimension_semantics=sem, vmem_limit_bytes=VMEM_LIMIT)


def _resident(shape):
    n = len(shape)
    return pl.BlockSpec(shape, lambda *_: (0,) * n, pipeline_mode=pl.Buffered(1))


def _ffn_kernel(x_ref, g_ref, wgu_ref, wd_ref, o_ref, *, nf):
    x = x_ref[...]
    xn = _rms(x, g_ref[...]).astype(BF16)
    acc = None
    for j in range(nf):
        a = _dot(xn, wgu_ref[0, j])
        b = _dot(xn, wgu_ref[1, j])
        h = (a * jax.nn.sigmoid(a) * b).astype(BF16)
        part = _dot(h, wd_ref[j])
        acc = part if acc is None else acc + part
    o_ref[...] = x + 0.5 * acc


def _ffn(x2, g, wgu, wd, tm=512):
    n, d = x2.shape
    _, nf, _, tf = wgu.shape
    return pl.pallas_call(
        functools.partial(_ffn_kernel, nf=nf),
        grid=(n // tm,),
        in_specs=[
            pl.BlockSpec((tm, d), lambda i: (i, 0)),
            _resident((1, d)),
            _resident((2, nf, d, tf)),
            _resident((nf, tf, d)),
        ],
        out_specs=pl.BlockSpec((tm, d), lambda i: (i, 0)),
        out_shape=jax.ShapeDtypeStruct((n, d), F32),
        compiler_params=_cparams(("parallel",)),
    )(x2, g, wgu, wd)


def _norm_matmul_kernel(x_ref, g_ref, w_ref, o_ref, *, tn):
    xn = _rms(x_ref[...], g_ref[...]).astype(BF16)
    for j in range(w_ref.shape[1] // tn):
        o_ref[:, j * tn:(j + 1) * tn] = _dot(xn, w_ref[:, j * tn:(j + 1) * tn]).astype(o_ref.dtype)


def _norm_matmul(x2, g, w, tn, tm=512):
    n, d = x2.shape
    nout = w.shape[1]
    return pl.pallas_call(
        functools.partial(_norm_matmul_kernel, tn=tn),
        grid=(n // tm,),
        in_specs=[pl.BlockSpec((tm, d), lambda i: (i, 0)), _resident((1, d)), _resident((d, nout))],
        out_specs=pl.BlockSpec((tm, nout), lambda i: (i, 0)),
        out_shape=jax.ShapeDtypeStruct((n, nout), BF16),
        compiler_params=_cparams(("parallel",)),
    )(x2, g, w)


def _matmul_res_kernel(o_ref, w_ref, x_ref, y_ref):
    y_ref[...] = x_ref[...] + _dot(o_ref[...], w_ref[...])


def _matmul_res(o2, w, x2, tm=512):
    n, k = o2.shape
    d = w.shape[1]
    return pl.pallas_call(
        _matmul_res_kernel,
        grid=(n // tm,),
        in_specs=[pl.BlockSpec((tm, k), lambda i: (i, 0)), _resident((k, d)),
                  pl.BlockSpec((tm, d), lambda i: (i, 0))],
        out_specs=pl.BlockSpec((tm, d), lambda i: (i, 0)),
        out_shape=jax.ShapeDtypeStruct((n, d), F32),
        compiler_params=_cparams(("parallel",)),
    )(o2, w, x2)


def _ret_kernel(cd_ref, q_ref, k_ref, v_ref, g_ref, cos_ref, sin_ref, intra_ref, qd_ref, kd_ref,
                o_ref, s_ref, *, n_chunks):
    c_len = RET_CHUNK
    half = RET_DK // 2
    cd = cd_ref[pl.program_id(1)]
    intra = intra_ref[0]
    qd = qd_ref[0]
    kd = kd_ref[0]
    s_ref[...] = jnp.zeros_like(s_ref)

    def chunk(c, carry):
        r0 = pl.multiple_of(c * c_len, c_len)
        cos = cos_ref[pl.ds(r0, c_len), :]
        sin = sin_ref[pl.ds(r0, c_len), :]

        def rot(x):
            x1, x2 = x[:, :half], x[:, half:]
            return jnp.concatenate([x1 * cos - x2 * sin, x1 * sin + x2 * cos], axis=-1)

        qr = rot(q_ref[0, pl.ds(r0, c_len), :].astype(F32))
        kr = rot(k_ref[0, pl.ds(r0, c_len), :].astype(F32)) * (RET_DK ** -0.5)
        vc = v_ref[0, pl.ds(r0, c_len), :]
        s = _dot_nt(qr.astype(BF16), kr.astype(BF16)) * intra
        state = s_ref[...]
        o = _dot(s.astype(BF16), vc) + _dot((qr * qd).astype(BF16), state.astype(BF16))
        s_ref[...] = cd * state + _dot_tn((kr * kd).astype(BF16), vc)
        o = o * lax.rsqrt(jnp.mean(o * o, axis=-1, keepdims=True) + RMS_EPS)
        gate = g_ref[0, pl.ds(r0, c_len), :].astype(F32)
        o_ref[0, pl.ds(r0, c_len), :] = (gate * jax.nn.sigmoid(gate) * o).astype(o_ref.dtype)
        return carry

    lax.fori_loop(0, n_chunks, chunk, 0)


def _retention_core(h3):
    b, t, _ = h3.shape
    hh, dk, dv, c = RET_HEADS, RET_DK, RET_DV, RET_CHUNK
    half = dk // 2
    pos = jnp.arange(t, dtype=F32)
    inv_freq = RET_ROPE_BASE ** (-jnp.arange(half, dtype=F32) / half)
    ang = pos[:, None] * inv_freq[None, :]
    cos, sin = jnp.cos(ang), jnp.sin(ang)
    log_gamma = jnp.log(1.0 - 2.0 ** (-5.0 - jnp.arange(hh, dtype=F32)))
    i = jnp.arange(c)
    diff = i[:, None] - i[None, :]
    intra = jnp.where(diff >= 0, jnp.exp(log_gamma[:, None, None] * jnp.maximum(diff, 0)), 0.0)
    q_dec = jnp.exp(log_gamma[:, None] * (i + 1))[:, :, None]
    k_dec = jnp.exp(log_gamma[:, None] * (c - 1 - i))[:, :, None]
    chunk_dec = jnp.exp(log_gamma * c)
    nq = hh * dk // dk
    return pl.pallas_call(
        functools.partial(_ret_kernel, n_chunks=t // c),
        grid=(b, hh),
        in_specs=[
            pl.BlockSpec(memory_space=pltpu.SMEM),
            pl.BlockSpec((1, t, dk), lambda bi, hi: (bi, 0, hi)),
            pl.BlockSpec((1, t, dk), lambda bi, hi: (bi, 0, nq + hi)),
            pl.BlockSpec((1, t, dv), lambda bi, hi: (bi, 0, nq + hi)),
            pl.BlockSpec((1, t, dv), lambda bi, hi: (bi, 0, 2 * nq + hi)),
            _resident((t, half)),
            _resident((t, half)),
            pl.BlockSpec((1, c, c), lambda bi, hi: (hi, 0, 0)),
            pl.BlockSpec((1, c, 1), lambda bi, hi: (hi, 0, 0)),
            pl.BlockSpec((1, c, 1), lambda bi, hi: (hi, 0, 0)),
        ],
        out_specs=pl.BlockSpec((1, t, dv), lambda bi, hi: (bi, 0, hi)),
        out_shape=jax.ShapeDtypeStruct((b, t, hh * dv), BF16),
        scratch_shapes=[pltpu.VMEM((dk, dv), F32)],
        compiler_params=_cparams(("parallel", "arbitrary")),
    )(chunk_dec, h3, h3, h3, h3, cos, sin, intra, q_dec, k_dec)


def _k_headnorm(x, gain_row):
    klane = lax.broadcasted_iota(jnp.int32, x.shape, 1) < NSA_DH
    ms = jnp.sum(jnp.where(klane, x * x, 0.0), axis=-1, keepdims=True) * (1.0 / NSA_DH)
    return jnp.where(klane, x * lax.rsqrt(ms + RMS_EPS) * gain_row, x)


def _cmp_kernel(a_ref, pos_ref, w1_ref, w2_ref, kg_ref, o_ref, af_ref, *, t):
    n_blk = t // CMP_STRIDE
    af_ref[0:t, :] = a_ref[0].astype(F32)
    af_ref[t:t + CMP_LEN, :] = jnp.zeros((CMP_LEN, LANES), F32)
    acc = jnp.zeros((n_blk, 2 * CMP_HID), F32)
    for l in range(CMP_LEN):
        rows = af_ref[pl.ds(l, n_blk, stride=CMP_STRIDE), :]
        acc = acc + _dot((rows + pos_ref[l:l + 1, :]).astype(BF16), w1_ref[l])
    hid = jax.nn.gelu(acc, approximate=True)
    out = _dot(hid.astype(BF16), w2_ref[...])
    o_ref[0, 0] = _k_headnorm(out, kg_ref[...]).astype(o_ref.dtype)


def _compress(h3, pos, w1cat, w2cat, kgain_row):
    b, t, _ = h3.shape
    n_blk = t // CMP_STRIDE
    return pl.pallas_call(
        functools.partial(_cmp_kernel, t=t),
        grid=(b, NSA_GROUPS),
        in_specs=[
            pl.BlockSpec((1, t, LANES), lambda bi, gi: (bi, 0, 8 + gi)),
            _resident((CMP_LEN, LANES)),
            _resident((CMP_LEN, LANES, 2 * CMP_HID)),
            _resident((2 * CMP_HID, LANES)),
            _resident((1, LANES)),
        ],
        out_specs=pl.BlockSpec((1, 1, n_blk, LANES), lambda bi, gi: (bi, gi, 0, 0)),
        out_shape=jax.ShapeDtypeStruct((b, NSA_GROUPS, n_blk, LANES), BF16),
        scratch_shapes=[pltpu.VMEM((t + CMP_LEN, LANES), F32)],
        compiler_params=_cparams(("parallel", "parallel")),
    )(h3, pos, w1cat, w2cat, kgain_row)


def _nsa_attn_kernel(q_ref, kvc_ref, kvs_ref, kvw_ref, gate_ref, qg_ref, kg_ref, bnd_ref, bc_ref,
                     ovt_ref, eneg_ref, o_ref, ks_ref, kw_ref, vst_ref, vwt_ref, *, t):
    hpg = NSA_HPG
    dh = NSA_DH
    qt = pl.program_id(2)
    q0 = pl.multiple_of(qt * TQ, TQ)
    lane = lax.broadcasted_iota(jnp.int32, (1, LANES), 1)
    klane = lane < dh
    ones_rows = lax.broadcasted_iota(jnp.int32, (LANES, LANES), 0) < dh

    @pl.when(qt == 0)
    def _():
        ks_ref[0:TQ, :] = jnp.zeros((TQ, LANES), BF16)
        kw_ref[0:WINDOW, :] = jnp.zeros((WINDOW, LANES), BF16)
        pad_col = jnp.where(ones_rows, 1.0, 0.0).astype(BF16)
        for i in range(TQ // LANES):
            vst_ref[:, i * LANES:(i + 1) * LANES] = pad_col
        for i in range(WINDOW // LANES):
            vwt_ref[:, i * LANES:(i + 1) * LANES] = pad_col

        def norm_block(c, carry):
            r0 = pl.multiple_of(c * LANES, LANES)
            for src, dst, dst_t, pad, gi in ((kvs_ref, ks_ref, vst_ref, TQ, 0), (kvw_ref, kw_ref, vwt_ref, WINDOW, 1)):
                x = _k_headnorm(src[0, pl.ds(r0, LANES), :].astype(F32), kg_ref[gi:gi + 1, :])
                dst[pl.ds(pad + r0, LANES), :] = x.astype(BF16)
                dst_t[:, pl.ds(pad + r0, LANES)] = jnp.where(ones_rows, 1.0, x.T).astype(BF16)
            return carry

        lax.fori_loop(0, t // LANES, norm_block, 0)

    xq = q_ref[0].astype(F32)
    lane_q = lax.broadcasted_iota(jnp.int32, (1, hpg * dh), 1)
    xq2 = xq * xq
    qz = []
    for p in range(hpg):
        seg = (lane_q >= dh * p) & (lane_q < dh * (p + 1))
        ms = jnp.sum(jnp.where(seg, xq2, 0.0), axis=-1, keepdims=True) * (1.0 / dh)
        c0 = LANES * (p // 2)
        half = xq[:, c0:c0 + LANES] * lax.rsqrt(ms + RMS_EPS) * qg_ref[:, c0:c0 + LANES]
        if p % 2 == 1:
            half = pltpu.roll(half, dh, 1)
        qz.append(jnp.where(klane, half, 0.0).astype(BF16))
    q4 = jnp.concatenate(qz, axis=0)

    def heads(s, add, bias=None, bias_rows=0):
        out = []
        for p in range(hpg):
            sp = s[:, p * TQ:(p + 1) * TQ] + add
            if bias is not None:
                n = sp.shape[0] - bias_rows
                tail = sp[n:, :] + bias[p]
                sp = tail if n == 0 else jnp.concatenate([sp[:n, :], tail], axis=0)
            out.append(sp)
        return jnp.concatenate(out, axis=1)

    per_tile = TQ // CMP_STRIDE
    off = pl.multiple_of((LANES - per_tile) - qt * per_tile, per_tile)
    blk_i = lax.broadcasted_iota(jnp.int32, (LANES, TQ), 0)
    valid_c = (CMP_STRIDE * blk_i + (CMP_LEN - 1)) <= (q0 + lax.broadcasted_iota(jnp.int32, (LANES, TQ), 1))
    kvc = kvc_ref[0, 0]
    s_c = _dot_nt(kvc, q4)
    pcs = []
    for p in range(hpg):
        s = jnp.where(valid_c, s_c[:, p * TQ:(p + 1) * TQ] + bc_ref[0, p, pl.ds(off, LANES), :], NEG)
        m = jnp.max(s, axis=0, keepdims=True)
        e = jnp.where(valid_c, jnp.exp2(s - m), 0.0)
        l = jnp.sum(e, axis=0, keepdims=True)
        pcs.append(e / jnp.where(l > 0.0, l, 1.0))
    vc_t = jnp.where(ones_rows, 1.0, kvc.astype(F32).T).astype(BF16)
    o_cmp = _dot(vc_t, jnp.concatenate(pcs, axis=1).astype(BF16))
    psum = pcs[0]
    for p in range(1, hpg):
        psum = psum + pcs[p]
    p_hi = psum.astype(BF16)
    p_lo = (psum - p_hi.astype(F32)).astype(BF16)
    n_slc = t // SLC_LEN
    imp_t = (_dot(ovt_ref[...], p_hi) + _dot(ovt_ref[...], p_lo))[:n_slc, :]

    jrow = lax.broadcasted_iota(jnp.int32, (n_slc, TQ), 0)
    cur = (q0 + lax.broadcasted_iota(jnp.int32, (n_slc, TQ), 1)) // SLC_LEN
    forced = (jrow == 0) | (jrow == cur) | (jrow == cur - 1)
    vals = jnp.where(forced, SEL_BIG, jnp.where(jrow <= cur, imp_t, -SEL_BIG))
    cnt = jnp.zeros((n_slc, TQ), F32)
    for jp in range(n_slc):
        row = vals[jp:jp + 1, :]
        ahead = (row > vals) | ((row == vals) & (jrow > jp))
        cnt = cnt + jnp.where(ahead, 1.0, 0.0)
    not_sel = jnp.where(cnt < float(N_SEL), 0.0, 1.0)
    not_sel = jnp.concatenate([not_sel, jnp.zeros((LANES - n_slc, TQ), F32)], axis=0).astype(BF16)

    causal = jnp.where(lax.broadcasted_iota(jnp.int32, (TQ, TQ), 0) <= lax.broadcasted_iota(jnp.int32, (TQ, TQ), 1),
                       0.0, NEG)
    has_near = jnp.where(qt >= 1, jnp.zeros((TQ, TQ), F32), NEG)
    bnd = [bnd_ref[0, p] for p in range(hpg)]

    kv_nd = ks_ref[pl.ds(q0, 2 * TQ), :]
    add_nd = _dot(eneg_ref[pl.ds(q0, 2 * TQ), :], not_sel) + jnp.concatenate([has_near, causal], axis=0)
    s = heads(_dot_nt(kv_nd, q4), add_nd, bnd, 2 * TQ)
    m_sel = jnp.max(s, axis=0, keepdims=True)
    acc_sel = _dot(vst_ref[:, pl.ds(q0, 2 * TQ)], jnp.exp2(s - m_sel).astype(BF16))

    ck = CK
    far_end = (qt - 1) * TQ
    key_i = lax.broadcasted_iota(jnp.int32, (ck, TQ), 0)

    def sel_far(c, carry):
        m_old, acc = carry
        r0 = pl.multiple_of(TQ + c * ck, LANES)
        kv = ks_ref[pl.ds(r0, ck), :]
        add = jnp.where(key_i + c * ck < far_end, _dot(eneg_ref[pl.ds(r0, ck), :], not_sel), NEG)
        s = heads(_dot_nt(kv, q4), add)
        m_new = jnp.maximum(m_old, jnp.max(s, axis=0, keepdims=True))
        pv = _dot(vst_ref[:, pl.ds(r0, ck)], jnp.exp2(s - m_new).astype(BF16))
        return m_new, jnp.exp2(m_old - m_new) * acc + pv

    _, acc_sel = lax.fori_loop(0, jnp.maximum(far_end + ck - 1, 0) // ck, sel_far, (m_sel, acc_sel))

    ww = WINDOW + TQ
    kv_w = kw_ref[pl.ds(q0, ww), :]
    jw = lax.broadcasted_iota(jnp.int32, (ww, TQ), 0)
    iw = lax.broadcasted_iota(jnp.int32, (ww, TQ), 1)
    valid_w = (jw > iw) & (jw <= iw + WINDOW) & (jw + (q0 - WINDOW) >= 0)
    s = heads(_dot_nt(kv_w, q4), jnp.where(valid_w, 0.0, NEG), bnd, 2 * TQ)
    m_win = jnp.max(s, axis=0, keepdims=True)
    acc_win = _dot(vwt_ref[:, pl.ds(q0, ww)], jnp.exp2(s - m_win).astype(BF16))

    gl_t = jax.nn.sigmoid(gate_ref[0].astype(F32)).T
    outs = []
    for p in range(hpg):
        sl = slice(p * TQ, (p + 1) * TQ)
        g_c = gl_t[p:p + 1, :]
        g_s = gl_t[hpg + p:hpg + p + 1, :] / acc_sel[0:1, sl]
        g_w = gl_t[2 * hpg + p:2 * hpg + p + 1, :] / acc_win[0:1, sl]
        outs.append(g_c * o_cmp[dh:, sl] + g_s * acc_sel[dh:, sl] + g_w * acc_win[dh:, sl])
    o_ref[0] = jnp.concatenate(outs, axis=0).T.astype(o_ref.dtype)


def _nsa_attention(h3, kvc, qgain, kgain, bias_nd, bias_cmp, overlap_t, eneg):
    b, t, _ = h3.shape
    g, hpg = NSA_GROUPS, NSA_HPG
    n_blk = t // CMP_STRIDE
    qw = hpg * NSA_DH
    return pl.pallas_call(
        functools.partial(_nsa_attn_kernel, t=t),
        grid=(b, g, t // TQ),
        in_specs=[
            pl.BlockSpec((1, TQ, qw), lambda bi, gi, qi: (bi, qi, gi)),
            pl.BlockSpec((1, 1, n_blk, LANES), lambda bi, gi, qi: (bi, gi, 0, 0)),
            pl.BlockSpec((1, t, LANES), lambda bi, gi, qi: (bi, 0, 12 + gi)),
            pl.BlockSpec((1, t, LANES), lambda bi, gi, qi: (bi, 0, 16 + gi)),
            pl.BlockSpec((1, TQ, LANES), lambda bi, gi, qi: (bi, qi, 20 + gi)),
            pl.BlockSpec((1, qw), lambda bi, gi, qi: (0, 0)),
            pl.BlockSpec((2, LANES), lambda bi, gi, qi: (0, 0)),
            pl.BlockSpec((1, hpg, 2 * TQ, TQ), lambda bi, gi, qi: (gi, 0, 0, 0)),
            pl.BlockSpec((1, hpg, 2 * LANES, TQ), lambda bi, gi, qi: (gi, 0, 0, 0)),
            pl.BlockSpec((LANES, LANES), lambda bi, gi, qi: (0, 0)),
            pl.BlockSpec((TQ + t, LANES), lambda bi, gi, qi: (0, 0)),
        ],
        out_specs=pl.BlockSpec((1, TQ, qw), lambda bi, gi, qi: (bi, qi, gi)),
        out_shape=jax.ShapeDtypeStruct((b, t, NSA_HEADS * NSA_DH), BF16),
        scratch_shapes=[pltpu.VMEM((TQ + t, LANES), BF16), pltpu.VMEM((WINDOW + t, LANES), BF16),
                        pltpu.VMEM((LANES, TQ + t), BF16), pltpu.VMEM((LANES, WINDOW + t), BF16)],
        compiler_params=_cparams(("arbitrary", "arbitrary", "arbitrary")),
    )(h3, kvc, h3, h3, h3, qgain, kgain, bias_nd, bias_cmp, overlap_t, eneg)


def _t5_bucket_np(dist):
    n = np.maximum(dist, 0)
    max_exact = NUM_BUCKETS // 2
    nf = np.maximum(n, max_exact).astype(np.float32)
    large = max_exact + (np.log(nf / max_exact) / np.float32(np.log(MAX_DISTANCE / max_exact))
                         * (NUM_BUCKETS - max_exact)).astype(np.int32)
    large = np.minimum(large, NUM_BUCKETS - 1)
    return np.where(n < max_exact, n, large)


def _nsa_constants(t):
    n_cmp = (t - CMP_LEN) // CMP_STRIDE + 1
    n_slc = t // SLC_LEN
    n_blk = t // CMP_STRIDE
    per_tile = TQ // CMP_STRIDE
    assert n_blk <= LANES and per_tile * (t // TQ - 1) <= LANES - per_tile
    iq = np.arange(TQ)
    rel = iq[None, :] - iq[:, None]
    assert _t5_bucket_np(np.array([TQ + 1]))[0] == NUM_BUCKETS - 1
    nd_bucket = _t5_bucket_np(np.concatenate([TQ + rel, rel], axis=0))
    blk_off = np.arange(LANES) - (LANES - per_tile)
    bc_bucket = _t5_bucket_np(iq[None, :] - CMP_STRIDE * blk_off[:, None] - (CMP_LEN - 1))
    bc_bucket = np.concatenate([bc_bucket, bc_bucket], axis=0)
    ci = np.arange(n_blk) * CMP_STRIDE
    sj = np.arange(n_slc) * SLC_LEN
    ov = ((ci[:, None] < sj[None, :] + SLC_LEN) & (ci[:, None] + CMP_LEN > sj[None, :])).astype(np.float32)
    ov[n_cmp:, :] = 0.0
    overlap_t = np.zeros((LANES, LANES), np.float32)
    overlap_t[:n_slc, :n_blk] = ov.T
    eneg = np.zeros((TQ + t, LANES), np.float32)
    eneg[TQ + np.arange(t), np.arange(t) // SLC_LEN] = NEG
    return nd_bucket, bc_bucket, overlap_t, eneg


def _nsa_w_in_columns():
    kv0 = NSA_HEADS * NSA_DH
    kvw = NSA_GROUPS * NSA_DH
    cols = list(range(kv0))
    for br in range(3):
        for g in range(NSA_GROUPS):
            k_src = kv0 + (2 * br) * kvw + g * NSA_DH
            v_src = kv0 + (2 * br + 1) * kvw + g * NSA_DH
            cols += list(range(k_src, k_src + NSA_DH)) + list(range(v_src, v_src + NSA_DH))
    gate0 = kv0 + 6 * kvw
    for g in range(NSA_GROUPS):
        blk = [-1] * LANES
        for br in range(3):
            for p in range(NSA_HPG):
                blk[br * NSA_HPG + p] = gate0 + br * NSA_HEADS + g * NSA_HPG + p
        cols += blk
    return np.asarray(cols, np.int32)


def _retention_layer(x2, b, t, norm_g, w_in, w_out):
    h = _norm_matmul(x2, norm_g, w_in.astype(BF16), tn=1024)
    o = _retention_core(h.reshape(b, t, -1))
    return _matmul_res(o.reshape(b * t, -1), w_out.astype(BF16), x2)


def _nsa_layer(x2, b, t, norm_g, w_in, w_out, q_gain, k_gain, cmp_pos, cmp_w1, cmp_w2, bias_nd,
               bias_cmp, overlap_t, eneg):
    cols = _nsa_w_in_columns()
    w_k = jnp.where(cols[None, :] >= 0, jnp.take(w_in, np.maximum(cols, 0), axis=1), 0.0).astype(BF16)
    h3 = _norm_matmul(x2, norm_g, w_k, tn=1024).reshape(b, t, NSA_COLS)

    dh = NSA_DH
    w1 = cmp_w1.reshape(2, CMP_LEN, dh, CMP_HID)
    zero1 = jnp.zeros((CMP_LEN, dh, CMP_HID), F32)
    w1cat = jnp.concatenate([jnp.concatenate([w1[0], zero1], axis=-1),
                             jnp.concatenate([zero1, w1[1]], axis=-1)], axis=1).astype(BF16)
    zero2 = jnp.zeros((CMP_HID, dh), F32)
    w2cat = jnp.concatenate([jnp.concatenate([cmp_w2[0], zero2], axis=-1),
                             jnp.concatenate([zero2, cmp_w2[1]], axis=-1)], axis=0).astype(BF16)
    pos = jnp.concatenate([cmp_pos[0], cmp_pos[1]], axis=-1)
    ones = jnp.ones((dh,), F32)
    kvc = _compress(h3, pos, w1cat, w2cat, jnp.concatenate([k_gain[0], ones])[None, :])

    qgain = jnp.tile(q_gain * (dh ** -0.5 * LOG2E), NSA_HPG)[None, :]
    kgain = jnp.stack([jnp.concatenate([k_gain[1], ones]), jnp.concatenate([k_gain[2], ones])])
    o = _nsa_attention(h3, kvc, qgain, kgain, bias_nd, bias_cmp, overlap_t, eneg)
    return _matmul_res(o.reshape(b * t, -1), w_out.astype(BF16), x2)


def kernel(x, ffn1_norm, ffn1_w_gu, ffn1_w_down, mix_norm, ffn2_norm, ffn2_w_gu, ffn2_w_down,
           ret_w_in, ret_w_out, nsa_w_in, nsa_w_out, nsa_q_gain, nsa_k_gain,
           nsa_cmp_pos, nsa_cmp_w1, nsa_cmp_w2, rel_bias):
    b, t, d = x.shape
    depth = ffn1_norm.shape[0]
    d_ff = ffn1_w_down.shape[1]
    tf = d_ff
    nf = d_ff // tf

    def ffn_weights(w_gu, w_down):
        wgu = w_gu.astype(BF16).reshape(d, 2, nf, tf).transpose(1, 2, 0, 3)
        return wgu, w_down.astype(BF16).reshape(nf, tf, d)

    nd_bucket, bc_bucket, overlap_t, eneg = _nsa_constants(t)
    tbl = rel_bias.astype(F32) * LOG2E
    bias_nd = (tbl[nd_bucket] - tbl[NUM_BUCKETS - 1]).transpose(2, 0, 1)
    bias_nd = bias_nd.reshape(NSA_GROUPS, NSA_HPG, 2 * TQ, TQ)
    bias_cmp = tbl[bc_bucket].transpose(2, 0, 1).reshape(NSA_GROUPS, NSA_HPG, 2 * LANES, TQ)
    overlap_t = jnp.asarray(overlap_t, BF16)
    eneg = jnp.asarray(eneg, BF16)

    x2 = x.reshape(b * t, d)
    for layer in range(depth):
        x2 = _ffn(x2, ffn1_norm[layer][None, :], *ffn_weights(ffn1_w_gu[layer], ffn1_w_down[layer]))
        j = layer // 2
        if layer % 2 == 0:
            x2 = _retention_layer(x2, b, t, mix_norm[layer][None, :], ret_w_in[j], ret_w_out[j])
        else:
            x2 = _nsa_layer(x2, b, t, mix_norm[layer][None, :], nsa_w_in[j], nsa_w_out[j], nsa_q_gain[j],
                            nsa_k_gain[j], nsa_cmp_pos[j], nsa_cmp_w1[j], nsa_cmp_w2[j], bias_nd,
                            bias_cmp, overlap_t, eneg)
        x2 = _ffn(x2, ffn2_norm[layer][None, :], *ffn_weights(ffn2_w_gu[layer], ffn2_w_down[layer]))
    return x2.reshape(b, t, d)
```

```python
import functools

import numpy as np
import jax
import jax.numpy as jnp
from jax import lax
from jax.experimental import pallas as pl
from jax.experimental.pallas import tpu as pltpu

F32 = jnp.float32
BF16 = jnp.bfloat16

RMS_EPS = 1e-6
NEG = -1e30
SEL_BIG = 1e9
LOG2E = 1.4426950408889634

LANES = 128
VMEM_LIMIT = 56 * 1024 * 1024

RET_HEADS = 4
RET_DK = 256
RET_DV = 512
RET_CHUNK = 128
RET_ROPE_BASE = 10000.0

NSA_HEADS = 16
NSA_GROUPS = 4
NSA_HPG = 4
NSA_DH = 64
CMP_LEN = 32
CMP_STRIDE = 16
CMP_HID = 256
SLC_LEN = 64
N_SEL = 8
WINDOW = 512
NUM_BUCKETS = 32
MAX_DISTANCE = 128
TQ = 256
CK = 512
NSA_COLS = 1024 + 3 * 4 * 128 + 4 * 128


def _dot(a, b):
    return jnp.dot(a, b, preferred_element_type=F32)


def _dot_nt(a, b):
    return lax.dot_general(a, b, (((1,), (1,)), ((), ())), preferred_element_type=F32)


def _dot_tn(a, b):
    return lax.dot_general(a, b, (((0,), (0,)), ((), ())), preferred_element_type=F32)


def _rms(x, g):
    ms = jnp.mean(x * x, axis=-1, keepdims=True)
    return x * lax.rsqrt(ms + RMS_EPS) * g


def _cparams(sem):
    return pltpu.CompilerParams(dimension_semantics=sem, vmem_limit_bytes=VMEM_LIMIT)


def _resident(shape):
    n = len(shape)
    return pl.BlockSpec(shape, lambda *_: (0,) * n, pipeline_mode=pl.Buffered(1))


def _ffn_kernel(x_ref, g_ref, wgu_ref, wd_ref, o_ref, *, nf):
    x = x_ref[...]
    xn = _rms(x, g_ref[...]).astype(BF16)
    acc = None
    for j in range(nf):
        a = _dot(xn, wgu_ref[0, j])
        b = _dot(xn, wgu_ref[1, j])
        h = (a * jax.nn.sigmoid(a) * b).astype(BF16)
        part = _dot(h, wd_ref[j])
        acc = part if acc is None else acc + part
    o_ref[...] = x + 0.5 * acc


def _ffn(x2, g, wgu, wd, tm=512):
    n, d = x2.shape
    _, nf, _, tf = wgu.shape
    return pl.pallas_call(
        functools.partial(_ffn_kernel, nf=nf),
        grid=(n // tm,),
        in_specs=[
            pl.BlockSpec((tm, d), lambda i: (i, 0)),
            _resident((1, d)),
            _resident((2, nf, d, tf)),
            _resident((nf, tf, d)),
        ],
        out_specs=pl.BlockSpec((tm, d), lambda i: (i, 0)),
        out_shape=jax.ShapeDtypeStruct((n, d), F32),
        compiler_params=_cparams(("parallel",)),
    )(x2, g, wgu, wd)


def _norm_matmul_kernel(x_ref, g_ref, w_ref, o_ref, *, tn):
    xn = _rms(x_ref[...], g_ref[...]).astype(BF16)
    for j in range(w_ref.shape[1] // tn):
        o_ref[:, j * tn:(j + 1) * tn] = _dot(xn, w_ref[:, j * tn:(j + 1) * tn]).astype(o_ref.dtype)


def _norm_matmul(x2, g, w, tn, tm=512):
    n, d = x2.shape
    nout = w.shape[1]
    return pl.pallas_call(
        functools.partial(_norm_matmul_kernel, tn=tn),
        grid=(n // tm,),
        in_specs=[pl.BlockSpec((tm, d), lambda i: (i, 0)), _resident((1, d)), _resident((d, nout))],
        out_specs=pl.BlockSpec((tm, nout), lambda i: (i, 0)),
        out_shape=jax.ShapeDtypeStruct((n, nout), BF16),
        compiler_params=_cparams(("parallel",)),
    )(x2, g, w)


def _matmul_res_kernel(o_ref, w_ref, x_ref, y_ref):
    y_ref[...] = x_ref[...] + _dot(o_ref[...], w_ref[...])


def _matmul_res(o2, w, x2, tm=512):
    n, k = o2.shape
    d = w.shape[1]
    return pl.pallas_call(
        _matmul_res_kernel,
        grid=(n // tm,),
        in_specs=[pl.BlockSpec((tm, k), lambda i: (i, 0)), _resident((k, d)),
                  pl.BlockSpec((tm, d), lambda i: (i, 0))],
        out_specs=pl.BlockSpec((tm, d), lambda i: (i, 0)),
        out_shape=jax.ShapeDtypeStruct((n, d), F32),
        compiler_params=_cparams(("parallel",)),
    )(o2, w, x2)


def _ret_kernel(cd_ref, q_ref, k_ref, v_ref, g_ref, cos_ref, sin_ref, intra_ref, qd_ref, kd_ref,
                o_ref, s_ref, *, n_chunks):
    c_len = RET_CHUNK
    half = RET_DK // 2
    cd = cd_ref[pl.program_id(1)]
    intra = intra_ref[0]
    qd = qd_ref[0]
    kd = kd_ref[0]
    s_ref[...] = jnp.zeros_like(s_ref)

    def chunk(c, carry):
        r0 = pl.multiple_of(c * c_len, c_len)
        cos = cos_ref[pl.ds(r0, c_len), :]
        sin = sin_ref[pl.ds(r0, c_len), :]

        def rot(x):
            x1, x2 = x[:, :half], x[:, half:]
            return jnp.concatenate([x1 * cos - x2 * sin, x1 * sin + x2 * cos], axis=-1)

        qr = rot(q_ref[0, pl.ds(r0, c_len), :].astype(F32))
        kr = rot(k_ref[0, pl.ds(r0, c_len), :].astype(F32)) * (RET_DK ** -0.5)
        vc = v_ref[0, pl.ds(r0, c_len), :]
        s = _dot_nt(qr.astype(BF16), kr.astype(BF16)) * intra
        state = s_ref[...]
        o = _dot(s.astype(BF16), vc) + _dot((qr * qd).astype(BF16), state.astype(BF16))
        s_ref[...] = cd * state + _dot_tn((kr * kd).astype(BF16), vc)
        o = o * lax.rsqrt(jnp.mean(o * o, axis=-1, keepdims=True) + RMS_EPS)
        gate = g_ref[0, pl.ds(r0, c_len), :].astype(F32)
        o_ref[0, pl.ds(r0, c_len), :] = (gate * jax.nn.sigmoid(gate) * o).astype(o_ref.dtype)
        return carry

    lax.fori_loop(0, n_chunks, chunk, 0, unroll=2)


def _retention_core(h3):
    b, t, _ = h3.shape
    hh, dk, dv, c = RET_HEADS, RET_DK, RET_DV, RET_CHUNK
    half = dk // 2
    pos = jnp.arange(t, dtype=F32)
    inv_freq = RET_ROPE_BASE ** (-jnp.arange(half, dtype=F32) / half)
    ang = pos[:, None] * inv_freq[None, :]
    cos, sin = jnp.cos(ang), jnp.sin(ang)
    log_gamma = jnp.log(1.0 - 2.0 ** (-5.0 - jnp.arange(hh, dtype=F32)))
    i = jnp.arange(c)
    diff = i[:, None] - i[None, :]
    intra = jnp.where(diff >= 0, jnp.exp(log_gamma[:, None, None] * jnp.maximum(diff, 0)), 0.0)
    q_dec = jnp.exp(log_gamma[:, None] * (i + 1))[:, :, None]
    k_dec = jnp.exp(log_gamma[:, None] * (c - 1 - i))[:, :, None]
    chunk_dec = jnp.exp(log_gamma * c)
    nq = hh * dk // dk
    return pl.pallas_call(
        functools.partial(_ret_kernel, n_chunks=t // c),
        grid=(b, hh),
        in_specs=[
            pl.BlockSpec(memory_space=pltpu.SMEM),
            pl.BlockSpec((1, t, dk), lambda bi, hi: (bi, 0, hi)),
            pl.BlockSpec((1, t, dk), lambda bi, hi: (bi, 0, nq + hi)),
            pl.BlockSpec((1, t, dv), lambda bi, hi: (bi, 0, nq + hi)),
            pl.BlockSpec((1, t, dv), lambda bi, hi: (bi, 0, 2 * nq + hi)),
            _resident((t, half)),
            _resident((t, half)),
            pl.BlockSpec((1, c, c), lambda bi, hi: (hi, 0, 0)),
            pl.BlockSpec((1, c, 1), lambda bi, hi: (hi, 0, 0)),
            pl.BlockSpec((1, c, 1), lambda bi, hi: (hi, 0, 0)),
        ],
        out_specs=pl.BlockSpec((1, t, dv), lambda bi, hi: (bi, 0, hi)),
        out_shape=jax.ShapeDtypeStruct((b, t, hh * dv), BF16),
        scratch_shapes=[pltpu.VMEM((dk, dv), F32)],
        compiler_params=_cparams(("parallel", "arbitrary")),
    )(chunk_dec, h3, h3, h3, h3, cos, sin, intra, q_dec, k_dec)


def _k_headnorm(x, gain_row):
    klane = lax.broadcasted_iota(jnp.int32, x.shape, 1) < NSA_DH
    ms = jnp.sum(jnp.where(klane, x * x, 0.0), axis=-1, keepdims=True) * (1.0 / NSA_DH)
    return jnp.where(klane, x * lax.rsqrt(ms + RMS_EPS) * gain_row, x)


def _cmp_kernel(a_ref, pos_ref, w1_ref, w2_ref, kg_ref, o_ref, af_ref, *, t):
    n_blk = t // CMP_STRIDE
    af_ref[0:t, :] = a_ref[0].astype(F32)
    af_ref[t:t + CMP_LEN, :] = jnp.zeros((CMP_LEN, LANES), F32)
    acc = jnp.zeros((n_blk, 2 * CMP_HID), F32)
    for l in range(CMP_LEN):
        rows = af_ref[pl.ds(l, n_blk, stride=CMP_STRIDE), :]
        acc = acc + _dot((rows + pos_ref[l:l + 1, :]).astype(BF16), w1_ref[l])
    hid = jax.nn.gelu(acc, approximate=True)
    out = _dot(hid.astype(BF16), w2_ref[...])
    o_ref[0, 0] = _k_headnorm(out, kg_ref[...]).astype(o_ref.dtype)


def _compress(h3, pos, w1cat, w2cat, kgain_row):
    b, t, _ = h3.shape
    n_blk = t // CMP_STRIDE
    return pl.pallas_call(
        functools.partial(_cmp_kernel, t=t),
        grid=(b, NSA_GROUPS),
        in_specs=[
            pl.BlockSpec((1, t, LANES), lambda bi, gi: (bi, 0, 8 + gi)),
            _resident((CMP_LEN, LANES)),
            _resident((CMP_LEN, LANES, 2 * CMP_HID)),
            _resident((2 * CMP_HID, LANES)),
            _resident((1, LANES)),
        ],
        out_specs=pl.BlockSpec((1, 1, n_blk, LANES), lambda bi, gi: (bi, gi, 0, 0)),
        out_shape=jax.ShapeDtypeStruct((b, NSA_GROUPS, n_blk, LANES), BF16),
        scratch_shapes=[pltpu.VMEM((t + CMP_LEN, LANES), F32)],
        compiler_params=_cparams(("parallel", "parallel")),
    )(h3, pos, w1cat, w2cat, kgain_row)


def _nsa_attn_kernel(q_ref, kvc_ref, kvs_ref, kvw_ref, gate_ref, qg_ref, kg_ref, bnd_ref, bc_ref,
                     ovt_ref, eneg_ref, o_ref, ks_ref, kw_ref, vst_ref, vwt_ref, *, t):
    hpg = NSA_HPG
    dh = NSA_DH
    qt = pl.program_id(2)
    q0 = pl.multiple_of(qt * TQ, TQ)
    lane = lax.broadcasted_iota(jnp.int32, (1, LANES), 1)
    klane = lane < dh
    ones_rows = lax.broadcasted_iota(jnp.int32, (LANES, LANES), 0) < dh

    @pl.when(qt == 0)
    def _():
        ks_ref[0:TQ, :] = jnp.zeros((TQ, LANES), BF16)
        kw_ref[0:WINDOW, :] = jnp.zeros((WINDOW, LANES), BF16)
        pad_col = jnp.where(ones_rows, 1.0, 0.0).astype(BF16)
        for i in range(TQ // LANES):
            vst_ref[:, i * LANES:(i + 1) * LANES] = pad_col
        for i in range(WINDOW // LANES):
            vwt_ref[:, i * LANES:(i + 1) * LANES] = pad_col

        def norm_block(c, carry):
            r0 = pl.multiple_of(c * LANES, LANES)
            for src, dst, dst_t, pad, gi in ((kvs_ref, ks_ref, vst_ref, TQ, 0), (kvw_ref, kw_ref, vwt_ref, WINDOW, 1)):
                x = _k_headnorm(src[0, pl.ds(r0, LANES), :].astype(F32), kg_ref[gi:gi + 1, :])
                dst[pl.ds(pad + r0, LANES), :] = x.astype(BF16)
                dst_t[:, pl.ds(pad + r0, LANES)] = jnp.where(ones_rows, 1.0, x.T).astype(BF16)
            return carry

        lax.fori_loop(0, t // LANES, norm_block, 0, unroll=2)

    xq = q_ref[0].astype(F32)
    lane_q = lax.broadcasted_iota(jnp.int32, (1, hpg * dh), 1)
    xq2 = xq * xq
    qz = []
    for p in range(hpg):
        seg = (lane_q >= dh * p) & (lane_q < dh * (p + 1))
        ms = jnp.sum(jnp.where(seg, xq2, 0.0), axis=-1, keepdims=True) * (1.0 / dh)
        c0 = LANES * (p // 2)
        half = xq[:, c0:c0 + LANES] * lax.rsqrt(ms + RMS_EPS) * qg_ref[:, c0:c0 + LANES]
        if p % 2 == 1:
            half = pltpu.roll(half, dh, 1)
        qz.append(jnp.where(klane, half, 0.0).astype(BF16))
    q4 = jnp.concatenate(qz, axis=0)

    def heads(s, add, bias=None, bias_rows=0):
        out = []
        for p in range(hpg):
            sp = s[:, p * TQ:(p + 1) * TQ] + add
            if bias is not None:
                n = sp.shape[0] - bias_rows
                tail = sp[n:, :] + bias[p]
                sp = tail if n == 0 else jnp.concatenate([sp[:n, :], tail], axis=0)
            out.append(sp)
        return jnp.concatenate(out, axis=1)

    per_tile = TQ // CMP_STRIDE
    off = pl.multiple_of((LANES - per_tile) - qt * per_tile, per_tile)
    blk_i = lax.broadcasted_iota(jnp.int32, (LANES, TQ), 0)
    valid_c = (CMP_STRIDE * blk_i + (CMP_LEN - 1)) <= (q0 + lax.broadcasted_iota(jnp.int32, (LANES, TQ), 1))
    kvc = kvc_ref[0, 0]
    s_c = _dot_nt(kvc, q4)
    pcs = []
    for p in range(hpg):
        s = jnp.where(valid_c, s_c[:, p * TQ:(p + 1) * TQ] + bc_ref[0, p, pl.ds(off, LANES), :], NEG)
        m = jnp.max(s, axis=0, keepdims=True)
        e = jnp.where(valid_c, jnp.exp2(s - m), 0.0)
        l = jnp.sum(e, axis=0, keepdims=True)
        pcs.append(e / jnp.where(l > 0.0, l, 1.0))
    vc_t = jnp.where(ones_rows, 1.0, kvc.astype(F32).T).astype(BF16)
    o_cmp = _dot(vc_t, jnp.concatenate(pcs, axis=1).astype(BF16))
    psum = pcs[0]
    for p in range(1, hpg):
        psum = psum + pcs[p]
    p_hi = psum.astype(BF16)
    p_lo = (psum - p_hi.astype(F32)).astype(BF16)
    n_slc = t // SLC_LEN
    imp_t = (_dot(ovt_ref[...], p_hi) + _dot(ovt_ref[...], p_lo))[:n_slc, :]

    jrow = lax.broadcasted_iota(jnp.int32, (n_slc, TQ), 0)
    cur = (q0 + lax.broadcasted_iota(jnp.int32, (n_slc, TQ), 1)) // SLC_LEN
    forced = (jrow == 0) | (jrow == cur) | (jrow == cur - 1)
    vals = jnp.where(forced, SEL_BIG, jnp.where(jrow <= cur, imp_t, -SEL_BIG))
    cnt = jnp.zeros((n_slc, TQ), F32)
    for jp in range(n_slc):
        row = vals[jp:jp + 1, :]
        ahead = (row > vals) | ((row == vals) & (jrow > jp))
        cnt = cnt + jnp.where(ahead, 1.0, 0.0)
    not_sel = jnp.where(cnt < float(N_SEL), 0.0, 1.0)
    not_sel = jnp.concatenate([not_sel, jnp.zeros((LANES - n_slc, TQ), F32)], axis=0).astype(BF16)

    causal = jnp.where(lax.broadcasted_iota(jnp.int32, (TQ, TQ), 0) <= lax.broadcasted_iota(jnp.int32, (TQ, TQ), 1),
                       0.0, NEG)
    has_near = jnp.where(qt >= 1, jnp.zeros((TQ, TQ), F32), NEG)
    bnd = [bnd_ref[0, p] for p in range(hpg)]

    kv_nd = ks_ref[pl.ds(q0, 2 * TQ), :]
    add_nd = _dot(eneg_ref[pl.ds(q0, 2 * TQ), :], not_sel) + jnp.concatenate([has_near, causal], axis=0)
    s = heads(_dot_nt(kv_nd, q4), add_nd, bnd, 2 * TQ)
    m_sel = jnp.max(s, axis=0, keepdims=True)
    acc_sel = _dot(vst_ref[:, pl.ds(q0, 2 * TQ)], jnp.exp2(s - m_sel).astype(BF16))

    ck = CK
    far_end = (qt - 1) * TQ
    key_i = lax.broadcasted_iota(jnp.int32, (ck, TQ), 0)

    def sel_far(c, carry):
        m_old, acc = carry
        r0 = pl.multiple_of(TQ + c * ck, LANES)
        kv = ks_ref[pl.ds(r0, ck), :]
        add = jnp.where(key_i + c * ck < far_end, _dot(eneg_ref[pl.ds(r0, ck), :], not_sel), NEG)
        s = heads(_dot_nt(kv, q4), add)
        m_new = jnp.maximum(m_old, jnp.max(s, axis=0, keepdims=True))
        pv = _dot(vst_ref[:, pl.ds(r0, ck)], jnp.exp2(s - m_new).astype(BF16))
        return m_new, jnp.exp2(m_old - m_new) * acc + pv

    _, acc_sel = lax.fori_loop(0, jnp.maximum(far_end + ck - 1, 0) // ck, sel_far, (m_sel, acc_sel))

    ww = WINDOW + TQ
    kv_w = kw_ref[pl.ds(q0, ww), :]
    jw = lax.broadcasted_iota(jnp.int32, (ww, TQ), 0)
    iw = lax.broadcasted_iota(jnp.int32, (ww, TQ), 1)
    valid_w = (jw > iw) & (jw <= iw + WINDOW) & (jw + (q0 - WINDOW) >= 0)
    s = heads(_dot_nt(kv_w, q4), jnp.where(valid_w, 0.0, NEG), bnd, 2 * TQ)
    m_win = jnp.max(s, axis=0, keepdims=True)
    acc_win = _dot(vwt_ref[:, pl.ds(q0, ww)], jnp.exp2(s - m_win).astype(BF16))

    gl_t = jax.nn.sigmoid(gate_ref[0].astype(F32)).T
    outs = []
    for p in range(hpg):
        sl = slice(p * TQ, (p + 1) * TQ)
        g_c = gl_t[p:p + 1, :]
        g_s = gl_t[hpg + p:hpg + p + 1, :] / acc_sel[0:1, sl]
        g_w = gl_t[2 * hpg + p:2 * hpg + p + 1, :] / acc_win[0:1, sl]
        outs.append(g_c * o_cmp[dh:, sl] + g_s * acc_sel[dh:, sl] + g_w * acc_win[dh:, sl])
    o_ref[0] = jnp.concatenate(outs, axis=0).T.astype(o_ref.dtype)


def _nsa_attention(h3, kvc, qgain, kgain, bias_nd, bias_cmp, overlap_t, eneg):
    b, t, _ = h3.shape
    g, hpg = NSA_GROUPS, NSA_HPG
    n_blk = t // CMP_STRIDE
    qw = hpg * NSA_DH
    return pl.pallas_call(
        functools.partial(_nsa_attn_kernel, t=t),
        grid=(b, g, t // TQ),
        in_specs=[
            pl.BlockSpec((1, TQ, qw), lambda bi, gi, qi: (bi, qi, gi)),
            pl.BlockSpec((1, 1, n_blk, LANES), lambda bi, gi, qi: (bi, gi, 0, 0)),
            pl.BlockSpec((1, t, LANES), lambda bi, gi, qi: (bi, 0, 12 + gi)),
            pl.BlockSpec((1, t, LANES), lambda bi, gi, qi: (bi, 0, 16 + gi)),
            pl.BlockSpec((1, TQ, LANES), lambda bi, gi, qi: (bi, qi, 20 + gi)),
            pl.BlockSpec((1, qw), lambda bi, gi, qi: (0, 0)),
            pl.BlockSpec((2, LANES), lambda bi, gi, qi: (0, 0)),
            pl.BlockSpec((1, hpg, 2 * TQ, TQ), lambda bi, gi, qi: (gi, 0, 0, 0)),
            pl.BlockSpec((1, hpg, 2 * LANES, TQ), lambda bi, gi, qi: (gi, 0, 0, 0)),
            pl.BlockSpec((LANES, LANES), lambda bi, gi, qi: (0, 0)),
            pl.BlockSpec((TQ + t, LANES), lambda bi, gi, qi: (0, 0)),
        ],
        out_specs=pl.BlockSpec((1, TQ, qw), lambda bi, gi, qi: (bi, qi, gi)),
        out_shape=jax.ShapeDtypeStruct((b, t, NSA_HEADS * NSA_DH), BF16),
        scratch_shapes=[pltpu.VMEM((TQ + t, LANES), BF16), pltpu.VMEM((WINDOW + t, LANES), BF16),
                        pltpu.VMEM((LANES, TQ + t), BF16), pltpu.VMEM((LANES, WINDOW + t), BF16)],
        compiler_params=_cparams(("arbitrary", "arbitrary", "arbitrary")),
    )(h3, kvc, h3, h3, h3, qgain, kgain, bias_nd, bias_cmp, overlap_t, eneg)


def _bias_kernel(tbl_ref, nd_ref, bc_ref, ond_ref, obc_ref):
    h = pl.program_id(0)
    far = tbl_ref[NUM_BUCKETS - 1, h]
    nd = nd_ref[...]
    bc = bc_ref[...]
    o_nd = jnp.zeros(nd.shape, F32)
    o_bc = jnp.zeros(bc.shape, F32)
    for bucket in range(NUM_BUCKETS):
        v = tbl_ref[bucket, h]
        o_nd = jnp.where(nd == bucket, (v - far) * LOG2E, o_nd)
        o_bc = jnp.where(bc == bucket, v * LOG2E, o_bc)
    ond_ref[0] = o_nd
    obc_ref[0] = o_bc


def _bias_tiles(rel_bias, nd_bucket, bc_bucket):
    n_heads = rel_bias.shape[1]
    return pl.pallas_call(
        _bias_kernel,
        grid=(n_heads,),
        in_specs=[pl.BlockSpec(memory_space=pltpu.SMEM), _resident(nd_bucket.shape), _resident(bc_bucket.shape)],
        out_specs=[pl.BlockSpec((1,) + nd_bucket.shape, lambda h: (h, 0, 0)),
                   pl.BlockSpec((1,) + bc_bucket.shape, lambda h: (h, 0, 0))],
        out_shape=[jax.ShapeDtypeStruct((n_heads,) + nd_bucket.shape, F32),
                   jax.ShapeDtypeStruct((n_heads,) + bc_bucket.shape, F32)],
        compiler_params=_cparams(("parallel",)),
    )(rel_bias.astype(F32), jnp.asarray(nd_bucket, jnp.int32), jnp.asarray(bc_bucket, jnp.int32))


def _t5_bucket_np(dist):
    n = np.maximum(dist, 0)
    max_exact = NUM_BUCKETS // 2
    nf = np.maximum(n, max_exact).astype(np.float32)
    large = max_exact + (np.log(nf / max_exact) / np.float32(np.log(MAX_DISTANCE / max_exact))
                         * (NUM_BUCKETS - max_exact)).astype(np.int32)
    large = np.minimum(large, NUM_BUCKETS - 1)
    return np.where(n < max_exact, n, large)


def _nsa_constants(t):
    n_cmp = (t - CMP_LEN) // CMP_STRIDE + 1
    n_slc = t // SLC_LEN
    n_blk = t // CMP_STRIDE
    per_tile = TQ // CMP_STRIDE
    assert n_blk <= LANES and per_tile * (t // TQ - 1) <= LANES - per_tile
    iq = np.arange(TQ)
    rel = iq[None, :] - iq[:, None]
    assert _t5_bucket_np(np.array([TQ + 1]))[0] == NUM_BUCKETS - 1
    nd_bucket = _t5_bucket_np(np.concatenate([TQ + rel, rel], axis=0))
    blk_off = np.arange(LANES) - (LANES - per_tile)
    bc_bucket = _t5_bucket_np(iq[None, :] - CMP_STRIDE * blk_off[:, None] - (CMP_LEN - 1))
    bc_bucket = np.concatenate([bc_bucket, bc_bucket], axis=0)
    ci = np.arange(n_blk) * CMP_STRIDE
    sj = np.arange(n_slc) * SLC_LEN
    ov = ((ci[:, None] < sj[None, :] + SLC_LEN) & (ci[:, None] + CMP_LEN > sj[None, :])).astype(np.float32)
    ov[n_cmp:, :] = 0.0
    overlap_t = np.zeros((LANES, LANES), np.float32)
    overlap_t[:n_slc, :n_blk] = ov.T
    eneg = np.zeros((TQ + t, LANES), np.float32)
    eneg[TQ + np.arange(t), np.arange(t) // SLC_LEN] = NEG
    return nd_bucket, bc_bucket, overlap_t, eneg


def _nsa_w_in_columns():
    kv0 = NSA_HEADS * NSA_DH
    kvw = NSA_GROUPS * NSA_DH
    cols = list(range(kv0))
    for br in range(3):
        for g in range(NSA_GROUPS):
            k_src = kv0 + (2 * br) * kvw + g * NSA_DH
            v_src = kv0 + (2 * br + 1) * kvw + g * NSA_DH
            cols += list(range(k_src, k_src + NSA_DH)) + list(range(v_src, v_src + NSA_DH))
    gate0 = kv0 + 6 * kvw
    for g in range(NSA_GROUPS):
        blk = [-1] * LANES
        for br in range(3):
            for p in range(NSA_HPG):
                blk[br * NSA_HPG + p] = gate0 + br * NSA_HEADS + g * NSA_HPG + p
        cols += blk
    return np.asarray(cols, np.int32)


def _retention_layer(x2, b, t, norm_g, w_in, w_out):
    h = _norm_matmul(x2, norm_g, w_in.astype(BF16), tn=1024)
    o = _retention_core(h.reshape(b, t, -1))
    return _matmul_res(o.reshape(b * t, -1), w_out.astype(BF16), x2)


def _nsa_layer(x2, b, t, norm_g, w_in, w_out, q_gain, k_gain, cmp_pos, cmp_w1, cmp_w2, bias_nd,
               bias_cmp, overlap_t, eneg):
    cols = _nsa_w_in_columns()
    w_k = jnp.where(cols[None, :] >= 0, jnp.take(w_in, np.maximum(cols, 0), axis=1), 0.0).astype(BF16)
    h3 = _norm_matmul(x2, norm_g, w_k, tn=1024).reshape(b, t, NSA_COLS)

    dh = NSA_DH
    w1 = cmp_w1.reshape(2, CMP_LEN, dh, CMP_HID)
    zero1 = jnp.zeros((CMP_LEN, dh, CMP_HID), F32)
    w1cat = jnp.concatenate([jnp.concatenate([w1[0], zero1], axis=-1),
                             jnp.concatenate([zero1, w1[1]], axis=-1)], axis=1).astype(BF16)
    zero2 = jnp.zeros((CMP_HID, dh), F32)
    w2cat = jnp.concatenate([jnp.concatenate([cmp_w2[0], zero2], axis=-1),
                             jnp.concatenate([zero2, cmp_w2[1]], axis=-1)], axis=0).astype(BF16)
    pos = jnp.concatenate([cmp_pos[0], cmp_pos[1]], axis=-1)
    ones = jnp.ones((dh,), F32)
    kvc = _compress(h3, pos, w1cat, w2cat, jnp.concatenate([k_gain[0], ones])[None, :])

    qgain = jnp.tile(q_gain * (dh ** -0.5 * LOG2E), NSA_HPG)[None, :]
    kgain = jnp.stack([jnp.concatenate([k_gain[1], ones]), jnp.concatenate([k_gain[2], ones])])
    o = _nsa_attention(h3, kvc, qgain, kgain, bias_nd, bias_cmp, overlap_t, eneg)
    return _matmul_res(o.reshape(b * t, -1), w_out.astype(BF16), x2)


def kernel(x, ffn1_norm, ffn1_w_gu, ffn1_w_down, mix_norm, ffn2_norm, ffn2_w_gu, ffn2_w_down,
           ret_w_in, ret_w_out, nsa_w_in, nsa_w_out, nsa_q_gain, nsa_k_gain,
           nsa_cmp_pos, nsa_cmp_w1, nsa_cmp_w2, rel_bias):
    b, t, d = x.shape
    depth = ffn1_norm.shape[0]
    d_ff = ffn1_w_down.shape[1]
    tf = d_ff
    nf = d_ff // tf

    def ffn_weights(w_gu, w_down):
        wgu = w_gu.astype(BF16).reshape(d, 2, nf, tf).transpose(1, 2, 0, 3)
        return wgu, w_down.astype(BF16).reshape(nf, tf, d)

    nd_bucket, bc_bucket, overlap_t, eneg = _nsa_constants(t)
    bias_nd, bias_cmp = _bias_tiles(rel_bias, nd_bucket, bc_bucket)
    bias_nd = bias_nd.reshape(NSA_GROUPS, NSA_HPG, 2 * TQ, TQ)
    bias_cmp = bias_cmp.reshape(NSA_GROUPS, NSA_HPG, 2 * LANES, TQ)
    overlap_t = jnp.asarray(overlap_t, BF16)
    eneg = jnp.asarray(eneg, BF16)

    x2 = x.reshape(b * t, d)
    for layer in range(depth):
        x2 = _ffn(x2, ffn1_norm[layer][None, :], *ffn_weights(ffn1_w_gu[layer], ffn1_w_down[layer]))
        j = layer // 2
        if layer % 2 == 0:
            x2 = _retention_layer(x2, b, t, mix_norm[layer][None, :], ret_w_in[j], ret_w_out[j])
        else:
            x2 = _nsa_layer(x2, b, t, mix_norm[layer][None, :], nsa_w_in[j], nsa_w_out[j], nsa_q_gain[j],
                            nsa_k_gain[j], nsa_cmp_pos[j], nsa_cmp_w1[j], nsa_cmp_w2[j], bias_nd,
                            bias_cmp, overlap_t, eneg)
        x2 = _ffn(x2, ffn2_norm[layer][None, :], *ffn_weights(ffn2_w_gu[layer], ffn2_w_down[layer]))
    return x2.reshape(b, t, d)
```

```python
import functools

import numpy as np
import jax
import jax.numpy as jnp
from jax import lax
from jax.experimental import pallas as pl
from jax.experimental.pallas import tpu as pltpu

F32 = jnp.float32
BF16 = jnp.bfloat16

RMS_EPS = 1e-6
NEG = -1e30
SEL_BIG = 1e9
LOG2E = 1.4426950408889634

LANES = 128
VMEM_LIMIT = 56 * 1024 * 1024

RET_HEADS = 4
RET_DK = 256
RET_DV = 512
RET_CHUNK = 128
RET_ROPE_BASE = 10000.0

NSA_HEADS = 16
NSA_GROUPS = 4
NSA_HPG = 4
NSA_DH = 64
CMP_LEN = 32
CMP_STRIDE = 16
CMP_HID = 256
SLC_LEN = 64
N_SEL = 8
WINDOW = 512
NUM_BUCKETS = 32
MAX_DISTANCE = 128
TQ = 256
CK = 512
FLAG0 = 64
PAD_LANE = 96
NSA_COLS = 1024 + 3 * 4 * 128 + 4 * 128


def _dot(a, b):
    return jnp.dot(a, b, preferred_element_type=F32)


def _dot_nt(a, b):
    return lax.dot_general(a, b, (((1,), (1,)), ((), ())), preferred_element_type=F32)


def _dot_tn(a, b):
    return lax.dot_general(a, b, (((0,), (0,)), ((), ())), preferred_element_type=F32)


def _rms(x, g):
    ms = jnp.mean(x * x, axis=-1, keepdims=True)
    return x * lax.rsqrt(ms + RMS_EPS) * g


def _cparams(sem):
    return pltpu.CompilerParams(dimension_semantics=sem, vmem_limit_bytes=VMEM_LIMIT)


def _resident(shape):
    n = len(shape)
    return pl.BlockSpec(shape, lambda *_: (0,) * n, pipeline_mode=pl.Buffered(1))


def _ffn_kernel(x_ref, g_ref, wgu_ref, wd_ref, o_ref, *, nf):
    x = x_ref[...]
    xn = _rms(x, g_ref[...]).astype(BF16)
    acc = None
    for j in range(nf):
        a = _dot(xn, wgu_ref[0, j])
        b = _dot(xn, wgu_ref[1, j])
        h = (a * jax.nn.sigmoid(a) * b).astype(BF16)
        part = _dot(h, wd_ref[j])
        acc = part if acc is None else acc + part
    o_ref[...] = x + 0.5 * acc


def _ffn(x2, g, wgu, wd, tm=512):
    n, d = x2.shape
    _, nf, _, tf = wgu.shape
    return pl.pallas_call(
        functools.partial(_ffn_kernel, nf=nf),
        grid=(n // tm,),
        in_specs=[
            pl.BlockSpec((tm, d), lambda i: (i, 0)),
            _resident((1, d)),
            _resident((2, nf, d, tf)),
            _resident((nf, tf, d)),
        ],
        out_specs=pl.BlockSpec((tm, d), lambda i: (i, 0)),
        out_shape=jax.ShapeDtypeStruct((n, d), F32),
        compiler_params=_cparams(("parallel",)),
    )(x2, g, wgu, wd)


def _norm_matmul_kernel(x_ref, g_ref, w_ref, o_ref, *, tn):
    xn = _rms(x_ref[...], g_ref[...]).astype(BF16)
    for j in range(w_ref.shape[1] // tn):
        o_ref[:, j * tn:(j + 1) * tn] = _dot(xn, w_ref[:, j * tn:(j + 1) * tn]).astype(o_ref.dtype)


def _norm_matmul(x2, g, w, tn, tm=512):
    n, d = x2.shape
    nout = w.shape[1]
    return pl.pallas_call(
        functools.partial(_norm_matmul_kernel, tn=tn),
        grid=(n // tm,),
        in_specs=[pl.BlockSpec((tm, d), lambda i: (i, 0)), _resident((1, d)), _resident((d, nout))],
        out_specs=pl.BlockSpec((tm, nout), lambda i: (i, 0)),
        out_shape=jax.ShapeDtypeStruct((n, nout), BF16),
        compiler_params=_cparams(("parallel",)),
    )(x2, g, w)


def _matmul_res_kernel(o_ref, w_ref, x_ref, y_ref):
    y_ref[...] = x_ref[...] + _dot(o_ref[...], w_ref[...])


def _matmul_res(o2, w, x2, tm=512):
    n, k = o2.shape
    d = w.shape[1]
    return pl.pallas_call(
        _matmul_res_kernel,
        grid=(n // tm,),
        in_specs=[pl.BlockSpec((tm, k), lambda i: (i, 0)), _resident((k, d)),
                  pl.BlockSpec((tm, d), lambda i: (i, 0))],
        out_specs=pl.BlockSpec((tm, d), lambda i: (i, 0)),
        out_shape=jax.ShapeDtypeStruct((n, d), F32),
        compiler_params=_cparams(("parallel",)),
    )(o2, w, x2)


def _ret_kernel(cd_ref, q_ref, k_ref, v_ref, g_ref, cos_ref, sin_ref, intra_ref, qd_ref, kd_ref,
                o_ref, s_ref, *, n_chunks):
    c_len = RET_CHUNK
    half = RET_DK // 2
    cd = cd_ref[pl.program_id(1)]
    intra = intra_ref[0]
    qd = qd_ref[0]
    kd = kd_ref[0]
    s_ref[...] = jnp.zeros_like(s_ref)

    def chunk(c, carry):
        r0 = pl.multiple_of(c * c_len, c_len)
        cos = cos_ref[pl.ds(r0, c_len), :]
        sin = sin_ref[pl.ds(r0, c_len), :]

        def rot(x):
            x1, x2 = x[:, :half], x[:, half:]
            return jnp.concatenate([x1 * cos - x2 * sin, x1 * sin + x2 * cos], axis=-1)

        qr = rot(q_ref[0, pl.ds(r0, c_len), :].astype(F32))
        kr = rot(k_ref[0, pl.ds(r0, c_len), :].astype(F32)) * (RET_DK ** -0.5)
        vc = v_ref[0, pl.ds(r0, c_len), :]
        s = _dot_nt(qr.astype(BF16), kr.astype(BF16)) * intra
        state = s_ref[...]
        o = _dot(s.astype(BF16), vc) + _dot((qr * qd).astype(BF16), state.astype(BF16))
        s_ref[...] = cd * state + _dot_tn((kr * kd).astype(BF16), vc)
        o = o * lax.rsqrt(jnp.mean(o * o, axis=-1, keepdims=True) + RMS_EPS)
        gate = g_ref[0, pl.ds(r0, c_len), :].astype(F32)
        o_ref[0, pl.ds(r0, c_len), :] = (gate * jax.nn.sigmoid(gate) * o).astype(o_ref.dtype)
        return carry

    lax.fori_loop(0, n_chunks, chunk, 0, unroll=2)


def _retention_core(h3):
    b, t, _ = h3.shape
    hh, dk, dv, c = RET_HEADS, RET_DK, RET_DV, RET_CHUNK
    half = dk // 2
    pos = jnp.arange(t, dtype=F32)
    inv_freq = RET_ROPE_BASE ** (-jnp.arange(half, dtype=F32) / half)
    ang = pos[:, None] * inv_freq[None, :]
    cos, sin = jnp.cos(ang), jnp.sin(ang)
    log_gamma = jnp.log(1.0 - 2.0 ** (-5.0 - jnp.arange(hh, dtype=F32)))
    i = jnp.arange(c)
    diff = i[:, None] - i[None, :]
    intra = jnp.where(diff >= 0, jnp.exp(log_gamma[:, None, None] * jnp.maximum(diff, 0)), 0.0)
    q_dec = jnp.exp(log_gamma[:, None] * (i + 1))[:, :, None]
    k_dec = jnp.exp(log_gamma[:, None] * (c - 1 - i))[:, :, None]
    chunk_dec = jnp.exp(log_gamma * c)
    nq = hh * dk // dk
    return pl.pallas_call(
        functools.partial(_ret_kernel, n_chunks=t // c),
        grid=(b, hh),
        in_specs=[
            pl.BlockSpec(memory_space=pltpu.SMEM),
            pl.BlockSpec((1, t, dk), lambda bi, hi: (bi, 0, hi)),
            pl.BlockSpec((1, t, dk), lambda bi, hi: (bi, 0, nq + hi)),
            pl.BlockSpec((1, t, dv), lambda bi, hi: (bi, 0, nq + hi)),
            pl.BlockSpec((1, t, dv), lambda bi, hi: (bi, 0, 2 * nq + hi)),
            _resident((t, half)),
            _resident((t, half)),
            pl.BlockSpec((1, c, c), lambda bi, hi: (hi, 0, 0)),
            pl.BlockSpec((1, c, 1), lambda bi, hi: (hi, 0, 0)),
            pl.BlockSpec((1, c, 1), lambda bi, hi: (hi, 0, 0)),
        ],
        out_specs=pl.BlockSpec((1, t, dv), lambda bi, hi: (bi, 0, hi)),
        out_shape=jax.ShapeDtypeStruct((b, t, hh * dv), BF16),
        scratch_shapes=[pltpu.VMEM((dk, dv), F32)],
        compiler_params=_cparams(("parallel", "arbitrary")),
    )(chunk_dec, h3, h3, h3, h3, cos, sin, intra, q_dec, k_dec)


def _k_headnorm(x, gain_row):
    klane = lax.broadcasted_iota(jnp.int32, x.shape, 1) < NSA_DH
    ms = jnp.sum(jnp.where(klane, x * x, 0.0), axis=-1, keepdims=True) * (1.0 / NSA_DH)
    return jnp.where(klane, x * lax.rsqrt(ms + RMS_EPS) * gain_row, x)


def _cmp_kernel(a_ref, pos_ref, w1_ref, w2_ref, kg_ref, o_ref, af_ref, *, t):
    n_blk = t // CMP_STRIDE
    af_ref[0:t, :] = a_ref[0].astype(F32)
    af_ref[t:t + CMP_LEN, :] = jnp.zeros((CMP_LEN, LANES), F32)
    acc = jnp.zeros((n_blk, 2 * CMP_HID), F32)
    for l in range(CMP_LEN):
        rows = af_ref[pl.ds(l, n_blk, stride=CMP_STRIDE), :]
        acc = acc + _dot((rows + pos_ref[l:l + 1, :]).astype(BF16), w1_ref[l])
    hid = jax.nn.gelu(acc, approximate=True)
    out = _dot(hid.astype(BF16), w2_ref[...])
    o_ref[0, 0] = _k_headnorm(out, kg_ref[...]).astype(o_ref.dtype)


def _compress(h3, pos, w1cat, w2cat, kgain_row):
    b, t, _ = h3.shape
    n_blk = t // CMP_STRIDE
    return pl.pallas_call(
        functools.partial(_cmp_kernel, t=t),
        grid=(b, NSA_GROUPS),
        in_specs=[
            pl.BlockSpec((1, t, LANES), lambda bi, gi: (bi, 0, 8 + gi)),
            _resident((CMP_LEN, LANES)),
            _resident((CMP_LEN, LANES, 2 * CMP_HID)),
            _resident((2 * CMP_HID, LANES)),
            _resident((1, LANES)),
        ],
        out_specs=pl.BlockSpec((1, 1, n_blk, LANES), lambda bi, gi: (bi, gi, 0, 0)),
        out_shape=jax.ShapeDtypeStruct((b, NSA_GROUPS, n_blk, LANES), BF16),
        scratch_shapes=[pltpu.VMEM((t + CMP_LEN, LANES), F32)],
        compiler_params=_cparams(("parallel", "parallel")),
    )(h3, pos, w1cat, w2cat, kgain_row)


def _nsa_attn_kernel(q_ref, kvc_ref, kvs_ref, kvw_ref, gate_ref, qg_ref, kg_ref, bnd_ref, bw_ref, bc_ref,
                     ovt_ref, kflag_ref, o_ref, ks_ref, kw_ref, vst_ref, vwt_ref, sel_ref, *, t):
    hpg = NSA_HPG
    dh = NSA_DH
    qt = pl.program_id(2)
    q0 = pl.multiple_of(qt * TQ, TQ)
    lane = lax.broadcasted_iota(jnp.int32, (1, LANES), 1)
    klane = lane < dh
    ones_rows = lax.broadcasted_iota(jnp.int32, (LANES, LANES), 0) < dh

    @pl.when(qt == 0)
    def _():
        pad_rows = jnp.where(lane == PAD_LANE, 1.0, 0.0).astype(BF16)
        ks_ref[0:TQ, :] = jnp.broadcast_to(pad_rows, (TQ, LANES))
        kw_ref[0:WINDOW, :] = jnp.broadcast_to(pad_rows, (WINDOW, LANES))
        pad_col = jnp.where(ones_rows, 1.0, 0.0).astype(BF16)
        for i in range(TQ // LANES):
            vst_ref[:, i * LANES:(i + 1) * LANES] = pad_col
        for i in range(WINDOW // LANES):
            vwt_ref[:, i * LANES:(i + 1) * LANES] = pad_col

        def norm_block(c, carry):
            r0 = pl.multiple_of(c * LANES, LANES)
            for src, dst, dst_t, pad, gi in ((kvs_ref, ks_ref, vst_ref, TQ, 0), (kvw_ref, kw_ref, vwt_ref, WINDOW, 1)):
                x = _k_headnorm(src[0, pl.ds(r0, LANES), :].astype(F32), kg_ref[gi:gi + 1, :])
                flags = kflag_ref[pl.ds(r0, LANES), :] if gi == 0 else jnp.zeros((LANES, LANES), BF16)
                dst[pl.ds(pad + r0, LANES), :] = jnp.where(klane, x.astype(BF16), flags)
                dst_t[:, pl.ds(pad + r0, LANES)] = jnp.where(ones_rows, 1.0, x.T).astype(BF16)
            return carry

        lax.fori_loop(0, t // LANES, norm_block, 0, unroll=2)

    xq = q_ref[0].astype(F32)
    lane_q = lax.broadcasted_iota(jnp.int32, (1, hpg * dh), 1)
    xq2 = xq * xq
    qz = []
    for p in range(hpg):
        seg = (lane_q >= dh * p) & (lane_q < dh * (p + 1))
        ms = jnp.sum(jnp.where(seg, xq2, 0.0), axis=-1, keepdims=True) * (1.0 / dh)
        c0 = LANES * (p // 2)
        half = xq[:, c0:c0 + LANES] * lax.rsqrt(ms + RMS_EPS) * qg_ref[:, c0:c0 + LANES]
        if p % 2 == 1:
            half = pltpu.roll(half, dh, 1)
        qz.append(jnp.where(klane, half, 0.0))

    def stacked_q(flag_lanes):
        return jnp.concatenate([(qp + flag_lanes).astype(BF16) for qp in qz], axis=0)

    ww = WINDOW + TQ
    s_w = _dot_nt(kw_ref[pl.ds(q0, ww), :], stacked_q(jnp.where(lane == PAD_LANE, NEG, 0.0)))
    s = jnp.concatenate([s_w[:, p * TQ:(p + 1) * TQ] + bw_ref[0, p] for p in range(hpg)], axis=1)
    m_win = jnp.max(s, axis=0, keepdims=True)
    acc_win = _dot(vwt_ref[:, pl.ds(q0, ww)], jnp.exp2(s - m_win).astype(BF16))

    per_tile = TQ // CMP_STRIDE
    off = pl.multiple_of((LANES - per_tile) - qt * per_tile, per_tile)
    blk_i = lax.broadcasted_iota(jnp.int32, (LANES, TQ), 0)
    valid_c = (CMP_STRIDE * blk_i + (CMP_LEN - 1)) <= (q0 + lax.broadcasted_iota(jnp.int32, (LANES, TQ), 1))
    kvc = kvc_ref[0, 0]
    s_c = _dot_nt(kvc, stacked_q(0.0))
    pcs = []
    for p in range(hpg):
        s = jnp.where(valid_c, s_c[:, p * TQ:(p + 1) * TQ] + bc_ref[0, p, pl.ds(off, LANES), :], NEG)
        m = jnp.max(s, axis=0, keepdims=True)
        e = jnp.where(valid_c, jnp.exp2(s - m), 0.0)
        l = jnp.sum(e, axis=0, keepdims=True)
        pcs.append(e / jnp.where(l > 0.0, l, 1.0))
    vc_t = jnp.where(ones_rows, 1.0, kvc.astype(F32).T).astype(BF16)
    o_cmp = _dot(vc_t, jnp.concatenate(pcs, axis=1).astype(BF16))
    psum = pcs[0]
    for p in range(1, hpg):
        psum = psum + pcs[p]
    p_hi = psum.astype(BF16)
    p_lo = (psum - p_hi.astype(F32)).astype(BF16)
    n_slc = t // SLC_LEN
    imp_t = (_dot(ovt_ref[...], p_hi) + _dot(ovt_ref[...], p_lo))[:n_slc, :]

    jrow = lax.broadcasted_iota(jnp.int32, (n_slc, TQ), 0)
    cur = (q0 + lax.broadcasted_iota(jnp.int32, (n_slc, TQ), 1)) // SLC_LEN
    forced = (jrow == 0) | (jrow == cur) | (jrow == cur - 1)
    vals = jnp.where(forced, SEL_BIG, jnp.where(jrow <= cur, imp_t, -SEL_BIG))
    cnt = jnp.zeros((n_slc, TQ), F32)
    for jp in range(n_slc):
        row = vals[jp:jp + 1, :]
        ahead = (row > vals) | ((row == vals) & (jrow > jp))
        cnt = cnt + jnp.where(ahead, 1.0, 0.0)

    far_end = (qt - 1) * TQ
    drop_nd = jnp.where(cnt < float(N_SEL), 0.0, NEG)
    drop_far = jnp.where(jrow * SLC_LEN < far_end, drop_nd, NEG)

    def flag_lanes(drop):
        rows = [jnp.zeros((FLAG0, TQ), F32), drop, jnp.full((PAD_LANE - FLAG0 - n_slc + 8, TQ), NEG, F32),
                jnp.zeros((LANES - PAD_LANE - 8, TQ), F32)]
        return jnp.concatenate(rows, axis=0).T

    ck = CK
    q_nd = stacked_q(flag_lanes(drop_nd))
    q_far = stacked_q(flag_lanes(drop_far))
    n_far = jnp.maximum(far_end + ck - 1, 0) // ck

    def select_branch(k):
        s_nd = _dot_nt(ks_ref[pl.ds(q0, 2 * TQ), :], q_nd)
        scores = [jnp.concatenate([s_nd[:, p * TQ:(p + 1) * TQ] + bnd_ref[0, p] for p in range(hpg)], axis=1)]
        values = [vst_ref[:, pl.ds(q0, 2 * TQ)]]
        for c in range(k):
            scores.append(_dot_nt(ks_ref[TQ + c * ck:TQ + (c + 1) * ck, :], q_far))
            values.append(vst_ref[:, TQ + c * ck:TQ + (c + 1) * ck])
        m = jnp.max(scores[0], axis=0, keepdims=True)
        for s in scores[1:]:
            m = jnp.maximum(m, jnp.max(s, axis=0, keepdims=True))
        acc = _dot(values[0], jnp.exp2(scores[0] - m).astype(BF16))
        for s, v in zip(scores[1:], values[1:]):
            acc = acc + _dot(v, jnp.exp2(s - m).astype(BF16))
        sel_ref[...] = acc

    for k in range((t - 2 * TQ) // ck + 1):
        pl.when(n_far == k)(functools.partial(select_branch, k))
    acc_sel = sel_ref[...]

    gl_t = jax.nn.sigmoid(gate_ref[0].astype(F32)).T
    outs = []
    for p in range(hpg):
        sl = slice(p * TQ, (p + 1) * TQ)
        g_c = gl_t[p:p + 1, :]
        g_s = gl_t[hpg + p:hpg + p + 1, :] / acc_sel[0:1, sl]
        g_w = gl_t[2 * hpg + p:2 * hpg + p + 1, :] / acc_win[0:1, sl]
        outs.append(g_c * o_cmp[dh:, sl] + g_s * acc_sel[dh:, sl] + g_w * acc_win[dh:, sl])
    o_ref[0] = jnp.concatenate(outs, axis=0).T.astype(o_ref.dtype)


def _nsa_attention(h3, kvc, qgain, kgain, bias_nd, bias_w, bias_cmp, overlap_t, kflag):
    b, t, _ = h3.shape
    g, hpg = NSA_GROUPS, NSA_HPG
    n_blk = t // CMP_STRIDE
    qw = hpg * NSA_DH
    return pl.pallas_call(
        functools.partial(_nsa_attn_kernel, t=t),
        grid=(b, g, t // TQ),
        in_specs=[
            pl.BlockSpec((1, TQ, qw), lambda bi, gi, qi: (bi, qi, gi)),
            pl.BlockSpec((1, 1, n_blk, LANES), lambda bi, gi, qi: (bi, gi, 0, 0)),
            pl.BlockSpec((1, t, LANES), lambda bi, gi, qi: (bi, 0, 12 + gi)),
            pl.BlockSpec((1, t, LANES), lambda bi, gi, qi: (bi, 0, 16 + gi)),
            pl.BlockSpec((1, TQ, LANES), lambda bi, gi, qi: (bi, qi, 20 + gi)),
            pl.BlockSpec((1, qw), lambda bi, gi, qi: (0, 0)),
            pl.BlockSpec((2, LANES), lambda bi, gi, qi: (0, 0)),
            pl.BlockSpec((1, hpg, 2 * TQ, TQ), lambda bi, gi, qi: (gi, 0, 0, 0)),
            pl.BlockSpec((1, hpg, WINDOW + TQ, TQ), lambda bi, gi, qi: (gi, 0, 0, 0)),
            pl.BlockSpec((1, hpg, 2 * LANES, TQ), lambda bi, gi, qi: (gi, 0, 0, 0)),
            pl.BlockSpec((LANES, LANES), lambda bi, gi, qi: (0, 0)),
            pl.BlockSpec((t, LANES), lambda bi, gi, qi: (0, 0)),
        ],
        out_specs=pl.BlockSpec((1, TQ, qw), lambda bi, gi, qi: (bi, qi, gi)),
        out_shape=jax.ShapeDtypeStruct((b, t, NSA_HEADS * NSA_DH), BF16),
        scratch_shapes=[pltpu.VMEM((TQ + t, LANES), BF16), pltpu.VMEM((WINDOW + t, LANES), BF16),
                        pltpu.VMEM((LANES, TQ + t), BF16), pltpu.VMEM((LANES, WINDOW + t), BF16),
                        pltpu.VMEM((LANES, hpg * TQ), F32)],
        compiler_params=_cparams(("arbitrary", "arbitrary", "arbitrary")),
    )(h3, kvc, h3, h3, h3, qgain, kgain, bias_nd, bias_w, bias_cmp, overlap_t, kflag)


def _bias_kernel(tbl_ref, nd_ref, w_ref, bc_ref, ond_ref, ow_ref, obc_ref):
    h = pl.program_id(0)
    far = tbl_ref[NUM_BUCKETS - 1, h]
    nd = nd_ref[...]
    w = w_ref[...]
    bc = bc_ref[...]
    o_nd = jnp.full(nd.shape, NEG, F32)
    o_w = jnp.full(w.shape, NEG, F32)
    o_bc = jnp.zeros(bc.shape, F32)
    for bucket in range(NUM_BUCKETS):
        v = tbl_ref[bucket, h]
        o_nd = jnp.where(nd == bucket, (v - far) * LOG2E, o_nd)
        o_w = jnp.where(w == bucket, (v - far) * LOG2E, o_w)
        o_bc = jnp.where(bc == bucket, v * LOG2E, o_bc)
    ond_ref[0] = o_nd
    ow_ref[0] = o_w
    obc_ref[0] = o_bc


def _bias_tiles(rel_bias, nd_bucket, w_bucket, bc_bucket):
    n_heads = rel_bias.shape[1]
    tables = (nd_bucket, w_bucket, bc_bucket)
    return pl.pallas_call(
        _bias_kernel,
        grid=(n_heads,),
        in_specs=[pl.BlockSpec(memory_space=pltpu.SMEM)] + [_resident(a.shape) for a in tables],
        out_specs=[pl.BlockSpec((1,) + a.shape, lambda h: (h, 0, 0)) for a in tables],
        out_shape=[jax.ShapeDtypeStruct((n_heads,) + a.shape, F32) for a in tables],
        compiler_params=_cparams(("parallel",)),
    )(rel_bias.astype(F32), *[jnp.asarray(a, jnp.int32) for a in tables])


def _t5_bucket_np(dist):
    n = np.maximum(dist, 0)
    max_exact = NUM_BUCKETS // 2
    nf = np.maximum(n, max_exact).astype(np.float32)
    large = max_exact + (np.log(nf / max_exact) / np.float32(np.log(MAX_DISTANCE / max_exact))
                         * (NUM_BUCKETS - max_exact)).astype(np.int32)
    large = np.minimum(large, NUM_BUCKETS - 1)
    return np.where(n < max_exact, n, large)


def _nsa_constants(t):
    n_cmp = (t - CMP_LEN) // CMP_STRIDE + 1
    n_slc = t // SLC_LEN
    n_blk = t // CMP_STRIDE
    per_tile = TQ // CMP_STRIDE
    assert n_blk <= LANES and per_tile * (t // TQ - 1) <= LANES - per_tile
    iq = np.arange(TQ)
    rel = iq[None, :] - iq[:, None]
    assert _t5_bucket_np(np.array([TQ + 1]))[0] == NUM_BUCKETS - 1
    dist_nd = np.concatenate([TQ + rel, rel], axis=0)
    nd_bucket = np.where(dist_nd >= 0, _t5_bucket_np(dist_nd), -1)
    dist_w = np.arange(TQ)[None, :] + WINDOW - np.arange(WINDOW + TQ)[:, None]
    w_bucket = np.where((dist_w >= 0) & (dist_w < WINDOW), _t5_bucket_np(dist_w), -1)
    blk_off = np.arange(LANES) - (LANES - per_tile)
    bc_bucket = _t5_bucket_np(iq[None, :] - CMP_STRIDE * blk_off[:, None] - (CMP_LEN - 1))
    bc_bucket = np.concatenate([bc_bucket, bc_bucket], axis=0)
    ci = np.arange(n_blk) * CMP_STRIDE
    sj = np.arange(n_slc) * SLC_LEN
    ov = ((ci[:, None] < sj[None, :] + SLC_LEN) & (ci[:, None] + CMP_LEN > sj[None, :])).astype(np.float32)
    ov[n_cmp:, :] = 0.0
    overlap_t = np.zeros((LANES, LANES), np.float32)
    overlap_t[:n_slc, :n_blk] = ov.T
    assert FLAG0 + n_slc <= PAD_LANE
    kflag = np.zeros((t, LANES), np.float32)
    kflag[np.arange(t), FLAG0 + np.arange(t) // SLC_LEN] = 1.0
    return nd_bucket, w_bucket, bc_bucket, overlap_t, kflag


def _nsa_w_in_columns():
    kv0 = NSA_HEADS * NSA_DH
    kvw = NSA_GROUPS * NSA_DH
    cols = list(range(kv0))
    for br in range(3):
        for g in range(NSA_GROUPS):
            k_src = kv0 + (2 * br) * kvw + g * NSA_DH
            v_src = kv0 + (2 * br + 1) * kvw + g * NSA_DH
            cols += list(range(k_src, k_src + NSA_DH)) + list(range(v_src, v_src + NSA_DH))
    gate0 = kv0 + 6 * kvw
    for g in range(NSA_GROUPS):
        blk = [-1] * LANES
        for br in range(3):
            for p in range(NSA_HPG):
                blk[br * NSA_HPG + p] = gate0 + br * NSA_HEADS + g * NSA_HPG + p
        cols += blk
    return np.asarray(cols, np.int32)


def _retention_layer(x2, b, t, norm_g, w_in, w_out):
    h = _norm_matmul(x2, norm_g, w_in.astype(BF16), tn=1024)
    o = _retention_core(h.reshape(b, t, -1))
    return _matmul_res(o.reshape(b * t, -1), w_out.astype(BF16), x2)


def _nsa_layer(x2, b, t, norm_g, w_in, w_out, q_gain, k_gain, cmp_pos, cmp_w1, cmp_w2, bias_nd, bias_w,
               bias_cmp, overlap_t, kflag):
    cols = _nsa_w_in_columns()
    w_k = jnp.where(cols[None, :] >= 0, jnp.take(w_in, np.maximum(cols, 0), axis=1), 0.0).astype(BF16)
    h3 = _norm_matmul(x2, norm_g, w_k, tn=1024).reshape(b, t, NSA_COLS)

    dh = NSA_DH
    w1 = cmp_w1.reshape(2, CMP_LEN, dh, CMP_HID)
    zero1 = jnp.zeros((CMP_LEN, dh, CMP_HID), F32)
    w1cat = jnp.concatenate([jnp.concatenate([w1[0], zero1], axis=-1),
                             jnp.concatenate([zero1, w1[1]], axis=-1)], axis=1).astype(BF16)
    zero2 = jnp.zeros((CMP_HID, dh), F32)
    w2cat = jnp.concatenate([jnp.concatenate([cmp_w2[0], zero2], axis=-1),
                             jnp.concatenate([zero2, cmp_w2[1]], axis=-1)], axis=0).astype(BF16)
    pos = jnp.concatenate([cmp_pos[0], cmp_pos[1]], axis=-1)
    ones = jnp.ones((dh,), F32)
    kvc = _compress(h3, pos, w1cat, w2cat, jnp.concatenate([k_gain[0], ones])[None, :])

    qgain = jnp.tile(q_gain * (dh ** -0.5 * LOG2E), NSA_HPG)[None, :]
    kgain = jnp.stack([jnp.concatenate([k_gain[1], ones]), jnp.concatenate([k_gain[2], ones])])
    o = _nsa_attention(h3, kvc, qgain, kgain, bias_nd, bias_w, bias_cmp, overlap_t, kflag)
    return _matmul_res(o.reshape(b * t, -1), w_out.astype(BF16), x2)


def kernel(x, ffn1_norm, ffn1_w_gu, ffn1_w_down, mix_norm, ffn2_norm, ffn2_w_gu, ffn2_w_down,
           ret_w_in, ret_w_out, nsa_w_in, nsa_w_out, nsa_q_gain, nsa_k_gain,
           nsa_cmp_pos, nsa_cmp_w1, nsa_cmp_w2, rel_bias):
    b, t, d = x.shape
    depth = ffn1_norm.shape[0]
    d_ff = ffn1_w_down.shape[1]
    tf = d_ff
    nf = d_ff // tf

    def ffn_weights(w_gu, w_down):
        wgu = w_gu.astype(BF16).reshape(d, 2, nf, tf).transpose(1, 2, 0, 3)
        return wgu, w_down.astype(BF16).reshape(nf, tf, d)

    nd_bucket, w_bucket, bc_bucket, overlap_t, kflag = _nsa_constants(t)
    bias_nd, bias_w, bias_cmp = _bias_tiles(rel_bias, nd_bucket, w_bucket, bc_bucket)
    bias_nd = bias_nd.reshape(NSA_GROUPS, NSA_HPG, 2 * TQ, TQ)
    bias_w = bias_w.reshape(NSA_GROUPS, NSA_HPG, WINDOW + TQ, TQ)
    bias_cmp = bias_cmp.reshape(NSA_GROUPS, NSA_HPG, 2 * LANES, TQ)
    overlap_t = jnp.asarray(overlap_t, BF16)
    kflag = jnp.asarray(kflag, BF16)

    x2 = x.reshape(b * t, d)
    for layer in range(depth):
        x2 = _ffn(x2, ffn1_norm[layer][None, :], *ffn_weights(ffn1_w_gu[layer], ffn1_w_down[layer]))
        j = layer // 2
        if layer % 2 == 0:
            x2 = _retention_layer(x2, b, t, mix_norm[layer][None, :], ret_w_in[j], ret_w_out[j])
        else:
            x2 = _nsa_layer(x2, b, t, mix_norm[layer][None, :], nsa_w_in[j], nsa_w_out[j], nsa_q_gain[j],
                            nsa_k_gain[j], nsa_cmp_pos[j], nsa_cmp_w1[j], nsa_cmp_w2[j], bias_nd, bias_w,
                            bias_cmp, overlap_t, kflag)
        x2 = _ffn(x2, ffn2_norm[layer][None, :], *ffn_weights(ffn2_w_gu[layer], ffn2_w_down[layer]))
    return x2.reshape(b, t, d)
```

```python
import functools

import numpy as np
import jax
import jax.numpy as jnp
from jax import lax
from jax.experimental import pallas as pl
from jax.experimental.pallas import tpu as pltpu

F32 = jnp.float32
BF16 = jnp.bfloat16

RMS_EPS = 1e-6
NEG = -1e30
SEL_BIG = 1e9
LOG2E = 1.4426950408889634

LANES = 128
VMEM_LIMIT = 56 * 1024 * 1024

RET_HEADS = 4
RET_DK = 256
RET_DV = 512
RET_CHUNK = 128
RET_ROPE_BASE = 10000.0

NSA_HEADS = 16
NSA_GROUPS = 4
NSA_HPG = 4
NSA_DH = 64
CMP_LEN = 32
CMP_STRIDE = 16
CMP_HID = 256
SLC_LEN = 64
N_SEL = 8
WINDOW = 512
NUM_BUCKETS = 32
MAX_DISTANCE = 128
TQ = 256
CK = 512
FLAG0 = 64
PAD_LANE = 96
NSA_COLS = 1024 + 3 * 4 * 128 + 4 * 128


def _dot(a, b):
    return jnp.dot(a, b, preferred_element_type=F32)


def _dot_nt(a, b):
    return lax.dot_general(a, b, (((1,), (1,)), ((), ())), preferred_element_type=F32)


def _dot_tn(a, b):
    return lax.dot_general(a, b, (((0,), (0,)), ((), ())), preferred_element_type=F32)


def _rms(x, g):
    ms = jnp.mean(x * x, axis=-1, keepdims=True)
    return x * lax.rsqrt(ms + RMS_EPS) * g


def _cparams(sem):
    return pltpu.CompilerParams(dimension_semantics=sem, vmem_limit_bytes=VMEM_LIMIT)


def _resident(shape):
    n = len(shape)
    return pl.BlockSpec(shape, lambda *_: (0,) * n, pipeline_mode=pl.Buffered(1))


def _ffn_kernel(x_ref, g_ref, wgu_ref, wd_ref, o_ref, *, nf):
    x = x_ref[...]
    xn = _rms(x, g_ref[...]).astype(BF16)
    acc = None
    for j in range(nf):
        a = _dot(xn, wgu_ref[0, j])
        b = _dot(xn, wgu_ref[1, j])
        h = (a * jax.nn.sigmoid(a) * b).astype(BF16)
        part = _dot(h, wd_ref[j])
        acc = part if acc is None else acc + part
    o_ref[...] = x + 0.5 * acc


def _ffn(x2, g, wgu, wd, tm=512):
    n, d = x2.shape
    _, nf, _, tf = wgu.shape
    return pl.pallas_call(
        functools.partial(_ffn_kernel, nf=nf),
        grid=(n // tm,),
        in_specs=[
            pl.BlockSpec((tm, d), lambda i: (i, 0)),
            _resident((1, d)),
            _resident((2, nf, d, tf)),
            _resident((nf, tf, d)),
        ],
        out_specs=pl.BlockSpec((tm, d), lambda i: (i, 0)),
        out_shape=jax.ShapeDtypeStruct((n, d), F32),
        compiler_params=_cparams(("parallel",)),
    )(x2, g, wgu, wd)


def _norm_matmul_kernel(x_ref, g_ref, w_ref, o_ref, *, tn):
    xn = _rms(x_ref[...], g_ref[...]).astype(BF16)
    for j in range(w_ref.shape[1] // tn):
        o_ref[:, j * tn:(j + 1) * tn] = _dot(xn, w_ref[:, j * tn:(j + 1) * tn]).astype(o_ref.dtype)


def _norm_matmul(x2, g, w, tn, tm=512):
    n, d = x2.shape
    nout = w.shape[1]
    return pl.pallas_call(
        functools.partial(_norm_matmul_kernel, tn=tn),
        grid=(n // tm,),
        in_specs=[pl.BlockSpec((tm, d), lambda i: (i, 0)), _resident((1, d)), _resident((d, nout))],
        out_specs=pl.BlockSpec((tm, nout), lambda i: (i, 0)),
        out_shape=jax.ShapeDtypeStruct((n, nout), BF16),
        compiler_params=_cparams(("parallel",)),
    )(x2, g, w)


def _matmul_res_kernel(o_ref, w_ref, x_ref, y_ref):
    y_ref[...] = x_ref[...] + _dot(o_ref[...], w_ref[...])


def _matmul_res(o2, w, x2, tm=512):
    n, k = o2.shape
    d = w.shape[1]
    return pl.pallas_call(
        _matmul_res_kernel,
        grid=(n // tm,),
        in_specs=[pl.BlockSpec((tm, k), lambda i: (i, 0)), _resident((k, d)),
                  pl.BlockSpec((tm, d), lambda i: (i, 0))],
        out_specs=pl.BlockSpec((tm, d), lambda i: (i, 0)),
        out_shape=jax.ShapeDtypeStruct((n, d), F32),
        compiler_params=_cparams(("parallel",)),
    )(o2, w, x2)


def _ret_kernel(cd_ref, q_ref, k_ref, v_ref, g_ref, cos_ref, sin_ref, intra_ref, qd_ref, kd_ref,
                o_ref, s_ref, *, n_chunks):
    c_len = RET_CHUNK
    half = RET_DK // 2
    cd = cd_ref[pl.program_id(1)]
    intra = intra_ref[0]
    qd = qd_ref[0]
    kd = kd_ref[0]
    s_ref[...] = jnp.zeros_like(s_ref)

    def chunk(c, carry):
        r0 = pl.multiple_of(c * c_len, c_len)
        cos = cos_ref[pl.ds(r0, c_len), :]
        sin = sin_ref[pl.ds(r0, c_len), :]

        def rot(x):
            x1, x2 = x[:, :half], x[:, half:]
            return jnp.concatenate([x1 * cos - x2 * sin, x1 * sin + x2 * cos], axis=-1)

        qr = rot(q_ref[0, pl.ds(r0, c_len), :].astype(F32))
        kr = rot(k_ref[0, pl.ds(r0, c_len), :].astype(F32)) * (RET_DK ** -0.5)
        vc = v_ref[0, pl.ds(r0, c_len), :]
        s = _dot_nt(qr.astype(BF16), kr.astype(BF16)) * intra
        state = s_ref[...]
        o = _dot(s.astype(BF16), vc) + _dot((qr * qd).astype(BF16), state.astype(BF16))
        s_ref[...] = cd * state + _dot_tn((kr * kd).astype(BF16), vc)
        o = o * lax.rsqrt(jnp.mean(o * o, axis=-1, keepdims=True) + RMS_EPS)
        gate = g_ref[0, pl.ds(r0, c_len), :].astype(F32)
        o_ref[0, pl.ds(r0, c_len), :] = (gate * jax.nn.sigmoid(gate) * o).astype(o_ref.dtype)
        return carry

    lax.fori_loop(0, n_chunks, chunk, 0, unroll=4)


def _retention_core(h3):
    b, t, _ = h3.shape
    hh, dk, dv, c = RET_HEADS, RET_DK, RET_DV, RET_CHUNK
    half = dk // 2
    pos = jnp.arange(t, dtype=F32)
    inv_freq = RET_ROPE_BASE ** (-jnp.arange(half, dtype=F32) / half)
    ang = pos[:, None] * inv_freq[None, :]
    cos, sin = jnp.cos(ang), jnp.sin(ang)
    log_gamma = jnp.log(1.0 - 2.0 ** (-5.0 - jnp.arange(hh, dtype=F32)))
    i = jnp.arange(c)
    diff = i[:, None] - i[None, :]
    intra = jnp.where(diff >= 0, jnp.exp(log_gamma[:, None, None] * jnp.maximum(diff, 0)), 0.0)
    q_dec = jnp.exp(log_gamma[:, None] * (i + 1))[:, :, None]
    k_dec = jnp.exp(log_gamma[:, None] * (c - 1 - i))[:, :, None]
    chunk_dec = jnp.exp(log_gamma * c)
    nq = hh * dk // dk
    return pl.pallas_call(
        functools.partial(_ret_kernel, n_chunks=t // c),
        grid=(b, hh),
        in_specs=[
            pl.BlockSpec(memory_space=pltpu.SMEM),
            pl.BlockSpec((1, t, dk), lambda bi, hi: (bi, 0, hi)),
            pl.BlockSpec((1, t, dk), lambda bi, hi: (bi, 0, nq + hi)),
            pl.BlockSpec((1, t, dv), lambda bi, hi: (bi, 0, nq + hi)),
            pl.BlockSpec((1, t, dv), lambda bi, hi: (bi, 0, 2 * nq + hi)),
            _resident((t, half)),
            _resident((t, half)),
            pl.BlockSpec((1, c, c), lambda bi, hi: (hi, 0, 0)),
            pl.BlockSpec((1, c, 1), lambda bi, hi: (hi, 0, 0)),
            pl.BlockSpec((1, c, 1), lambda bi, hi: (hi, 0, 0)),
        ],
        out_specs=pl.BlockSpec((1, t, dv), lambda bi, hi: (bi, 0, hi)),
        out_shape=jax.ShapeDtypeStruct((b, t, hh * dv), BF16),
        scratch_shapes=[pltpu.VMEM((dk, dv), F32)],
        compiler_params=_cparams(("parallel", "arbitrary")),
    )(chunk_dec, h3, h3, h3, h3, cos, sin, intra, q_dec, k_dec)


def _k_headnorm(x, gain_row):
    klane = lax.broadcasted_iota(jnp.int32, x.shape, 1) < NSA_DH
    ms = jnp.sum(jnp.where(klane, x * x, 0.0), axis=-1, keepdims=True) * (1.0 / NSA_DH)
    return jnp.where(klane, x * lax.rsqrt(ms + RMS_EPS) * gain_row, x)


def _cmp_kernel(a_ref, pos_ref, w1_ref, w2_ref, kg_ref, o_ref, af_ref, *, t):
    n_blk = t // CMP_STRIDE
    af_ref[0:t, :] = a_ref[0].astype(F32)
    af_ref[t:t + CMP_LEN, :] = jnp.zeros((CMP_LEN, LANES), F32)
    acc = jnp.zeros((n_blk, 2 * CMP_HID), F32)
    for l in range(CMP_LEN):
        rows = af_ref[pl.ds(l, n_blk, stride=CMP_STRIDE), :]
        acc = acc + _dot((rows + pos_ref[l:l + 1, :]).astype(BF16), w1_ref[l])
    hid = jax.nn.gelu(acc, approximate=True)
    out = _dot(hid.astype(BF16), w2_ref[...])
    o_ref[0, 0] = _k_headnorm(out, kg_ref[...]).astype(o_ref.dtype)


def _compress(h3, pos, w1cat, w2cat, kgain_row):
    b, t, _ = h3.shape
    n_blk = t // CMP_STRIDE
    return pl.pallas_call(
        functools.partial(_cmp_kernel, t=t),
        grid=(b, NSA_GROUPS),
        in_specs=[
            pl.BlockSpec((1, t, LANES), lambda bi, gi: (bi, 0, 8 + gi)),
            _resident((CMP_LEN, LANES)),
            _resident((CMP_LEN, LANES, 2 * CMP_HID)),
            _resident((2 * CMP_HID, LANES)),
            _resident((1, LANES)),
        ],
        out_specs=pl.BlockSpec((1, 1, n_blk, LANES), lambda bi, gi: (bi, gi, 0, 0)),
        out_shape=jax.ShapeDtypeStruct((b, NSA_GROUPS, n_blk, LANES), BF16),
        scratch_shapes=[pltpu.VMEM((t + CMP_LEN, LANES), F32)],
        compiler_params=_cparams(("parallel", "parallel")),
    )(h3, pos, w1cat, w2cat, kgain_row)


def _nsa_attn_kernel(q_ref, kvc_ref, kvs_ref, kvw_ref, gate_ref, qg_ref, kg_ref, bnd_ref, bw_ref, bc_ref,
                     ovt_ref, kflag_ref, o_ref, ks_ref, kw_ref, vst_ref, vwt_ref, sel_ref, *, t):
    hpg = NSA_HPG
    dh = NSA_DH
    qt = pl.program_id(2)
    q0 = pl.multiple_of(qt * TQ, TQ)
    lane = lax.broadcasted_iota(jnp.int32, (1, LANES), 1)
    klane = lane < dh
    ones_rows = lax.broadcasted_iota(jnp.int32, (LANES, LANES), 0) < dh

    @pl.when(qt == 0)
    def _():
        pad_rows = jnp.where(lane == PAD_LANE, 1.0, 0.0).astype(BF16)
        ks_ref[0:TQ, :] = jnp.broadcast_to(pad_rows, (TQ, LANES))
        kw_ref[0:WINDOW, :] = jnp.broadcast_to(pad_rows, (WINDOW, LANES))
        pad_col = jnp.where(ones_rows, 1.0, 0.0).astype(BF16)
        for i in range(TQ // LANES):
            vst_ref[:, i * LANES:(i + 1) * LANES] = pad_col
        for i in range(WINDOW // LANES):
            vwt_ref[:, i * LANES:(i + 1) * LANES] = pad_col

        def norm_block(c, carry):
            r0 = pl.multiple_of(c * LANES, LANES)
            for src, dst, dst_t, pad, gi in ((kvs_ref, ks_ref, vst_ref, TQ, 0), (kvw_ref, kw_ref, vwt_ref, WINDOW, 1)):
                x = _k_headnorm(src[0, pl.ds(r0, LANES), :].astype(F32), kg_ref[gi:gi + 1, :])
                flags = kflag_ref[pl.ds(r0, LANES), :] if gi == 0 else jnp.zeros((LANES, LANES), BF16)
                dst[pl.ds(pad + r0, LANES), :] = jnp.where(klane, x.astype(BF16), flags)
                dst_t[:, pl.ds(pad + r0, LANES)] = jnp.where(ones_rows, 1.0, x.T).astype(BF16)
            return carry

        lax.fori_loop(0, t // LANES, norm_block, 0, unroll=4)

    xq = q_ref[0].astype(F32)
    lane_q = lax.broadcasted_iota(jnp.int32, (1, hpg * dh), 1)
    xq2 = xq * xq
    qz = []
    for p in range(hpg):
        seg = (lane_q >= dh * p) & (lane_q < dh * (p + 1))
        ms = jnp.sum(jnp.where(seg, xq2, 0.0), axis=-1, keepdims=True) * (1.0 / dh)
        c0 = LANES * (p // 2)
        half = xq[:, c0:c0 + LANES] * lax.rsqrt(ms + RMS_EPS) * qg_ref[:, c0:c0 + LANES]
        if p % 2 == 1:
            half = pltpu.roll(half, dh, 1)
        qz.append(jnp.where(klane, half, 0.0))

    def stacked_q(flag_lanes):
        return jnp.concatenate([(qp + flag_lanes).astype(BF16) for qp in qz], axis=0)

    ww = WINDOW + TQ
    s_w = _dot_nt(kw_ref[pl.ds(q0, ww), :], stacked_q(jnp.where(lane == PAD_LANE, NEG, 0.0)))
    s = jnp.concatenate([s_w[:, p * TQ:(p + 1) * TQ] + bw_ref[0, p] for p in range(hpg)], axis=1)
    m_win = jnp.max(s, axis=0, keepdims=True)
    acc_win = _dot(vwt_ref[:, pl.ds(q0, ww)], jnp.exp2(s - m_win).astype(BF16))

    per_tile = TQ // CMP_STRIDE
    off = pl.multiple_of((LANES - per_tile) - qt * per_tile, per_tile)
    blk_i = lax.broadcasted_iota(jnp.int32, (LANES, TQ), 0)
    valid_c = (CMP_STRIDE * blk_i + (CMP_LEN - 1)) <= (q0 + lax.broadcasted_iota(jnp.int32, (LANES, TQ), 1))
    kvc = kvc_ref[0, 0]
    s_c = _dot_nt(kvc, stacked_q(0.0))
    pcs = []
    for p in range(hpg):
        s = jnp.where(valid_c, s_c[:, p * TQ:(p + 1) * TQ] + bc_ref[0, p, pl.ds(off, LANES), :], NEG)
        m = jnp.max(s, axis=0, keepdims=True)
        e = jnp.where(valid_c, jnp.exp2(s - m), 0.0)
        l = jnp.sum(e, axis=0, keepdims=True)
        pcs.append(e / jnp.where(l > 0.0, l, 1.0))
    vc_t = jnp.where(ones_rows, 1.0, kvc.astype(F32).T).astype(BF16)
    o_cmp = _dot(vc_t, jnp.concatenate(pcs, axis=1).astype(BF16))
    psum = pcs[0]
    for p in range(1, hpg):
        psum = psum + pcs[p]
    p_hi = psum.astype(BF16)
    p_lo = (psum - p_hi.astype(F32)).astype(BF16)
    n_slc = t // SLC_LEN
    imp_t = (_dot(ovt_ref[...], p_hi) + _dot(ovt_ref[...], p_lo))[:n_slc, :]

    jrow = lax.broadcasted_iota(jnp.int32, (n_slc, TQ), 0)
    cur = (q0 + lax.broadcasted_iota(jnp.int32, (n_slc, TQ), 1)) // SLC_LEN
    forced = (jrow == 0) | (jrow == cur) | (jrow == cur - 1)
    vals = jnp.where(forced, SEL_BIG, jnp.where(jrow <= cur, imp_t, -SEL_BIG))
    terms = []
    for jp in range(n_slc):
        row = vals[jp:jp + 1, :]
        ahead = (row > vals) | ((row == vals) & (jrow > jp))
        terms.append(jnp.where(ahead, 1.0, 0.0))
    while len(terms) > 1:
        terms = [a + b for a, b in zip(terms[0::2], terms[1::2])]
    cnt = terms[0]

    far_end = (qt - 1) * TQ
    drop_nd = jnp.where(cnt < float(N_SEL), 0.0, NEG)
    drop_far = jnp.where(jrow * SLC_LEN < far_end, drop_nd, NEG)

    def flag_lanes(drop):
        rows = [jnp.zeros((FLAG0, TQ), F32), drop, jnp.full((PAD_LANE - FLAG0 - n_slc + 8, TQ), NEG, F32),
                jnp.zeros((LANES - PAD_LANE - 8, TQ), F32)]
        return jnp.concatenate(rows, axis=0).T

    ck = CK
    q_nd = stacked_q(flag_lanes(drop_nd))
    q_far = stacked_q(flag_lanes(drop_far))
    n_far = jnp.maximum(far_end + ck - 1, 0) // ck

    def select_branch(k):
        s_nd = _dot_nt(ks_ref[pl.ds(q0, 2 * TQ), :], q_nd)
        scores = [jnp.concatenate([s_nd[:, p * TQ:(p + 1) * TQ] + bnd_ref[0, p] for p in range(hpg)], axis=1)]
        values = [vst_ref[:, pl.ds(q0, 2 * TQ)]]
        for c in range(k):
            scores.append(_dot_nt(ks_ref[TQ + c * ck:TQ + (c + 1) * ck, :], q_far))
            values.append(vst_ref[:, TQ + c * ck:TQ + (c + 1) * ck])
        m = jnp.max(scores[0], axis=0, keepdims=True)
        for s in scores[1:]:
            m = jnp.maximum(m, jnp.max(s, axis=0, keepdims=True))
        acc = _dot(values[0], jnp.exp2(scores[0] - m).astype(BF16))
        for s, v in zip(scores[1:], values[1:]):
            acc = acc + _dot(v, jnp.exp2(s - m).astype(BF16))
        sel_ref[...] = acc

    for k in range((t - 2 * TQ) // ck + 1):
        pl.when(n_far == k)(functools.partial(select_branch, k))
    acc_sel = sel_ref[...]

    gl_t = jax.nn.sigmoid(gate_ref[0].astype(F32)).T
    outs = []
    for p in range(hpg):
        sl = slice(p * TQ, (p + 1) * TQ)
        g_c = gl_t[p:p + 1, :]
        g_s = gl_t[hpg + p:hpg + p + 1, :] / acc_sel[0:1, sl]
        g_w = gl_t[2 * hpg + p:2 * hpg + p + 1, :] / acc_win[0:1, sl]
        outs.append(g_c * o_cmp[dh:, sl] + g_s * acc_sel[dh:, sl] + g_w * acc_win[dh:, sl])
    o_ref[0] = jnp.concatenate(outs, axis=0).T.astype(o_ref.dtype)


def _nsa_attention(h3, kvc, qgain, kgain, bias_nd, bias_w, bias_cmp, overlap_t, kflag):
    b, t, _ = h3.shape
    g, hpg = NSA_GROUPS, NSA_HPG
    n_blk = t // CMP_STRIDE
    qw = hpg * NSA_DH
    return pl.pallas_call(
        functools.partial(_nsa_attn_kernel, t=t),
        grid=(b, g, t // TQ),
        in_specs=[
            pl.BlockSpec((1, TQ, qw), lambda bi, gi, qi: (bi, qi, gi)),
            pl.BlockSpec((1, 1, n_blk, LANES), lambda bi, gi, qi: (bi, gi, 0, 0)),
            pl.BlockSpec((1, t, LANES), lambda bi, gi, qi: (bi, 0, 12 + gi)),
            pl.BlockSpec((1, t, LANES), lambda bi, gi, qi: (bi, 0, 16 + gi)),
            pl.BlockSpec((1, TQ, LANES), lambda bi, gi, qi: (bi, qi, 20 + gi)),
            pl.BlockSpec((1, qw), lambda bi, gi, qi: (0, 0)),
            pl.BlockSpec((2, LANES), lambda bi, gi, qi: (0, 0)),
            pl.BlockSpec((1, hpg, 2 * TQ, TQ), lambda bi, gi, qi: (gi, 0, 0, 0)),
            pl.BlockSpec((1, hpg, WINDOW + TQ, TQ), lambda bi, gi, qi: (gi, 0, 0, 0)),
            pl.BlockSpec((1, hpg, 2 * LANES, TQ), lambda bi, gi, qi: (gi, 0, 0, 0)),
            pl.BlockSpec((LANES, LANES), lambda bi, gi, qi: (0, 0)),
            pl.BlockSpec((t, LANES), lambda bi, gi, qi: (0, 0)),
        ],
        out_specs=pl.BlockSpec((1, TQ, qw), lambda bi, gi, qi: (bi, qi, gi)),
        out_shape=jax.ShapeDtypeStruct((b, t, NSA_HEADS * NSA_DH), BF16),
        scratch_shapes=[pltpu.VMEM((TQ + t, LANES), BF16), pltpu.VMEM((WINDOW + t, LANES), BF16),
                        pltpu.VMEM((LANES, TQ + t), BF16), pltpu.VMEM((LANES, WINDOW + t), BF16),
                        pltpu.VMEM((LANES, hpg * TQ), F32)],
        compiler_params=_cparams(("arbitrary", "arbitrary", "arbitrary")),
    )(h3, kvc, h3, h3, h3, qgain, kgain, bias_nd, bias_w, bias_cmp, overlap_t, kflag)


def _bias_kernel(tbl_ref, nd_ref, w_ref, bc_ref, ond_ref, ow_ref, obc_ref):
    h = pl.program_id(0)
    far = tbl_ref[NUM_BUCKETS - 1, h]
    nd = nd_ref[...]
    w = w_ref[...]
    bc = bc_ref[...]
    o_nd = jnp.full(nd.shape, NEG, F32)
    o_w = jnp.full(w.shape, NEG, F32)
    o_bc = jnp.zeros(bc.shape, F32)
    for bucket in range(NUM_BUCKETS):
        v = tbl_ref[bucket, h]
        o_nd = jnp.where(nd == bucket, (v - far) * LOG2E, o_nd)
        o_w = jnp.where(w == bucket, (v - far) * LOG2E, o_w)
        o_bc = jnp.where(bc == bucket, v * LOG2E, o_bc)
    ond_ref[0] = o_nd
    ow_ref[0] = o_w
    obc_ref[0] = o_bc


def _bias_tiles(rel_bias, nd_bucket, w_bucket, bc_bucket):
    n_heads = rel_bias.shape[1]
    tables = (nd_bucket, w_bucket, bc_bucket)
    return pl.pallas_call(
        _bias_kernel,
        grid=(n_heads,),
        in_specs=[pl.BlockSpec(memory_space=pltpu.SMEM)] + [_resident(a.shape) for a in tables],
        out_specs=[pl.BlockSpec((1,) + a.shape, lambda h: (h, 0, 0)) for a in tables],
        out_shape=[jax.ShapeDtypeStruct((n_heads,) + a.shape, F32) for a in tables],
        compiler_params=_cparams(("parallel",)),
    )(rel_bias.astype(F32), *[jnp.asarray(a, jnp.int32) for a in tables])


def _t5_bucket_np(dist):
    n = np.maximum(dist, 0)
    max_exact = NUM_BUCKETS // 2
    nf = np.maximum(n, max_exact).astype(np.float32)
    large = max_exact + (np.log(nf / max_exact) / np.float32(np.log(MAX_DISTANCE / max_exact))
                         * (NUM_BUCKETS - max_exact)).astype(np.int32)
    large = np.minimum(large, NUM_BUCKETS - 1)
    return np.where(n < max_exact, n, large)


def _nsa_constants(t):
    n_cmp = (t - CMP_LEN) // CMP_STRIDE + 1
    n_slc = t // SLC_LEN
    n_blk = t // CMP_STRIDE
    per_tile = TQ // CMP_STRIDE
    assert n_blk <= LANES and per_tile * (t // TQ - 1) <= LANES - per_tile
    iq = np.arange(TQ)
    rel = iq[None, :] - iq[:, None]
    assert _t5_bucket_np(np.array([TQ + 1]))[0] == NUM_BUCKETS - 1
    dist_nd = np.concatenate([TQ + rel, rel], axis=0)
    nd_bucket = np.where(dist_nd >= 0, _t5_bucket_np(dist_nd), -1)
    dist_w = np.arange(TQ)[None, :] + WINDOW - np.arange(WINDOW + TQ)[:, None]
    w_bucket = np.where((dist_w >= 0) & (dist_w < WINDOW), _t5_bucket_np(dist_w), -1)
    blk_off = np.arange(LANES) - (LANES - per_tile)
    bc_bucket = _t5_bucket_np(iq[None, :] - CMP_STRIDE * blk_off[:, None] - (CMP_LEN - 1))
    bc_bucket = np.concatenate([bc_bucket, bc_bucket], axis=0)
    ci = np.arange(n_blk) * CMP_STRIDE
    sj = np.arange(n_slc) * SLC_LEN
    ov = ((ci[:, None] < sj[None, :] + SLC_LEN) & (ci[:, None] + CMP_LEN > sj[None, :])).astype(np.float32)
    ov[n_cmp:, :] = 0.0
    overlap_t = np.zeros((LANES, LANES), np.float32)
    overlap_t[:n_slc, :n_blk] = ov.T
    assert FLAG0 + n_slc <= PAD_LANE
    kflag = np.zeros((t, LANES), np.float32)
    kflag[np.arange(t), FLAG0 + np.arange(t) // SLC_LEN] = 1.0
    return nd_bucket, w_bucket, bc_bucket, overlap_t, kflag


def _nsa_w_in_columns():
    kv0 = NSA_HEADS * NSA_DH
    kvw = NSA_GROUPS * NSA_DH
    cols = list(range(kv0))
    for br in range(3):
        for g in range(NSA_GROUPS):
            k_src = kv0 + (2 * br) * kvw + g * NSA_DH
            v_src = kv0 + (2 * br + 1) * kvw + g * NSA_DH
            cols += list(range(k_src, k_src + NSA_DH)) + list(range(v_src, v_src + NSA_DH))
    gate0 = kv0 + 6 * kvw
    for g in range(NSA_GROUPS):
        blk = [-1] * LANES
        for br in range(3):
            for p in range(NSA_HPG):
                blk[br * NSA_HPG + p] = gate0 + br * NSA_HEADS + g * NSA_HPG + p
        cols += blk
    return np.asarray(cols, np.int32)


def _retention_layer(x2, b, t, norm_g, w_in, w_out):
    h = _norm_matmul(x2, norm_g, w_in.astype(BF16), tn=1024)
    o = _retention_core(h.reshape(b, t, -1))
    return _matmul_res(o.reshape(b * t, -1), w_out.astype(BF16), x2)


def _nsa_layer(x2, b, t, norm_g, w_in, w_out, q_gain, k_gain, cmp_pos, cmp_w1, cmp_w2, bias_nd, bias_w,
               bias_cmp, overlap_t, kflag):
    cols = _nsa_w_in_columns()
    w_k = jnp.where(cols[None, :] >= 0, jnp.take(w_in, np.maximum(cols, 0), axis=1), 0.0).astype(BF16)
    h3 = _norm_matmul(x2, norm_g, w_k, tn=1024).reshape(b, t, NSA_COLS)

    dh = NSA_DH
    w1 = cmp_w1.reshape(2, CMP_LEN, dh, CMP_HID)
    zero1 = jnp.zeros((CMP_LEN, dh, CMP_HID), F32)
    w1cat = jnp.concatenate([jnp.concatenate([w1[0], zero1], axis=-1),
                             jnp.concatenate([zero1, w1[1]], axis=-1)], axis=1).astype(BF16)
    zero2 = jnp.zeros((CMP_HID, dh), F32)
    w2cat = jnp.concatenate([jnp.concatenate([cmp_w2[0], zero2], axis=-1),
                             jnp.concatenate([zero2, cmp_w2[1]], axis=-1)], axis=0).astype(BF16)
    pos = jnp.concatenate([cmp_pos[0], cmp_pos[1]], axis=-1)
    ones = jnp.ones((dh,), F32)
    kvc = _compress(h3, pos, w1cat, w2cat, jnp.concatenate([k_gain[0], ones])[None, :])

    qgain = jnp.tile(q_gain * (dh ** -0.5 * LOG2E), NSA_HPG)[None, :]
    kgain = jnp.stack([jnp.concatenate([k_gain[1], ones]), jnp.concatenate([k_gain[2], ones])])
    o = _nsa_attention(h3, kvc, qgain, kgain, bias_nd, bias_w, bias_cmp, overlap_t, kflag)
    return _matmul_res(o.reshape(b * t, -1), w_out.astype(BF16), x2)


def kernel(x, ffn1_norm, ffn1_w_gu, ffn1_w_down, mix_norm, ffn2_norm, ffn2_w_gu, ffn2_w_down,
           ret_w_in, ret_w_out, nsa_w_in, nsa_w_out, nsa_q_gain, nsa_k_gain,
           nsa_cmp_pos, nsa_cmp_w1, nsa_cmp_w2, rel_bias):
    b, t, d = x.shape
    depth = ffn1_norm.shape[0]
    d_ff = ffn1_w_down.shape[1]
    tf = d_ff
    nf = d_ff // tf

    def ffn_weights(w_gu, w_down):
        wgu = w_gu.astype(BF16).reshape(d, 2, nf, tf).transpose(1, 2, 0, 3)
        return wgu, w_down.astype(BF16).reshape(nf, tf, d)

    nd_bucket, w_bucket, bc_bucket, overlap_t, kflag = _nsa_constants(t)
    bias_nd, bias_w, bias_cmp = _bias_tiles(rel_bias, nd_bucket, w_bucket, bc_bucket)
    bias_nd = bias_nd.reshape(NSA_GROUPS, NSA_HPG, 2 * TQ, TQ)
    bias_w = bias_w.reshape(NSA_GROUPS, NSA_HPG, WINDOW + TQ, TQ)
    bias_cmp = bias_cmp.reshape(NSA_GROUPS, NSA_HPG, 2 * LANES, TQ)
    overlap_t = jnp.asarray(overlap_t, BF16)
    kflag = jnp.asarray(kflag, BF16)

    x2 = x.reshape(b * t, d)
    for layer in range(depth):
        x2 = _ffn(x2, ffn1_norm[layer][None, :], *ffn_weights(ffn1_w_gu[layer], ffn1_w_down[layer]))
        j = layer // 2
        if layer % 2 == 0:
            x2 = _retention_layer(x2, b, t, mix_norm[layer][None, :], ret_w_in[j], ret_w_out[j])
        else:
            x2 = _nsa_layer(x2, b, t, mix_norm[layer][None, :], nsa_w_in[j], nsa_w_out[j], nsa_q_gain[j],
                            nsa_k_gain[j], nsa_cmp_pos[j], nsa_cmp_w1[j], nsa_cmp_w2[j], bias_nd, bias_w,
                            bias_cmp, overlap_t, kflag)
        x2 = _ffn(x2, ffn2_norm[layer][None, :], *ffn_weights(ffn2_w_gu[layer], ffn2_w_down[layer]))
    return x2.reshape(b, t, d)
```

```python
import functools

import numpy as np
import jax
import jax.numpy as jnp
from jax import lax
from jax.experimental import pallas as pl
from jax.experimental.pallas import tpu as pltpu

F32 = jnp.float32
BF16 = jnp.bfloat16

RMS_EPS = 1e-6
NEG = -1e30
SEL_BIG = 1e9
LOG2E = 1.4426950408889634

LANES = 128
VMEM_LIMIT = 56 * 1024 * 1024

RET_HEADS = 4
RET_DK = 256
RET_DV = 512
RET_CHUNK = 128
RET_ROPE_BASE = 10000.0

NSA_HEADS = 16
NSA_GROUPS = 4
NSA_HPG = 4
NSA_DH = 64
CMP_LEN = 32
CMP_STRIDE = 16
CMP_HID = 256
SLC_LEN = 64
N_SEL = 8
WINDOW = 512
NUM_BUCKETS = 32
MAX_DISTANCE = 128
TQ = 256
CK = 512
FLAG0 = 64
PAD_LANE = 96
NSA_COLS = 1024 + 3 * 4 * 128 + 4 * 128


def _dot(a, b):
    return jnp.dot(a, b, preferred_element_type=F32)


def _dot_nt(a, b):
    return lax.dot_general(a, b, (((1,), (1,)), ((), ())), preferred_element_type=F32)


def _dot_tn(a, b):
    return lax.dot_general(a, b, (((0,), (0,)), ((), ())), preferred_element_type=F32)


def _rms(x, g):
    ms = jnp.mean(x * x, axis=-1, keepdims=True)
    return x * lax.rsqrt(ms + RMS_EPS) * g


def _cparams(sem):
    return pltpu.CompilerParams(dimension_semantics=sem, vmem_limit_bytes=VMEM_LIMIT)


def _resident(shape):
    n = len(shape)
    return pl.BlockSpec(shape, lambda *_: (0,) * n, pipeline_mode=pl.Buffered(1))


def _ffn_kernel(x_ref, g_ref, wgu_ref, wd_ref, o_ref):
    d_ff = wd_ref.shape[0]
    x = x_ref[...]
    xn = _rms(x, g_ref[...]).astype(BF16)
    a = _dot(xn, wgu_ref[:, :d_ff])
    b = _dot(xn, wgu_ref[:, d_ff:])
    h = (a * jax.nn.sigmoid(a) * b).astype(BF16)
    o_ref[...] = x + 0.5 * _dot(h, wd_ref[...])


def _ffn(x2, g, wgu, wd, tm=512):
    n, d = x2.shape
    return pl.pallas_call(
        _ffn_kernel,
        grid=(n // tm,),
        in_specs=[
            pl.BlockSpec((tm, d), lambda i: (i, 0)),
            _resident((1, d)),
            _resident(wgu.shape),
            _resident(wd.shape),
        ],
        out_specs=pl.BlockSpec((tm, d), lambda i: (i, 0)),
        out_shape=jax.ShapeDtypeStruct((n, d), F32),
        compiler_params=_cparams(("parallel",)),
    )(x2, g, wgu, wd)


def _norm_matmul_kernel(x_ref, g_ref, w_ref, o_ref, *, tn):
    xn = _rms(x_ref[...], g_ref[...]).astype(BF16)
    for j in range(w_ref.shape[1] // tn):
        o_ref[:, j * tn:(j + 1) * tn] = _dot(xn, w_ref[:, j * tn:(j + 1) * tn]).astype(o_ref.dtype)


def _norm_matmul(x2, g, w, tn, tm=512):
    n, d = x2.shape
    nout = w.shape[1]
    return pl.pallas_call(
        functools.partial(_norm_matmul_kernel, tn=tn),
        grid=(n // tm,),
        in_specs=[pl.BlockSpec((tm, d), lambda i: (i, 0)), _resident((1, d)), _resident((d, nout))],
        out_specs=pl.BlockSpec((tm, nout), lambda i: (i, 0)),
        out_shape=jax.ShapeDtypeStruct((n, nout), BF16),
        compiler_params=_cparams(("parallel",)),
    )(x2, g, w)


def _matmul_res_kernel(o_ref, w_ref, x_ref, y_ref):
    y_ref[...] = x_ref[...] + _dot(o_ref[...], w_ref[...])


def _matmul_res(o2, w, x2, tm=512):
    n, k = o2.shape
    d = w.shape[1]
    return pl.pallas_call(
        _matmul_res_kernel,
        grid=(n // tm,),
        in_specs=[pl.BlockSpec((tm, k), lambda i: (i, 0)), _resident((k, d)),
                  pl.BlockSpec((tm, d), lambda i: (i, 0))],
        out_specs=pl.BlockSpec((tm, d), lambda i: (i, 0)),
        out_shape=jax.ShapeDtypeStruct((n, d), F32),
        compiler_params=_cparams(("parallel",)),
    )(o2, w, x2)


def _ret_kernel(cd_ref, q_ref, k_ref, v_ref, g_ref, cos_ref, sin_ref, intra_ref, qd_ref, kd_ref,
                o_ref, s_ref, *, n_chunks):
    c_len = RET_CHUNK
    half = RET_DK // 2
    cd = cd_ref[pl.program_id(1)]
    intra = intra_ref[0]
    qd = qd_ref[0]
    kd = kd_ref[0]
    s_ref[...] = jnp.zeros_like(s_ref)

    def chunk(c, carry):
        r0 = pl.multiple_of(c * c_len, c_len)
        cos = cos_ref[pl.ds(r0, c_len), :]
        sin = sin_ref[pl.ds(r0, c_len), :]

        def rot(x):
            x1, x2 = x[:, :half], x[:, half:]
            return jnp.concatenate([x1 * cos - x2 * sin, x1 * sin + x2 * cos], axis=-1)

        qr = rot(q_ref[0, pl.ds(r0, c_len), :].astype(F32))
        kr = rot(k_ref[0, pl.ds(r0, c_len), :].astype(F32)) * (RET_DK ** -0.5)
        vc = v_ref[0, pl.ds(r0, c_len), :]
        s = _dot_nt(qr.astype(BF16), kr.astype(BF16)) * intra
        state = s_ref[...]
        o = _dot(s.astype(BF16), vc) + _dot((qr * qd).astype(BF16), state.astype(BF16))
        s_ref[...] = cd * state + _dot_tn((kr * kd).astype(BF16), vc)
        o = o * lax.rsqrt(jnp.mean(o * o, axis=-1, keepdims=True) + RMS_EPS)
        gate = g_ref[0, pl.ds(r0, c_len), :].astype(F32)
        o_ref[0, pl.ds(r0, c_len), :] = (gate * jax.nn.sigmoid(gate) * o).astype(o_ref.dtype)
        return carry

    lax.fori_loop(0, n_chunks, chunk, 0, unroll=4)


def _retention_core(h3):
    b, t, _ = h3.shape
    hh, dk, dv, c = RET_HEADS, RET_DK, RET_DV, RET_CHUNK
    half = dk // 2
    pos = jnp.arange(t, dtype=F32)
    inv_freq = RET_ROPE_BASE ** (-jnp.arange(half, dtype=F32) / half)
    ang = pos[:, None] * inv_freq[None, :]
    cos, sin = jnp.cos(ang), jnp.sin(ang)
    log_gamma = jnp.log(1.0 - 2.0 ** (-5.0 - jnp.arange(hh, dtype=F32)))
    i = jnp.arange(c)
    diff = i[:, None] - i[None, :]
    intra = jnp.where(diff >= 0, jnp.exp(log_gamma[:, None, None] * jnp.maximum(diff, 0)), 0.0)
    q_dec = jnp.exp(log_gamma[:, None] * (i + 1))[:, :, None]
    k_dec = jnp.exp(log_gamma[:, None] * (c - 1 - i))[:, :, None]
    chunk_dec = jnp.exp(log_gamma * c)
    nq = hh * dk // dk
    return pl.pallas_call(
        functools.partial(_ret_kernel, n_chunks=t // c),
        grid=(b, hh),
        in_specs=[
            pl.BlockSpec(memory_space=pltpu.SMEM),
            pl.BlockSpec((1, t, dk), lambda bi, hi: (bi, 0, hi)),
            pl.BlockSpec((1, t, dk), lambda bi, hi: (bi, 0, nq + hi)),
            pl.BlockSpec((1, t, dv), lambda bi, hi: (bi, 0, nq + hi)),
            pl.BlockSpec((1, t, dv), lambda bi, hi: (bi, 0, 2 * nq + hi)),
            _resident((t, half)),
            _resident((t, half)),
            pl.BlockSpec((1, c, c), lambda bi, hi: (hi, 0, 0)),
            pl.BlockSpec((1, c, 1), lambda bi, hi: (hi, 0, 0)),
            pl.BlockSpec((1, c, 1), lambda bi, hi: (hi, 0, 0)),
        ],
        out_specs=pl.BlockSpec((1, t, dv), lambda bi, hi: (bi, 0, hi)),
        out_shape=jax.ShapeDtypeStruct((b, t, hh * dv), BF16),
        scratch_shapes=[pltpu.VMEM((dk, dv), F32)],
        compiler_params=_cparams(("parallel", "arbitrary")),
    )(chunk_dec, h3, h3, h3, h3, cos, sin, intra, q_dec, k_dec)


def _k_headnorm(x, gain_row):
    klane = lax.broadcasted_iota(jnp.int32, x.shape, 1) < NSA_DH
    ms = jnp.sum(jnp.where(klane, x * x, 0.0), axis=-1, keepdims=True) * (1.0 / NSA_DH)
    return jnp.where(klane, x * lax.rsqrt(ms + RMS_EPS) * gain_row, x)


def _cmp_kernel(a_ref, pos_ref, w1_ref, w2_ref, kg_ref, o_ref, af_ref, *, t):
    n_blk = t // CMP_STRIDE
    af_ref[0:t, :] = a_ref[0].astype(F32)
    af_ref[t:t + CMP_LEN, :] = jnp.zeros((CMP_LEN, LANES), F32)
    acc = jnp.zeros((n_blk, 2 * CMP_HID), F32)
    for l in range(CMP_LEN):
        rows = af_ref[pl.ds(l, n_blk, stride=CMP_STRIDE), :]
        acc = acc + _dot((rows + pos_ref[l:l + 1, :]).astype(BF16), w1_ref[l])
    hid = jax.nn.gelu(acc, approximate=True)
    out = _dot(hid.astype(BF16), w2_ref[...])
    o_ref[0, 0] = _k_headnorm(out, kg_ref[...]).astype(o_ref.dtype)


def _compress(h3, pos, w1cat, w2cat, kgain_row):
    b, t, _ = h3.shape
    n_blk = t // CMP_STRIDE
    return pl.pallas_call(
        functools.partial(_cmp_kernel, t=t),
        grid=(b, NSA_GROUPS),
        in_specs=[
            pl.BlockSpec((1, t, LANES), lambda bi, gi: (bi, 0, 8 + gi)),
            _resident((CMP_LEN, LANES)),
            _resident((CMP_LEN, LANES, 2 * CMP_HID)),
            _resident((2 * CMP_HID, LANES)),
            _resident((1, LANES)),
        ],
        out_specs=pl.BlockSpec((1, 1, n_blk, LANES), lambda bi, gi: (bi, gi, 0, 0)),
        out_shape=jax.ShapeDtypeStruct((b, NSA_GROUPS, n_blk, LANES), BF16),
        scratch_shapes=[pltpu.VMEM((t + CMP_LEN, LANES), F32)],
        compiler_params=_cparams(("parallel", "parallel")),
    )(h3, pos, w1cat, w2cat, kgain_row)


def _nsa_attn_kernel(q_ref, kvc_ref, kvs_ref, kvw_ref, gate_ref, qg_ref, kg_ref, bnd_ref, bw_ref, bc_ref,
                     ovt_ref, kflag_ref, o_ref, ks_ref, kw_ref, vst_ref, vwt_ref, sel_ref, *, t):
    hpg = NSA_HPG
    dh = NSA_DH
    qt = pl.program_id(2)
    q0 = pl.multiple_of(qt * TQ, TQ)
    lane = lax.broadcasted_iota(jnp.int32, (1, LANES), 1)
    klane = lane < dh
    ones_rows = lax.broadcasted_iota(jnp.int32, (LANES, LANES), 0) < dh

    @pl.when(qt == 0)
    def _():
        pad_rows = jnp.where(lane == PAD_LANE, 1.0, 0.0).astype(BF16)
        ks_ref[0:TQ, :] = jnp.broadcast_to(pad_rows, (TQ, LANES))
        kw_ref[0:WINDOW, :] = jnp.broadcast_to(pad_rows, (WINDOW, LANES))
        pad_col = jnp.where(ones_rows, 1.0, 0.0).astype(BF16)
        for i in range(TQ // LANES):
            vst_ref[:, i * LANES:(i + 1) * LANES] = pad_col
        for i in range(WINDOW // LANES):
            vwt_ref[:, i * LANES:(i + 1) * LANES] = pad_col

        def norm_block(c, carry):
            r0 = pl.multiple_of(c * LANES, LANES)
            for src, dst, dst_t, pad, gi in ((kvs_ref, ks_ref, vst_ref, TQ, 0), (kvw_ref, kw_ref, vwt_ref, WINDOW, 1)):
                x = _k_headnorm(src[0, pl.ds(r0, LANES), :].astype(F32), kg_ref[gi:gi + 1, :])
                flags = kflag_ref[pl.ds(r0, LANES), :] if gi == 0 else jnp.zeros((LANES, LANES), BF16)
                dst[pl.ds(pad + r0, LANES), :] = jnp.where(klane, x.astype(BF16), flags)
                dst_t[:, pl.ds(pad + r0, LANES)] = jnp.where(ones_rows, 1.0, x.T).astype(BF16)
            return carry

        lax.fori_loop(0, t // LANES, norm_block, 0, unroll=4)

    xq = q_ref[0].astype(F32)
    lane_q = lax.broadcasted_iota(jnp.int32, (1, hpg * dh), 1)
    xq2 = xq * xq
    qz = []
    for p in range(hpg):
        seg = (lane_q >= dh * p) & (lane_q < dh * (p + 1))
        ms = jnp.sum(jnp.where(seg, xq2, 0.0), axis=-1, keepdims=True) * (1.0 / dh)
        c0 = LANES * (p // 2)
        half = xq[:, c0:c0 + LANES] * lax.rsqrt(ms + RMS_EPS) * qg_ref[:, c0:c0 + LANES]
        if p % 2 == 1:
            half = pltpu.roll(half, dh, 1)
        qz.append(jnp.where(klane, half, 0.0))

    def stacked_q(flag_lanes):
        return jnp.concatenate([(qp + flag_lanes).astype(BF16) for qp in qz], axis=0)

    ww = WINDOW + TQ
    s_w = _dot_nt(kw_ref[pl.ds(q0, ww), :], stacked_q(jnp.where(lane == PAD_LANE, NEG, 0.0)))
    s = jnp.concatenate([s_w[:, p * TQ:(p + 1) * TQ] + bw_ref[0, p] for p in range(hpg)], axis=1)
    m_win = jnp.max(s, axis=0, keepdims=True)
    acc_win = _dot(vwt_ref[:, pl.ds(q0, ww)], jnp.exp2(s - m_win).astype(BF16))

    per_tile = TQ // CMP_STRIDE
    off = pl.multiple_of((LANES - per_tile) - qt * per_tile, per_tile)
    blk_i = lax.broadcasted_iota(jnp.int32, (LANES, TQ), 0)
    valid_c = (CMP_STRIDE * blk_i + (CMP_LEN - 1)) <= (q0 + lax.broadcasted_iota(jnp.int32, (LANES, TQ), 1))
    kvc = kvc_ref[0, 0]
    s_c = _dot_nt(kvc, stacked_q(0.0))
    pcs = []
    for p in range(hpg):
        s = jnp.where(valid_c, s_c[:, p * TQ:(p + 1) * TQ] + bc_ref[0, p, pl.ds(off, LANES), :], NEG)
        m = jnp.max(s, axis=0, keepdims=True)
        e = jnp.where(valid_c, jnp.exp2(s - m), 0.0)
        l = jnp.sum(e, axis=0, keepdims=True)
        pcs.append(e / jnp.where(l > 0.0, l, 1.0))
    vc_t = jnp.where(ones_rows, 1.0, kvc.astype(F32).T).astype(BF16)
    o_cmp = _dot(vc_t, jnp.concatenate(pcs, axis=1).astype(BF16))
    psum = pcs[0]
    for p in range(1, hpg):
        psum = psum + pcs[p]
    p_hi = psum.astype(BF16)
    p_lo = (psum - p_hi.astype(F32)).astype(BF16)
    n_slc = t // SLC_LEN
    imp_t = (_dot(ovt_ref[...], p_hi) + _dot(ovt_ref[...], p_lo))[:n_slc, :]

    jrow = lax.broadcasted_iota(jnp.int32, (n_slc, TQ), 0)
    cur = (q0 + lax.broadcasted_iota(jnp.int32, (n_slc, TQ), 1)) // SLC_LEN
    forced = (jrow == 0) | (jrow == cur) | (jrow == cur - 1)
    vals = jnp.where(forced, SEL_BIG, jnp.where(jrow <= cur, imp_t, -SEL_BIG))
    terms = []
    for jp in range(n_slc):
        row = vals[jp:jp + 1, :]
        ahead = (row > vals) | ((row == vals) & (jrow > jp))
        terms.append(jnp.where(ahead, 1.0, 0.0))
    while len(terms) > 1:
        terms = [a + b for a, b in zip(terms[0::2], terms[1::2])]
    cnt = terms[0]

    far_end = (qt - 1) * TQ
    drop_nd = jnp.where(cnt < float(N_SEL), 0.0, NEG)
    drop_far = jnp.where(jrow * SLC_LEN < far_end, drop_nd, NEG)

    def flag_lanes(drop):
        rows = [jnp.zeros((FLAG0, TQ), F32), drop, jnp.full((PAD_LANE - FLAG0 - n_slc + 8, TQ), NEG, F32),
                jnp.zeros((LANES - PAD_LANE - 8, TQ), F32)]
        return jnp.concatenate(rows, axis=0).T

    ck = CK
    q_nd = stacked_q(flag_lanes(drop_nd))
    q_far = stacked_q(flag_lanes(drop_far))
    n_far = jnp.maximum(far_end + ck - 1, 0) // ck

    def select_branch(k):
        s_nd = _dot_nt(ks_ref[pl.ds(q0, 2 * TQ), :], q_nd)
        scores = [jnp.concatenate([s_nd[:, p * TQ:(p + 1) * TQ] + bnd_ref[0, p] for p in range(hpg)], axis=1)]
        values = [vst_ref[:, pl.ds(q0, 2 * TQ)]]
        for c in range(k):
            scores.append(_dot_nt(ks_ref[TQ + c * ck:TQ + (c + 1) * ck, :], q_far))
            values.append(vst_ref[:, TQ + c * ck:TQ + (c + 1) * ck])
        m = jnp.max(scores[0], axis=0, keepdims=True)
        for s in scores[1:]:
            m = jnp.maximum(m, jnp.max(s, axis=0, keepdims=True))
        acc = _dot(values[0], jnp.exp2(scores[0] - m).astype(BF16))
        for s, v in zip(scores[1:], values[1:]):
            acc = acc + _dot(v, jnp.exp2(s - m).astype(BF16))
        sel_ref[...] = acc

    for k in range((t - 2 * TQ) // ck + 1):
        pl.when(n_far == k)(functools.partial(select_branch, k))
    acc_sel = sel_ref[...]

    gl_t = jax.nn.sigmoid(gate_ref[0].astype(F32)).T
    outs = []
    for p in range(hpg):
        sl = slice(p * TQ, (p + 1) * TQ)
        g_c = gl_t[p:p + 1, :]
        g_s = gl_t[hpg + p:hpg + p + 1, :] / acc_sel[0:1, sl]
        g_w = gl_t[2 * hpg + p:2 * hpg + p + 1, :] / acc_win[0:1, sl]
        outs.append(g_c * o_cmp[dh:, sl] + g_s * acc_sel[dh:, sl] + g_w * acc_win[dh:, sl])
    o_ref[0] = jnp.concatenate(outs, axis=0).T.astype(o_ref.dtype)


def _nsa_attention(h3, kvc, qgain, kgain, bias_nd, bias_w, bias_cmp, overlap_t, kflag):
    b, t, _ = h3.shape
    g, hpg = NSA_GROUPS, NSA_HPG
    n_blk = t // CMP_STRIDE
    qw = hpg * NSA_DH
    return pl.pallas_call(
        functools.partial(_nsa_attn_kernel, t=t),
        grid=(b, g, t // TQ),
        in_specs=[
            pl.BlockSpec((1, TQ, qw), lambda bi, gi, qi: (bi, qi, gi)),
            pl.BlockSpec((1, 1, n_blk, LANES), lambda bi, gi, qi: (bi, gi, 0, 0)),
            pl.BlockSpec((1, t, LANES), lambda bi, gi, qi: (bi, 0, 12 + gi)),
            pl.BlockSpec((1, t, LANES), lambda bi, gi, qi: (bi, 0, 16 + gi)),
            pl.BlockSpec((1, TQ, LANES), lambda bi, gi, qi: (bi, qi, 20 + gi)),
            pl.BlockSpec((1, qw), lambda bi, gi, qi: (0, 0)),
            pl.BlockSpec((2, LANES), lambda bi, gi, qi: (0, 0)),
            pl.BlockSpec((1, hpg, 2 * TQ, TQ), lambda bi, gi, qi: (gi, 0, 0, 0)),
            pl.BlockSpec((1, hpg, WINDOW + TQ, TQ), lambda bi, gi, qi: (gi, 0, 0, 0)),
            pl.BlockSpec((1, hpg, 2 * LANES, TQ), lambda bi, gi, qi: (gi, 0, 0, 0)),
            pl.BlockSpec((LANES, LANES), lambda bi, gi, qi: (0, 0)),
            pl.BlockSpec((t, LANES), lambda bi, gi, qi: (0, 0)),
        ],
        out_specs=pl.BlockSpec((1, TQ, qw), lambda bi, gi, qi: (bi, qi, gi)),
        out_shape=jax.ShapeDtypeStruct((b, t, NSA_HEADS * NSA_DH), BF16),
        scratch_shapes=[pltpu.VMEM((TQ + t, LANES), BF16), pltpu.VMEM((WINDOW + t, LANES), BF16),
                        pltpu.VMEM((LANES, TQ + t), BF16), pltpu.VMEM((LANES, WINDOW + t), BF16),
                        pltpu.VMEM((LANES, hpg * TQ), F32)],
        compiler_params=_cparams(("arbitrary", "arbitrary", "arbitrary")),
    )(h3, kvc, h3, h3, h3, qgain, kgain, bias_nd, bias_w, bias_cmp, overlap_t, kflag)


def _bias_kernel(tbl_ref, nd_ref, w_ref, bc_ref, ond_ref, ow_ref, obc_ref):
    h = pl.program_id(0)
    far = tbl_ref[NUM_BUCKETS - 1, h]
    nd = nd_ref[...]
    w = w_ref[...]
    bc = bc_ref[...]
    o_nd = jnp.full(nd.shape, NEG, F32)
    o_w = jnp.full(w.shape, NEG, F32)
    o_bc = jnp.zeros(bc.shape, F32)
    for bucket in range(NUM_BUCKETS):
        v = tbl_ref[bucket, h]
        o_nd = jnp.where(nd == bucket, (v - far) * LOG2E, o_nd)
        o_w = jnp.where(w == bucket, (v - far) * LOG2E, o_w)
        o_bc = jnp.where(bc == bucket, v * LOG2E, o_bc)
    ond_ref[0] = o_nd
    ow_ref[0] = o_w
    obc_ref[0] = o_bc


def _bias_tiles(rel_bias, nd_bucket, w_bucket, bc_bucket):
    n_heads = rel_bias.shape[1]
    tables = (nd_bucket, w_bucket, bc_bucket)
    return pl.pallas_call(
        _bias_kernel,
        grid=(n_heads,),
        in_specs=[pl.BlockSpec(memory_space=pltpu.SMEM)] + [_resident(a.shape) for a in tables],
        out_specs=[pl.BlockSpec((1,) + a.shape, lambda h: (h, 0, 0)) for a in tables],
        out_shape=[jax.ShapeDtypeStruct((n_heads,) + a.shape, F32) for a in tables],
        compiler_params=_cparams(("parallel",)),
    )(rel_bias.astype(F32), *[jnp.asarray(a, jnp.int32) for a in tables])


def _t5_bucket_np(dist):
    n = np.maximum(dist, 0)
    max_exact = NUM_BUCKETS // 2
    nf = np.maximum(n, max_exact).astype(np.float32)
    large = max_exact + (np.log(nf / max_exact) / np.float32(np.log(MAX_DISTANCE / max_exact))
                         * (NUM_BUCKETS - max_exact)).astype(np.int32)
    large = np.minimum(large, NUM_BUCKETS - 1)
    return np.where(n < max_exact, n, large)


def _nsa_constants(t):
    n_cmp = (t - CMP_LEN) // CMP_STRIDE + 1
    n_slc = t // SLC_LEN
    n_blk = t // CMP_STRIDE
    per_tile = TQ // CMP_STRIDE
    assert n_blk <= LANES and per_tile * (t // TQ - 1) <= LANES - per_tile
    iq = np.arange(TQ)
    rel = iq[None, :] - iq[:, None]
    assert _t5_bucket_np(np.array([TQ + 1]))[0] == NUM_BUCKETS - 1
    dist_nd = np.concatenate([TQ + rel, rel], axis=0)
    nd_bucket = np.where(dist_nd >= 0, _t5_bucket_np(dist_nd), -1)
    dist_w = np.arange(TQ)[None, :] + WINDOW - np.arange(WINDOW + TQ)[:, None]
    w_bucket = np.where((dist_w >= 0) & (dist_w < WINDOW), _t5_bucket_np(dist_w), -1)
    blk_off = np.arange(LANES) - (LANES - per_tile)
    bc_bucket = _t5_bucket_np(iq[None, :] - CMP_STRIDE * blk_off[:, None] - (CMP_LEN - 1))
    bc_bucket = np.concatenate([bc_bucket, bc_bucket], axis=0)
    ci = np.arange(n_blk) * CMP_STRIDE
    sj = np.arange(n_slc) * SLC_LEN
    ov = ((ci[:, None] < sj[None, :] + SLC_LEN) & (ci[:, None] + CMP_LEN > sj[None, :])).astype(np.float32)
    ov[n_cmp:, :] = 0.0
    overlap_t = np.zeros((LANES, LANES), np.float32)
    overlap_t[:n_slc, :n_blk] = ov.T
    assert FLAG0 + n_slc <= PAD_LANE
    kflag = np.zeros((t, LANES), np.float32)
    kflag[np.arange(t), FLAG0 + np.arange(t) // SLC_LEN] = 1.0
    return nd_bucket, w_bucket, bc_bucket, overlap_t, kflag


def _nsa_w_in_columns():
    kv0 = NSA_HEADS * NSA_DH
    kvw = NSA_GROUPS * NSA_DH
    cols = list(range(kv0))
    for br in range(3):
        for g in range(NSA_GROUPS):
            k_src = kv0 + (2 * br) * kvw + g * NSA_DH
            v_src = kv0 + (2 * br + 1) * kvw + g * NSA_DH
            cols += list(range(k_src, k_src + NSA_DH)) + list(range(v_src, v_src + NSA_DH))
    gate0 = kv0 + 6 * kvw
    for g in range(NSA_GROUPS):
        blk = [-1] * LANES
        for br in range(3):
            for p in range(NSA_HPG):
                blk[br * NSA_HPG + p] = gate0 + br * NSA_HEADS + g * NSA_HPG + p
        cols += blk
    return np.asarray(cols, np.int32)


def _retention_layer(x2, b, t, norm_g, w_in, w_out):
    h = _norm_matmul(x2, norm_g, w_in.astype(BF16), tn=1024)
    o = _retention_core(h.reshape(b, t, -1))
    return _matmul_res(o.reshape(b * t, -1), w_out.astype(BF16), x2)


def _nsa_layer(x2, b, t, norm_g, w_in, w_out, q_gain, k_gain, cmp_pos, cmp_w1, cmp_w2, bias_nd, bias_w,
               bias_cmp, overlap_t, kflag):
    cols = _nsa_w_in_columns()
    w_k = jnp.where(cols[None, :] >= 0, jnp.take(w_in, np.maximum(cols, 0), axis=1), 0.0).astype(BF16)
    h3 = _norm_matmul(x2, norm_g, w_k, tn=1024).reshape(b, t, NSA_COLS)

    dh = NSA_DH
    w1 = cmp_w1.reshape(2, CMP_LEN, dh, CMP_HID)
    zero1 = jnp.zeros((CMP_LEN, dh, CMP_HID), F32)
    w1cat = jnp.concatenate([jnp.concatenate([w1[0], zero1], axis=-1),
                             jnp.concatenate([zero1, w1[1]], axis=-1)], axis=1).astype(BF16)
    zero2 = jnp.zeros((CMP_HID, dh), F32)
    w2cat = jnp.concatenate([jnp.concatenate([cmp_w2[0], zero2], axis=-1),
                             jnp.concatenate([zero2, cmp_w2[1]], axis=-1)], axis=0).astype(BF16)
    pos = jnp.concatenate([cmp_pos[0], cmp_pos[1]], axis=-1)
    ones = jnp.ones((dh,), F32)
    kvc = _compress(h3, pos, w1cat, w2cat, jnp.concatenate([k_gain[0], ones])[None, :])

    qgain = jnp.tile(q_gain * (dh ** -0.5 * LOG2E), NSA_HPG)[None, :]
    kgain = jnp.stack([jnp.concatenate([k_gain[1], ones]), jnp.concatenate([k_gain[2], ones])])
    o = _nsa_attention(h3, kvc, qgain, kgain, bias_nd, bias_w, bias_cmp, overlap_t, kflag)
    return _matmul_res(o.reshape(b * t, -1), w_out.astype(BF16), x2)


def kernel(x, ffn1_norm, ffn1_w_gu, ffn1_w_down, mix_norm, ffn2_norm, ffn2_w_gu, ffn2_w_down,
           ret_w_in, ret_w_out, nsa_w_in, nsa_w_out, nsa_q_gain, nsa_k_gain,
           nsa_cmp_pos, nsa_cmp_w1, nsa_cmp_w2, rel_bias):
    b, t, d = x.shape
    depth = ffn1_norm.shape[0]

    def ffn_weights(w_gu, w_down):
        return w_gu.astype(BF16), w_down.astype(BF16)

    nd_bucket, w_bucket, bc_bucket, overlap_t, kflag = _nsa_constants(t)
    bias_nd, bias_w, bias_cmp = _bias_tiles(rel_bias, nd_bucket, w_bucket, bc_bucket)
    bias_nd = bias_nd.reshape(NSA_GROUPS, NSA_HPG, 2 * TQ, TQ)
    bias_w = bias_w.reshape(NSA_GROUPS, NSA_HPG, WINDOW + TQ, TQ)
    bias_cmp = bias_cmp.reshape(NSA_GROUPS, NSA_HPG, 2 * LANES, TQ)
    overlap_t = jnp.asarray(overlap_t, BF16)
    kflag = jnp.asarray(kflag, BF16)

    x2 = x.reshape(b * t, d)
    for layer in range(depth):
        x2 = _ffn(x2, ffn1_norm[layer][None, :], *ffn_weights(ffn1_w_gu[layer], ffn1_w_down[layer]))
        j = layer // 2
        if layer % 2 == 0:
            x2 = _retention_layer(x2, b, t, mix_norm[layer][None, :], ret_w_in[j], ret_w_out[j])
        else:
            x2 = _nsa_layer(x2, b, t, mix_norm[layer][None, :], nsa_w_in[j], nsa_w_out[j], nsa_q_gain[j],
                            nsa_k_gain[j], nsa_cmp_pos[j], nsa_cmp_w1[j], nsa_cmp_w2[j], bias_nd, bias_w,
                            bias_cmp, overlap_t, kflag)
        x2 = _ffn(x2, ffn2_norm[layer][None, :], *ffn_weights(ffn2_w_gu[layer], ffn2_w_down[layer]))
    return x2.reshape(b, t, d)
```

```python
import functools

import numpy as np
import jax
import jax.numpy as jnp
from jax import lax
from jax.experimental import pallas as pl
from jax.experimental.pallas import tpu as pltpu

F32 = jnp.float32
BF16 = jnp.bfloat16

RMS_EPS = 1e-6
NEG = -1e30
SEL_BIG = 1e9
LOG2E = 1.4426950408889634

LANES = 128
VMEM_LIMIT = 56 * 1024 * 1024

RET_HEADS = 4
RET_DK = 256
RET_DV = 512
RET_CHUNK = 128
RET_ROPE_BASE = 10000.0

NSA_HEADS = 16
NSA_GROUPS = 4
NSA_HPG = 4
NSA_DH = 64
CMP_LEN = 32
CMP_STRIDE = 16
CMP_HID = 256
SLC_LEN = 64
N_SEL = 8
WINDOW = 512
NUM_BUCKETS = 32
MAX_DISTANCE = 128
TQ = 256
CK = 512
FLAG0 = 64
PAD_LANE = 96
NSA_COLS = 1024 + 3 * 4 * 128 + 4 * 128
NSA_KV0 = 1024 // 128


def _dot(a, b):
    return jnp.dot(a, b, preferred_element_type=F32)


def _dot_nt(a, b):
    return lax.dot_general(a, b, (((1,), (1,)), ((), ())), preferred_element_type=F32)


def _dot_tn(a, b):
    return lax.dot_general(a, b, (((0,), (0,)), ((), ())), preferred_element_type=F32)


def _rms(x, g):
    ms = jnp.mean(x * x, axis=-1, keepdims=True)
    return x * lax.rsqrt(ms + RMS_EPS) * g


def _cparams(sem):
    return pltpu.CompilerParams(dimension_semantics=sem, vmem_limit_bytes=VMEM_LIMIT)


def _resident(shape):
    n = len(shape)
    return pl.BlockSpec(shape, lambda *_: (0,) * n, pipeline_mode=pl.Buffered(1))


def _ffn_kernel(x_ref, g_ref, wgu_ref, wd_ref, o_ref):
    d_ff = wd_ref.shape[0]
    x = x_ref[...]
    xn = _rms(x, g_ref[...]).astype(BF16)
    a = _dot(xn, wgu_ref[:, :d_ff])
    b = _dot(xn, wgu_ref[:, d_ff:])
    h = (a * jax.nn.sigmoid(a) * b).astype(BF16)
    o_ref[...] = x + 0.5 * _dot(h, wd_ref[...])


def _ffn(x2, g, wgu, wd, tm=512):
    n, d = x2.shape
    return pl.pallas_call(
        _ffn_kernel,
        grid=(n // tm,),
        in_specs=[
            pl.BlockSpec((tm, d), lambda i: (i, 0)),
            _resident((1, d)),
            _resident(wgu.shape),
            _resident(wd.shape),
        ],
        out_specs=pl.BlockSpec((tm, d), lambda i: (i, 0)),
        out_shape=jax.ShapeDtypeStruct((n, d), F32),
        compiler_params=_cparams(("parallel",)),
    )(x2, g, wgu, wd)


def _norm_matmul_kernel(x_ref, g_ref, w_ref, o_ref, *, tn):
    xn = _rms(x_ref[...], g_ref[...]).astype(BF16)
    for j in range(w_ref.shape[1] // tn):
        o_ref[:, j * tn:(j + 1) * tn] = _dot(xn, w_ref[:, j * tn:(j + 1) * tn]).astype(o_ref.dtype)


def _norm_matmul(x2, g, w, tn, tm=512):
    n, d = x2.shape
    nout = w.shape[1]
    return pl.pallas_call(
        functools.partial(_norm_matmul_kernel, tn=tn),
        grid=(n // tm,),
        in_specs=[pl.BlockSpec((tm, d), lambda i: (i, 0)), _resident((1, d)), _resident((d, nout))],
        out_specs=pl.BlockSpec((tm, nout), lambda i: (i, 0)),
        out_shape=jax.ShapeDtypeStruct((n, nout), BF16),
        compiler_params=_cparams(("parallel",)),
    )(x2, g, w)


def _matmul_res_kernel(o_ref, w_ref, x_ref, y_ref):
    y_ref[...] = x_ref[...] + _dot(o_ref[...], w_ref[...])


def _matmul_res(o2, w, x2, tm=512):
    n, k = o2.shape
    d = w.shape[1]
    return pl.pallas_call(
        _matmul_res_kernel,
        grid=(n // tm,),
        in_specs=[pl.BlockSpec((tm, k), lambda i: (i, 0)), _resident((k, d)),
                  pl.BlockSpec((tm, d), lambda i: (i, 0))],
        out_specs=pl.BlockSpec((tm, d), lambda i: (i, 0)),
        out_shape=jax.ShapeDtypeStruct((n, d), F32),
        compiler_params=_cparams(("parallel",)),
    )(o2, w, x2)


def _ret_kernel(cd_ref, q_ref, k_ref, v_ref, g_ref, cos_ref, sin_ref, intra_ref, qd_ref, kd_ref,
                o_ref, s_ref, *, n_chunks):
    c_len = RET_CHUNK
    half = RET_DK // 2
    cd = cd_ref[pl.program_id(1)]
    intra = intra_ref[0]
    qd = qd_ref[0]
    kd = kd_ref[0]
    s_ref[...] = jnp.zeros_like(s_ref)

    def chunk(c, carry):
        r0 = pl.multiple_of(c * c_len, c_len)
        cos = cos_ref[pl.ds(r0, c_len), :]
        sin = sin_ref[pl.ds(r0, c_len), :]

        def rot(x):
            x1, x2 = x[:, :half], x[:, half:]
            return jnp.concatenate([x1 * cos - x2 * sin, x1 * sin + x2 * cos], axis=-1)

        qr = rot(q_ref[0, pl.ds(r0, c_len), :].astype(F32))
        kr = rot(k_ref[0, pl.ds(r0, c_len), :].astype(F32)) * (RET_DK ** -0.5)
        vc = v_ref[0, pl.ds(r0, c_len), :]
        s = _dot_nt(qr.astype(BF16), kr.astype(BF16)) * intra
        state = s_ref[...]
        o = _dot(s.astype(BF16), vc) + _dot((qr * qd).astype(BF16), state.astype(BF16))
        s_ref[...] = cd * state + _dot_tn((kr * kd).astype(BF16), vc)
        o = o * lax.rsqrt(jnp.mean(o * o, axis=-1, keepdims=True) + RMS_EPS)
        gate = g_ref[0, pl.ds(r0, c_len), :].astype(F32)
        o_ref[0, pl.ds(r0, c_len), :] = (gate * jax.nn.sigmoid(gate) * o).astype(o_ref.dtype)
        return carry

    lax.fori_loop(0, n_chunks, chunk, 0, unroll=4)


def _retention_core(h3):
    b, t, _ = h3.shape
    hh, dk, dv, c = RET_HEADS, RET_DK, RET_DV, RET_CHUNK
    half = dk // 2
    pos = jnp.arange(t, dtype=F32)
    inv_freq = RET_ROPE_BASE ** (-jnp.arange(half, dtype=F32) / half)
    ang = pos[:, None] * inv_freq[None, :]
    cos, sin = jnp.cos(ang), jnp.sin(ang)
    log_gamma = jnp.log(1.0 - 2.0 ** (-5.0 - jnp.arange(hh, dtype=F32)))
    i = jnp.arange(c)
    diff = i[:, None] - i[None, :]
    intra = jnp.where(diff >= 0, jnp.exp(log_gamma[:, None, None] * jnp.maximum(diff, 0)), 0.0)
    q_dec = jnp.exp(log_gamma[:, None] * (i + 1))[:, :, None]
    k_dec = jnp.exp(log_gamma[:, None] * (c - 1 - i))[:, :, None]
    chunk_dec = jnp.exp(log_gamma * c)
    k0 = hh
    v0 = 2 * hh * dk // dv
    g0 = v0 + hh
    return pl.pallas_call(
        functools.partial(_ret_kernel, n_chunks=t // c),
        grid=(b, hh),
        in_specs=[
            pl.BlockSpec(memory_space=pltpu.SMEM),
            pl.BlockSpec((1, t, dk), lambda bi, hi: (bi, 0, hi)),
            pl.BlockSpec((1, t, dk), lambda bi, hi: (bi, 0, k0 + hi)),
            pl.BlockSpec((1, t, dv), lambda bi, hi: (bi, 0, v0 + hi)),
            pl.BlockSpec((1, t, dv), lambda bi, hi: (bi, 0, g0 + hi)),
            _resident((t, half)),
            _resident((t, half)),
            pl.BlockSpec((1, c, c), lambda bi, hi: (hi, 0, 0)),
            pl.BlockSpec((1, c, 1), lambda bi, hi: (hi, 0, 0)),
            pl.BlockSpec((1, c, 1), lambda bi, hi: (hi, 0, 0)),
        ],
        out_specs=pl.BlockSpec((1, t, dv), lambda bi, hi: (bi, 0, hi)),
        out_shape=jax.ShapeDtypeStruct((b, t, hh * dv), BF16),
        scratch_shapes=[pltpu.VMEM((dk, dv), F32)],
        compiler_params=_cparams(("parallel", "arbitrary")),
    )(chunk_dec, h3, h3, h3, h3, cos, sin, intra, q_dec, k_dec)


def _k_headnorm(x, gain_row):
    klane = lax.broadcasted_iota(jnp.int32, x.shape, 1) < NSA_DH
    ms = jnp.sum(jnp.where(klane, x * x, 0.0), axis=-1, keepdims=True) * (1.0 / NSA_DH)
    return jnp.where(klane, x * lax.rsqrt(ms + RMS_EPS) * gain_row, x)


def _cmp_kernel(a_ref, pos_ref, w1_ref, w2_ref, kg_ref, o_ref, af_ref, *, t):
    n_blk = t // CMP_STRIDE
    af_ref[0:t, :] = a_ref[0].astype(F32)
    af_ref[t:t + CMP_LEN, :] = jnp.zeros((CMP_LEN, LANES), F32)
    acc = jnp.zeros((n_blk, 2 * CMP_HID), F32)
    for l in range(CMP_LEN):
        rows = af_ref[pl.ds(l, n_blk, stride=CMP_STRIDE), :]
        acc = acc + _dot((rows + pos_ref[l:l + 1, :]).astype(BF16), w1_ref[l])
    hid = jax.nn.gelu(acc, approximate=True)
    out = _dot(hid.astype(BF16), w2_ref[...])
    o_ref[0, 0] = _k_headnorm(out, kg_ref[...]).astype(o_ref.dtype)


def _compress(h3, pos, w1cat, w2cat, kgain_row):
    b, t, _ = h3.shape
    n_blk = t // CMP_STRIDE
    return pl.pallas_call(
        functools.partial(_cmp_kernel, t=t),
        grid=(b, NSA_GROUPS),
        in_specs=[
            pl.BlockSpec((1, t, LANES), lambda bi, gi: (bi, 0, NSA_KV0 + gi)),
            _resident((CMP_LEN, LANES)),
            _resident((CMP_LEN, LANES, 2 * CMP_HID)),
            _resident((2 * CMP_HID, LANES)),
            _resident((1, LANES)),
        ],
        out_specs=pl.BlockSpec((1, 1, n_blk, LANES), lambda bi, gi: (bi, gi, 0, 0)),
        out_shape=jax.ShapeDtypeStruct((b, NSA_GROUPS, n_blk, LANES), BF16),
        scratch_shapes=[pltpu.VMEM((t + CMP_LEN, LANES), F32)],
        compiler_params=_cparams(("parallel", "parallel")),
    )(h3, pos, w1cat, w2cat, kgain_row)


def _nsa_attn_kernel(q_ref, kvc_ref, kvs_ref, kvw_ref, gate_ref, qg_ref, kg_ref, bnd_ref, bw_ref, bc_ref,
                     ovt_ref, kflag_ref, o_ref, ks_ref, kw_ref, vst_ref, vwt_ref, sel_ref, *, t):
    hpg = NSA_HPG
    dh = NSA_DH
    qt = pl.program_id(2)
    q0 = pl.multiple_of(qt * TQ, TQ)
    lane = lax.broadcasted_iota(jnp.int32, (1, LANES), 1)
    klane = lane < dh
    ones_rows = lax.broadcasted_iota(jnp.int32, (LANES, LANES), 0) < dh

    @pl.when(qt == 0)
    def _():
        pad_rows = jnp.where(lane == PAD_LANE, 1.0, 0.0).astype(BF16)
        ks_ref[0:TQ, :] = jnp.broadcast_to(pad_rows, (TQ, LANES))
        kw_ref[0:WINDOW, :] = jnp.broadcast_to(pad_rows, (WINDOW, LANES))
        pad_col = jnp.where(ones_rows, 1.0, 0.0).astype(BF16)
        for i in range(TQ // LANES):
            vst_ref[:, i * LANES:(i + 1) * LANES] = pad_col
        for i in range(WINDOW // LANES):
            vwt_ref[:, i * LANES:(i + 1) * LANES] = pad_col

        def norm_block(c, carry):
            r0 = pl.multiple_of(c * LANES, LANES)
            for src, dst, dst_t, pad, gi in ((kvs_ref, ks_ref, vst_ref, TQ, 0), (kvw_ref, kw_ref, vwt_ref, WINDOW, 1)):
                x = _k_headnorm(src[0, pl.ds(r0, LANES), :].astype(F32), kg_ref[gi:gi + 1, :])
                flags = kflag_ref[pl.ds(r0, LANES), :] if gi == 0 else jnp.zeros((LANES, LANES), BF16)
                dst[pl.ds(pad + r0, LANES), :] = jnp.where(klane, x.astype(BF16), flags)
                dst_t[:, pl.ds(pad + r0, LANES)] = jnp.where(ones_rows, 1.0, x.T).astype(BF16)
            return carry

        lax.fori_loop(0, t // LANES, norm_block, 0, unroll=4)

    xq = q_ref[0].astype(F32)
    lane_q = lax.broadcasted_iota(jnp.int32, (1, hpg * dh), 1)
    xq2 = xq * xq
    qz = []
    for p in range(hpg):
        seg = (lane_q >= dh * p) & (lane_q < dh * (p + 1))
        ms = jnp.sum(jnp.where(seg, xq2, 0.0), axis=-1, keepdims=True) * (1.0 / dh)
        c0 = LANES * (p // 2)
        half = xq[:, c0:c0 + LANES] * lax.rsqrt(ms + RMS_EPS) * qg_ref[:, c0:c0 + LANES]
        if p % 2 == 1:
            half = pltpu.roll(half, dh, 1)
        qz.append(jnp.where(klane, half, 0.0))

    def stacked_q(flag_lanes):
        return jnp.concatenate([(qp + flag_lanes).astype(BF16) for qp in qz], axis=0)

    ww = WINDOW + TQ
    s_w = _dot_nt(kw_ref[pl.ds(q0, ww), :], stacked_q(jnp.where(lane == PAD_LANE, NEG, 0.0)))
    s = jnp.concatenate([s_w[:, p * TQ:(p + 1) * TQ] + bw_ref[0, p] for p in range(hpg)], axis=1)
    m_win = jnp.max(s, axis=0, keepdims=True)
    acc_win = _dot(vwt_ref[:, pl.ds(q0, ww)], jnp.exp2(s - m_win).astype(BF16))

    per_tile = TQ // CMP_STRIDE
    off = pl.multiple_of((LANES - per_tile) - qt * per_tile, per_tile)
    blk_i = lax.broadcasted_iota(jnp.int32, (LANES, TQ), 0)
    valid_c = (CMP_STRIDE * blk_i + (CMP_LEN - 1)) <= (q0 + lax.broadcasted_iota(jnp.int32, (LANES, TQ), 1))
    kvc = kvc_ref[0, 0]
    s_c = _dot_nt(kvc, stacked_q(0.0))
    pcs = []
    for p in range(hpg):
        s = jnp.where(valid_c, s_c[:, p * TQ:(p + 1) * TQ] + bc_ref[0, p, pl.ds(off, LANES), :], NEG)
        m = jnp.max(s, axis=0, keepdims=True)
        e = jnp.where(valid_c, jnp.exp2(s - m), 0.0)
        l = jnp.sum(e, axis=0, keepdims=True)
        pcs.append(e / jnp.where(l > 0.0, l, 1.0))
    vc_t = jnp.where(ones_rows, 1.0, kvc.astype(F32).T).astype(BF16)
    o_cmp = _dot(vc_t, jnp.concatenate(pcs, axis=1).astype(BF16))
    psum = pcs[0]
    for p in range(1, hpg):
        psum = psum + pcs[p]
    p_hi = psum.astype(BF16)
    p_lo = (psum - p_hi.astype(F32)).astype(BF16)
    n_slc = t // SLC_LEN
    imp_t = (_dot(ovt_ref[...], p_hi) + _dot(ovt_ref[...], p_lo))[:n_slc, :]

    jrow = lax.broadcasted_iota(jnp.int32, (n_slc, TQ), 0)
    cur = (q0 + lax.broadcasted_iota(jnp.int32, (n_slc, TQ), 1)) // SLC_LEN
    forced = (jrow == 0) | (jrow == cur) | (jrow == cur - 1)
    vals = jnp.where(forced, SEL_BIG, jnp.where(jrow <= cur, imp_t, -SEL_BIG))
    terms = []
    for jp in range(n_slc):
        row = vals[jp:jp + 1, :]
        ahead = (row > vals) | ((row == vals) & (jrow > jp))
        terms.append(jnp.where(ahead, 1.0, 0.0))
    while len(terms) > 1:
        terms = [a + b for a, b in zip(terms[0::2], terms[1::2])]
    cnt = terms[0]

    far_end = (qt - 1) * TQ
    drop_nd = jnp.where(cnt < float(N_SEL), 0.0, NEG)
    drop_far = jnp.where(jrow * SLC_LEN < far_end, drop_nd, NEG)

    def flag_lanes(drop):
        rows = [jnp.zeros((FLAG0, TQ), F32), drop, jnp.full((PAD_LANE - FLAG0 - n_slc + 8, TQ), NEG, F32),
                jnp.zeros((LANES - PAD_LANE - 8, TQ), F32)]
        return jnp.concatenate(rows, axis=0).T

    ck = CK
    q_nd = stacked_q(flag_lanes(drop_nd))
    q_far = stacked_q(flag_lanes(drop_far))
    n_far = jnp.maximum(far_end + ck - 1, 0) // ck

    s_nd = _dot_nt(ks_ref[pl.ds(q0, 2 * TQ), :], q_nd)
    s_nd = jnp.concatenate([s_nd[:, p * TQ:(p + 1) * TQ] + bnd_ref[0, p] for p in range(hpg)], axis=1)

    def select_branch(k):
        scores = [s_nd]
        values = [vst_ref[:, pl.ds(q0, 2 * TQ)]]
        for c in range(k):
            scores.append(_dot_nt(ks_ref[TQ + c * ck:TQ + (c + 1) * ck, :], q_far))
            values.append(vst_ref[:, TQ + c * ck:TQ + (c + 1) * ck])
        m = jnp.max(scores[0], axis=0, keepdims=True)
        for s in scores[1:]:
            m = jnp.maximum(m, jnp.max(s, axis=0, keepdims=True))
        acc = _dot(values[0], jnp.exp2(scores[0] - m).astype(BF16))
        for s, v in zip(scores[1:], values[1:]):
            acc = acc + _dot(v, jnp.exp2(s - m).astype(BF16))
        sel_ref[...] = acc

    for k in range((t - 2 * TQ) // ck + 1):
        pl.when(n_far == k)(functools.partial(select_branch, k))
    acc_sel = sel_ref[...]

    gl_t = jax.nn.sigmoid(gate_ref[0].astype(F32)).T
    outs = []
    for p in range(hpg):
        sl = slice(p * TQ, (p + 1) * TQ)
        g_c = gl_t[p:p + 1, :]
        g_s = gl_t[hpg + p:hpg + p + 1, :] / acc_sel[0:1, sl]
        g_w = gl_t[2 * hpg + p:2 * hpg + p + 1, :] / acc_win[0:1, sl]
        outs.append(g_c * o_cmp[dh:, sl] + g_s * acc_sel[dh:, sl] + g_w * acc_win[dh:, sl])
    o_ref[0] = jnp.concatenate(outs, axis=0).T.astype(o_ref.dtype)


def _nsa_attention(h3, kvc, qgain, kgain, bias_nd, bias_w, bias_cmp, overlap_t, kflag):
    b, t, _ = h3.shape
    g, hpg = NSA_GROUPS, NSA_HPG
    n_blk = t // CMP_STRIDE
    qw = hpg * NSA_DH
    sel0, win0, gate0 = NSA_KV0 + g, NSA_KV0 + 2 * g, NSA_KV0 + 3 * g
    return pl.pallas_call(
        functools.partial(_nsa_attn_kernel, t=t),
        grid=(g, b, t // TQ),
        in_specs=[
            pl.BlockSpec((1, TQ, qw), lambda gi, bi, qi: (bi, qi, gi)),
            pl.BlockSpec((1, 1, n_blk, LANES), lambda gi, bi, qi: (bi, gi, 0, 0)),
            pl.BlockSpec((1, t, LANES), lambda gi, bi, qi: (bi, 0, sel0 + gi)),
            pl.BlockSpec((1, t, LANES), lambda gi, bi, qi: (bi, 0, win0 + gi)),
            pl.BlockSpec((1, TQ, LANES), lambda gi, bi, qi: (bi, qi, gate0 + gi)),
            pl.BlockSpec((1, qw), lambda gi, bi, qi: (0, 0)),
            pl.BlockSpec((2, LANES), lambda gi, bi, qi: (0, 0)),
            pl.BlockSpec((1, hpg, 2 * TQ, TQ), lambda gi, bi, qi: (gi, 0, 0, 0)),
            pl.BlockSpec((1, hpg, WINDOW + TQ, TQ), lambda gi, bi, qi: (gi, 0, 0, 0)),
            pl.BlockSpec((1, hpg, 2 * LANES, TQ), lambda gi, bi, qi: (gi, 0, 0, 0)),
            pl.BlockSpec((LANES, LANES), lambda gi, bi, qi: (0, 0)),
            pl.BlockSpec((t, LANES), lambda gi, bi, qi: (0, 0)),
        ],
        out_specs=pl.BlockSpec((1, TQ, qw), lambda gi, bi, qi: (bi, qi, gi)),
        out_shape=jax.ShapeDtypeStruct((b, t, NSA_HEADS * NSA_DH), BF16),
        scratch_shapes=[pltpu.VMEM((TQ + t, LANES), BF16), pltpu.VMEM((WINDOW + t, LANES), BF16),
                        pltpu.VMEM((LANES, TQ + t), BF16), pltpu.VMEM((LANES, WINDOW + t), BF16),
                        pltpu.VMEM((LANES, hpg * TQ), F32)],
        compiler_params=_cparams(("arbitrary", "arbitrary", "arbitrary")),
    )(h3, kvc, h3, h3, h3, qgain, kgain, bias_nd, bias_w, bias_cmp, overlap_t, kflag)


def _bias_kernel(tbl_ref, nd_ref, w_ref, bc_ref, ond_ref, ow_ref, obc_ref):
    h = pl.program_id(0)
    far = tbl_ref[NUM_BUCKETS - 1, h]
    nd = nd_ref[...]
    w = w_ref[...]
    bc = bc_ref[...]
    o_nd = jnp.full(nd.shape, NEG, F32)
    o_w = jnp.full(w.shape, NEG, F32)
    o_bc = jnp.zeros(bc.shape, F32)
    for bucket in range(NUM_BUCKETS):
        v = tbl_ref[bucket, h]
        o_nd = jnp.where(nd == bucket, (v - far) * LOG2E, o_nd)
        o_w = jnp.where(w == bucket, (v - far) * LOG2E, o_w)
        o_bc = jnp.where(bc == bucket, v * LOG2E, o_bc)
    ond_ref[0] = o_nd
    ow_ref[0] = o_w
    obc_ref[0] = o_bc


def _bias_tiles(rel_bias, nd_bucket, w_bucket, bc_bucket):
    n_heads = rel_bias.shape[1]
    tables = (nd_bucket, w_bucket, bc_bucket)
    return pl.pallas_call(
        _bias_kernel,
        grid=(n_heads,),
        in_specs=[pl.BlockSpec(memory_space=pltpu.SMEM)] + [_resident(a.shape) for a in tables],
        out_specs=[pl.BlockSpec((1,) + a.shape, lambda h: (h, 0, 0)) for a in tables],
        out_shape=[jax.ShapeDtypeStruct((n_heads,) + a.shape, F32) for a in tables],
        compiler_params=_cparams(("parallel",)),
    )(rel_bias.astype(F32), *[jnp.asarray(a, jnp.int32) for a in tables])


def _t5_bucket_np(dist):
    n = np.maximum(dist, 0)
    max_exact = NUM_BUCKETS // 2
    nf = np.maximum(n, max_exact).astype(np.float32)
    large = max_exact + (np.log(nf / max_exact) / np.float32(np.log(MAX_DISTANCE / max_exact))
                         * (NUM_BUCKETS - max_exact)).astype(np.int32)
    large = np.minimum(large, NUM_BUCKETS - 1)
    return np.where(n < max_exact, n, large)


def _nsa_constants(t):
    n_cmp = (t - CMP_LEN) // CMP_STRIDE + 1
    n_slc = t // SLC_LEN
    n_blk = t // CMP_STRIDE
    per_tile = TQ // CMP_STRIDE
    assert n_blk <= LANES and per_tile * (t // TQ - 1) <= LANES - per_tile
    iq = np.arange(TQ)
    rel = iq[None, :] - iq[:, None]
    assert _t5_bucket_np(np.array([TQ + 1]))[0] == NUM_BUCKETS - 1
    dist_nd = np.concatenate([TQ + rel, rel], axis=0)
    nd_bucket = np.where(dist_nd >= 0, _t5_bucket_np(dist_nd), -1)
    dist_w = np.arange(TQ)[None, :] + WINDOW - np.arange(WINDOW + TQ)[:, None]
    w_bucket = np.where((dist_w >= 0) & (dist_w < WINDOW), _t5_bucket_np(dist_w), -1)
    blk_off = np.arange(LANES) - (LANES - per_tile)
    bc_bucket = _t5_bucket_np(iq[None, :] - CMP_STRIDE * blk_off[:, None] - (CMP_LEN - 1))
    bc_bucket = np.concatenate([bc_bucket, bc_bucket], axis=0)
    ci = np.arange(n_blk) * CMP_STRIDE
    sj = np.arange(n_slc) * SLC_LEN
    ov = ((ci[:, None] < sj[None, :] + SLC_LEN) & (ci[:, None] + CMP_LEN > sj[None, :])).astype(np.float32)
    ov[n_cmp:, :] = 0.0
    overlap_t = np.zeros((LANES, LANES), np.float32)
    overlap_t[:n_slc, :n_blk] = ov.T
    assert FLAG0 + n_slc <= PAD_LANE
    kflag = np.zeros((t, LANES), np.float32)
    kflag[np.arange(t), FLAG0 + np.arange(t) // SLC_LEN] = 1.0
    return nd_bucket, w_bucket, bc_bucket, overlap_t, kflag


def _nsa_w_in_columns():
    kv0 = NSA_HEADS * NSA_DH
    kvw = NSA_GROUPS * NSA_DH
    cols = list(range(kv0))
    for br in range(3):
        for g in range(NSA_GROUPS):
            k_src = kv0 + (2 * br) * kvw + g * NSA_DH
            v_src = kv0 + (2 * br + 1) * kvw + g * NSA_DH
            cols += list(range(k_src, k_src + NSA_DH)) + list(range(v_src, v_src + NSA_DH))
    gate0 = kv0 + 6 * kvw
    for g in range(NSA_GROUPS):
        blk = [-1] * LANES
        for br in range(3):
            for p in range(NSA_HPG):
                blk[br * NSA_HPG + p] = gate0 + br * NSA_HEADS + g * NSA_HPG + p
        cols += blk
    return np.asarray(cols, np.int32)


def _retention_layer(x2, b, t, norm_g, w_in, w_out):
    h = _norm_matmul(x2, norm_g, w_in.astype(BF16), tn=1024)
    o = _retention_core(h.reshape(b, t, -1))
    return _matmul_res(o.reshape(b * t, -1), w_out.astype(BF16), x2)


def _nsa_layer(x2, b, t, norm_g, w_in, w_out, q_gain, k_gain, cmp_pos, cmp_w1, cmp_w2, bias_nd, bias_w,
               bias_cmp, overlap_t, kflag):
    cols = _nsa_w_in_columns()
    w_k = jnp.where(cols[None, :] >= 0, jnp.take(w_in, np.maximum(cols, 0), axis=1), 0.0).astype(BF16)
    h3 = _norm_matmul(x2, norm_g, w_k, tn=1024).reshape(b, t, NSA_COLS)

    dh = NSA_DH
    w1 = cmp_w1.reshape(2, CMP_LEN, dh, CMP_HID)
    zero1 = jnp.zeros((CMP_LEN, dh, CMP_HID), F32)
    w1cat = jnp.concatenate([jnp.concatenate([w1[0], zero1], axis=-1),
                             jnp.concatenate([zero1, w1[1]], axis=-1)], axis=1).astype(BF16)
    zero2 = jnp.zeros((CMP_HID, dh), F32)
    w2cat = jnp.concatenate([jnp.concatenate([cmp_w2[0], zero2], axis=-1),
                             jnp.concatenate([zero2, cmp_w2[1]], axis=-1)], axis=0).astype(BF16)
    pos = jnp.concatenate([cmp_pos[0], cmp_pos[1]], axis=-1)
    ones = jnp.ones((dh,), F32)
    kvc = _compress(h3, pos, w1cat, w2cat, jnp.concatenate([k_gain[0], ones])[None, :])

    qgain = jnp.tile(q_gain * (dh ** -0.5 * LOG2E), NSA_HPG)[None, :]
    kgain = jnp.stack([jnp.concatenate([k_gain[1], ones]), jnp.concatenate([k_gain[2], ones])])
    o = _nsa_attention(h3, kvc, qgain, kgain, bias_nd, bias_w, bias_cmp, overlap_t, kflag)
    return _matmul_res(o.reshape(b * t, -1), w_out.astype(BF16), x2)


def kernel(x, ffn1_norm, ffn1_w_gu, ffn1_w_down, mix_norm, ffn2_norm, ffn2_w_gu, ffn2_w_down,
           ret_w_in, ret_w_out, nsa_w_in, nsa_w_out, nsa_q_gain, nsa_k_gain,
           nsa_cmp_pos, nsa_cmp_w1, nsa_cmp_w2, rel_bias):
    b, t, d = x.shape
    depth = ffn1_norm.shape[0]

    def ffn_weights(w_gu, w_down):
        return w_gu.astype(BF16), w_down.astype(BF16)

    nd_bucket, w_bucket, bc_bucket, overlap_t, kflag = _nsa_constants(t)
    bias_nd, bias_w, bias_cmp = _bias_tiles(rel_bias, nd_bucket, w_bucket, bc_bucket)
    bias_nd = bias_nd.reshape(NSA_GROUPS, NSA_HPG, 2 * TQ, TQ)
    bias_w = bias_w.reshape(NSA_GROUPS, NSA_HPG, WINDOW + TQ, TQ)
    bias_cmp = bias_cmp.reshape(NSA_GROUPS, NSA_HPG, 2 * LANES, TQ)
    overlap_t = jnp.asarray(overlap_t, BF16)
    kflag = jnp.asarray(kflag, BF16)

    x2 = x.reshape(b * t, d)
    for layer in range(depth):
        x2 = _ffn(x2, ffn1_norm[layer][None, :], *ffn_weights(ffn1_w_gu[layer], ffn1_w_down[layer]))
        j = layer // 2
        if layer % 2 == 0:
            x2 = _retention_layer(x2, b, t, mix_norm[layer][None, :], ret_w_in[j], ret_w_out[j])
        else:
            x2 = _nsa_layer(x2, b, t, mix_norm[layer][None, :], nsa_w_in[j], nsa_w_out[j], nsa_q_gain[j],
                            nsa_k_gain[j], nsa_cmp_pos[j], nsa_cmp_w1[j], nsa_cmp_w2[j], bias_nd, bias_w,
                            bias_cmp, overlap_t, kflag)
        x2 = _ffn(x2, ffn2_norm[layer][None, :], *ffn_weights(ffn2_w_gu[layer], ffn2_w_down[layer]))
    return x2.reshape(b, t, d)
```

```python
import functools

import numpy as np
import jax
import jax.numpy as jnp
from jax import lax
from jax.experimental import pallas as pl
from jax.experimental.pallas import tpu as pltpu

F32 = jnp.float32
BF16 = jnp.bfloat16

RMS_EPS = 1e-6
NEG = -1e30
SEL_BIG = 1e9
LOG2E = 1.4426950408889634

LANES = 128
VMEM_LIMIT = 56 * 1024 * 1024

RET_HEADS = 4
RET_DK = 256
RET_DV = 512
RET_CHUNK = 128
RET_ROPE_BASE = 10000.0

NSA_HEADS = 16
NSA_GROUPS = 4
NSA_HPG = 4
NSA_DH = 64
CMP_LEN = 32
CMP_STRIDE = 16
CMP_HID = 256
SLC_LEN = 64
N_SEL = 8
WINDOW = 512
NUM_BUCKETS = 32
MAX_DISTANCE = 128
TQ = 256
CK = 512
FLAG0 = 64
PAD_LANE = 96
NSA_COLS = 1024 + 3 * 4 * 128 + 4 * 128
NSA_KV0 = 1024 // 128


def _dot(a, b):
    return jnp.dot(a, b, preferred_element_type=F32)


def _dot_nt(a, b):
    return lax.dot_general(a, b, (((1,), (1,)), ((), ())), preferred_element_type=F32)


def _dot_tn(a, b):
    return lax.dot_general(a, b, (((0,), (0,)), ((), ())), preferred_element_type=F32)


def _rms(x, g):
    ms = jnp.mean(x * x, axis=-1, keepdims=True)
    return x * lax.rsqrt(ms + RMS_EPS) * g


def _cparams(sem):
    return pltpu.CompilerParams(dimension_semantics=sem, vmem_limit_bytes=VMEM_LIMIT)


def _resident(shape):
    n = len(shape)
    return pl.BlockSpec(shape, lambda *_: (0,) * n, pipeline_mode=pl.Buffered(1))


def _ffn_kernel(x_ref, g_ref, wgu_ref, wd_ref, o_ref):
    d_ff = wd_ref.shape[0]
    x = x_ref[...]
    xn = _rms(x, g_ref[...]).astype(BF16)
    a = _dot(xn, wgu_ref[:, :d_ff])
    b = _dot(xn, wgu_ref[:, d_ff:])
    h = (a * jax.nn.sigmoid(a) * b).astype(BF16)
    o_ref[...] = x + 0.5 * _dot(h, wd_ref[...])


def _ffn(x2, g, wgu, wd, tm=512):
    n, d = x2.shape
    return pl.pallas_call(
        _ffn_kernel,
        grid=(n // tm,),
        in_specs=[
            pl.BlockSpec((tm, d), lambda i: (i, 0)),
            _resident((1, d)),
            _resident(wgu.shape),
            _resident(wd.shape),
        ],
        out_specs=pl.BlockSpec((tm, d), lambda i: (i, 0)),
        out_shape=jax.ShapeDtypeStruct((n, d), F32),
        compiler_params=_cparams(("parallel",)),
    )(x2, g, wgu, wd)


def _proj_ffn_kernel(o_ref, wo_ref, x_ref, g_ref, wgu_ref, wd_ref, y_ref):
    d_ff = wd_ref.shape[0]
    x = x_ref[...] + _dot(o_ref[...], wo_ref[...])
    xn = _rms(x, g_ref[...]).astype(BF16)
    a = _dot(xn, wgu_ref[:, :d_ff])
    b = _dot(xn, wgu_ref[:, d_ff:])
    h = (a * jax.nn.sigmoid(a) * b).astype(BF16)
    y_ref[...] = x + 0.5 * _dot(h, wd_ref[...])


def _proj_ffn(o2, wo, x2, g, wgu, wd, tm=512):
    n, d = x2.shape
    k = o2.shape[1]
    return pl.pallas_call(
        _proj_ffn_kernel,
        grid=(n // tm,),
        in_specs=[
            pl.BlockSpec((tm, k), lambda i: (i, 0)),
            _resident(wo.shape),
            pl.BlockSpec((tm, d), lambda i: (i, 0)),
            _resident((1, d)),
            _resident(wgu.shape),
            _resident(wd.shape),
        ],
        out_specs=pl.BlockSpec((tm, d), lambda i: (i, 0)),
        out_shape=jax.ShapeDtypeStruct((n, d), F32),
        compiler_params=_cparams(("parallel",)),
    )(o2, wo, x2, g, wgu, wd)


def _norm_matmul_kernel(x_ref, g_ref, w_ref, o_ref, *, tn):
    xn = _rms(x_ref[...], g_ref[...]).astype(BF16)
    for j in range(w_ref.shape[1] // tn):
        o_ref[:, j * tn:(j + 1) * tn] = _dot(xn, w_ref[:, j * tn:(j + 1) * tn]).astype(o_ref.dtype)


def _norm_matmul(x2, g, w, tn, tm=512):
    n, d = x2.shape
    nout = w.shape[1]
    return pl.pallas_call(
        functools.partial(_norm_matmul_kernel, tn=tn),
        grid=(n // tm,),
        in_specs=[pl.BlockSpec((tm, d), lambda i: (i, 0)), _resident((1, d)), _resident((d, nout))],
        out_specs=pl.BlockSpec((tm, nout), lambda i: (i, 0)),
        out_shape=jax.ShapeDtypeStruct((n, nout), BF16),
        compiler_params=_cparams(("parallel",)),
    )(x2, g, w)


def _ret_kernel(cd_ref, q_ref, k_ref, v_ref, g_ref, cos_ref, sin_ref, intra_ref, qd_ref, kd_ref,
                o_ref, s_ref, *, n_chunks):
    c_len = RET_CHUNK
    half = RET_DK // 2
    cd = cd_ref[pl.program_id(1)]
    intra = intra_ref[0]
    qd = qd_ref[0]
    kd = kd_ref[0]
    s_ref[...] = jnp.zeros_like(s_ref)

    def chunk(c, carry):
        r0 = pl.multiple_of(c * c_len, c_len)
        cos = cos_ref[pl.ds(r0, c_len), :]
        sin = sin_ref[pl.ds(r0, c_len), :]

        def rot(x):
            x1, x2 = x[:, :half], x[:, half:]
            return jnp.concatenate([x1 * cos - x2 * sin, x1 * sin + x2 * cos], axis=-1)

        qr = rot(q_ref[0, pl.ds(r0, c_len), :].astype(F32))
        kr = rot(k_ref[0, pl.ds(r0, c_len), :].astype(F32)) * (RET_DK ** -0.5)
        vc = v_ref[0, pl.ds(r0, c_len), :]
        s = _dot_nt(qr.astype(BF16), kr.astype(BF16)) * intra
        state = s_ref[...]
        o = _dot(s.astype(BF16), vc) + _dot((qr * qd).astype(BF16), state.astype(BF16))
        s_ref[...] = cd * state + _dot_tn((kr * kd).astype(BF16), vc)
        o = o * lax.rsqrt(jnp.mean(o * o, axis=-1, keepdims=True) + RMS_EPS)
        gate = g_ref[0, pl.ds(r0, c_len), :].astype(F32)
        o_ref[0, pl.ds(r0, c_len), :] = (gate * jax.nn.sigmoid(gate) * o).astype(o_ref.dtype)
        return carry

    lax.fori_loop(0, n_chunks, chunk, 0, unroll=4)


def _retention_core(h3):
    b, t, _ = h3.shape
    hh, dk, dv, c = RET_HEADS, RET_DK, RET_DV, RET_CHUNK
    half = dk // 2
    pos = jnp.arange(t, dtype=F32)
    inv_freq = RET_ROPE_BASE ** (-jnp.arange(half, dtype=F32) / half)
    ang = pos[:, None] * inv_freq[None, :]
    cos, sin = jnp.cos(ang), jnp.sin(ang)
    log_gamma = jnp.log(1.0 - 2.0 ** (-5.0 - jnp.arange(hh, dtype=F32)))
    i = jnp.arange(c)
    diff = i[:, None] - i[None, :]
    intra = jnp.where(diff >= 0, jnp.exp(log_gamma[:, None, None] * jnp.maximum(diff, 0)), 0.0)
    q_dec = jnp.exp(log_gamma[:, None] * (i + 1))[:, :, None]
    k_dec = jnp.exp(log_gamma[:, None] * (c - 1 - i))[:, :, None]
    chunk_dec = jnp.exp(log_gamma * c)
    k0 = hh
    v0 = 2 * hh * dk // dv
    g0 = v0 + hh
    return pl.pallas_call(
        functools.partial(_ret_kernel, n_chunks=t // c),
        grid=(b, hh),
        in_specs=[
            pl.BlockSpec(memory_space=pltpu.SMEM),
            pl.BlockSpec((1, t, dk), lambda bi, hi: (bi, 0, hi)),
            pl.BlockSpec((1, t, dk), lambda bi, hi: (bi, 0, k0 + hi)),
            pl.BlockSpec((1, t, dv), lambda bi, hi: (bi, 0, v0 + hi)),
            pl.BlockSpec((1, t, dv), lambda bi, hi: (bi, 0, g0 + hi)),
            _resident((t, half)),
            _resident((t, half)),
            pl.BlockSpec((1, c, c), lambda bi, hi: (hi, 0, 0)),
            pl.BlockSpec((1, c, 1), lambda bi, hi: (hi, 0, 0)),
            pl.BlockSpec((1, c, 1), lambda bi, hi: (hi, 0, 0)),
        ],
        out_specs=pl.BlockSpec((1, t, dv), lambda bi, hi: (bi, 0, hi)),
        out_shape=jax.ShapeDtypeStruct((b, t, hh * dv), BF16),
        scratch_shapes=[pltpu.VMEM((dk, dv), F32)],
        compiler_params=_cparams(("parallel", "arbitrary")),
    )(chunk_dec, h3, h3, h3, h3, cos, sin, intra, q_dec, k_dec)


def _k_headnorm(x, gain_row):
    klane = lax.broadcasted_iota(jnp.int32, x.shape, 1) < NSA_DH
    ms = jnp.sum(jnp.where(klane, x * x, 0.0), axis=-1, keepdims=True) * (1.0 / NSA_DH)
    return jnp.where(klane, x * lax.rsqrt(ms + RMS_EPS) * gain_row, x)


def _cmp_kernel(a_ref, pos_ref, w1_ref, w2_ref, kg_ref, o_ref, af_ref, *, t):
    n_blk = t // CMP_STRIDE
    af_ref[0:t, :] = a_ref[0].astype(F32)
    af_ref[t:t + CMP_LEN, :] = jnp.zeros((CMP_LEN, LANES), F32)
    acc = jnp.zeros((n_blk, 2 * CMP_HID), F32)
    for l in range(CMP_LEN):
        rows = af_ref[pl.ds(l, n_blk, stride=CMP_STRIDE), :]
        acc = acc + _dot((rows + pos_ref[l:l + 1, :]).astype(BF16), w1_ref[l])
    hid = jax.nn.gelu(acc, approximate=True)
    out = _dot(hid.astype(BF16), w2_ref[...])
    o_ref[0, 0] = _k_headnorm(out, kg_ref[...]).astype(o_ref.dtype)


def _compress(h3, pos, w1cat, w2cat, kgain_row):
    b, t, _ = h3.shape
    n_blk = t // CMP_STRIDE
    return pl.pallas_call(
        functools.partial(_cmp_kernel, t=t),
        grid=(b, NSA_GROUPS),
        in_specs=[
            pl.BlockSpec((1, t, LANES), lambda bi, gi: (bi, 0, NSA_KV0 + gi)),
            _resident((CMP_LEN, LANES)),
            _resident((CMP_LEN, LANES, 2 * CMP_HID)),
            _resident((2 * CMP_HID, LANES)),
            _resident((1, LANES)),
        ],
        out_specs=pl.BlockSpec((1, 1, n_blk, LANES), lambda bi, gi: (bi, gi, 0, 0)),
        out_shape=jax.ShapeDtypeStruct((b, NSA_GROUPS, n_blk, LANES), BF16),
        scratch_shapes=[pltpu.VMEM((t + CMP_LEN, LANES), F32)],
        compiler_params=_cparams(("parallel", "parallel")),
    )(h3, pos, w1cat, w2cat, kgain_row)


def _nsa_attn_kernel(q_ref, kvc_ref, kvs_ref, kvw_ref, gate_ref, qg_ref, kg_ref, bnd_ref, bw_ref, bc_ref,
                     ovt_ref, kflag_ref, o_ref, ks_ref, kw_ref, vst_ref, vwt_ref, sel_ref, *, t):
    hpg = NSA_HPG
    dh = NSA_DH
    qt = pl.program_id(2)
    q0 = pl.multiple_of(qt * TQ, TQ)
    lane = lax.broadcasted_iota(jnp.int32, (1, LANES), 1)
    klane = lane < dh
    ones_rows = lax.broadcasted_iota(jnp.int32, (LANES, LANES), 0) < dh

    @pl.when(qt == 0)
    def _():
        pad_rows = jnp.where(lane == PAD_LANE, 1.0, 0.0).astype(BF16)
        ks_ref[0:TQ, :] = jnp.broadcast_to(pad_rows, (TQ, LANES))
        kw_ref[0:WINDOW, :] = jnp.broadcast_to(pad_rows, (WINDOW, LANES))
        pad_col = jnp.where(ones_rows, 1.0, 0.0).astype(BF16)
        for i in range(TQ // LANES):
            vst_ref[:, i * LANES:(i + 1) * LANES] = pad_col
        for i in range(WINDOW // LANES):
            vwt_ref[:, i * LANES:(i + 1) * LANES] = pad_col

        def norm_block(c, carry):
            r0 = pl.multiple_of(c * LANES, LANES)
            for src, dst, dst_t, pad, gi in ((kvs_ref, ks_ref, vst_ref, TQ, 0), (kvw_ref, kw_ref, vwt_ref, WINDOW, 1)):
                x = _k_headnorm(src[0, pl.ds(r0, LANES), :].astype(F32), kg_ref[gi:gi + 1, :])
                flags = kflag_ref[pl.ds(r0, LANES), :] if gi == 0 else jnp.zeros((LANES, LANES), BF16)
                dst[pl.ds(pad + r0, LANES), :] = jnp.where(klane, x.astype(BF16), flags)
                dst_t[:, pl.ds(pad + r0, LANES)] = jnp.where(ones_rows, 1.0, x.T).astype(BF16)
            return carry

        lax.fori_loop(0, t // LANES, norm_block, 0, unroll=4)

    xq = q_ref[0].astype(F32)
    lane_q = lax.broadcasted_iota(jnp.int32, (1, hpg * dh), 1)
    xq2 = xq * xq
    qz = []
    for p in range(hpg):
        seg = (lane_q >= dh * p) & (lane_q < dh * (p + 1))
        ms = jnp.sum(jnp.where(seg, xq2, 0.0), axis=-1, keepdims=True) * (1.0 / dh)
        c0 = LANES * (p // 2)
        half = xq[:, c0:c0 + LANES] * lax.rsqrt(ms + RMS_EPS) * qg_ref[:, c0:c0 + LANES]
        if p % 2 == 1:
            half = pltpu.roll(half, dh, 1)
        qz.append(jnp.where(klane, half, 0.0))

    def stacked_q(flag_lanes):
        return jnp.concatenate([(qp + flag_lanes).astype(BF16) for qp in qz], axis=0)

    ww = WINDOW + TQ
    s_w = _dot_nt(kw_ref[pl.ds(q0, ww), :], stacked_q(jnp.where(lane == PAD_LANE, NEG, 0.0)))
    s = jnp.concatenate([s_w[:, p * TQ:(p + 1) * TQ] + bw_ref[0, p] for p in range(hpg)], axis=1)
    m_win = jnp.max(s, axis=0, keepdims=True)
    acc_win = _dot(vwt_ref[:, pl.ds(q0, ww)], jnp.exp2(s - m_win).astype(BF16))

    per_tile = TQ // CMP_STRIDE
    off = pl.multiple_of((LANES - per_tile) - qt * per_tile, per_tile)
    blk_i = lax.broadcasted_iota(jnp.int32, (LANES, TQ), 0)
    valid_c = (CMP_STRIDE * blk_i + (CMP_LEN - 1)) <= (q0 + lax.broadcasted_iota(jnp.int32, (LANES, TQ), 1))
    kvc = kvc_ref[0, 0]
    s_c = _dot_nt(kvc, stacked_q(0.0))
    pcs = []
    for p in range(hpg):
        s = jnp.where(valid_c, s_c[:, p * TQ:(p + 1) * TQ] + bc_ref[0, p, pl.ds(off, LANES), :], NEG)
        m = jnp.max(s, axis=0, keepdims=True)
        e = jnp.where(valid_c, jnp.exp2(s - m), 0.0)
        l = jnp.sum(e, axis=0, keepdims=True)
        pcs.append(e / jnp.where(l > 0.0, l, 1.0))
    vc_t = jnp.where(ones_rows, 1.0, kvc.astype(F32).T).astype(BF16)
    o_cmp = _dot(vc_t, jnp.concatenate(pcs, axis=1).astype(BF16))
    psum = pcs[0]
    for p in range(1, hpg):
        psum = psum + pcs[p]
    p_hi = psum.astype(BF16)
    p_lo = (psum - p_hi.astype(F32)).astype(BF16)
    n_slc = t // SLC_LEN
    imp_t = (_dot(ovt_ref[...], p_hi) + _dot(ovt_ref[...], p_lo))[:n_slc, :]

    jrow = lax.broadcasted_iota(jnp.int32, (n_slc, TQ), 0)
    cur = (q0 + lax.broadcasted_iota(jnp.int32, (n_slc, TQ), 1)) // SLC_LEN
    forced = (jrow == 0) | (jrow == cur) | (jrow == cur - 1)
    vals = jnp.where(forced, SEL_BIG, jnp.where(jrow <= cur, imp_t, -SEL_BIG))
    terms = []
    for jp in range(n_slc):
        row = vals[jp:jp + 1, :]
        ahead = (row > vals) | ((row == vals) & (jrow > jp))
        terms.append(jnp.where(ahead, 1.0, 0.0))
    while len(terms) > 1:
        terms = [a + b for a, b in zip(terms[0::2], terms[1::2])]
    cnt = terms[0]

    far_end = (qt - 1) * TQ
    drop_nd = jnp.where(cnt < float(N_SEL), 0.0, NEG)
    drop_far = jnp.where(jrow * SLC_LEN < far_end, drop_nd, NEG)

    def flag_lanes(drop):
        rows = [jnp.zeros((FLAG0, TQ), F32), drop, jnp.full((PAD_LANE - FLAG0 - n_slc + 8, TQ), NEG, F32),
                jnp.zeros((LANES - PAD_LANE - 8, TQ), F32)]
        return jnp.concatenate(rows, axis=0).T

    ck = CK
    q_nd = stacked_q(flag_lanes(drop_nd))
    q_far = stacked_q(flag_lanes(drop_far))
    n_far = jnp.maximum(far_end + ck - 1, 0) // ck

    s_nd = _dot_nt(ks_ref[pl.ds(q0, 2 * TQ), :], q_nd)
    s_nd = jnp.concatenate([s_nd[:, p * TQ:(p + 1) * TQ] + bnd_ref[0, p] for p in range(hpg)], axis=1)

    def select_branch(k):
        scores = [s_nd]
        values = [vst_ref[:, pl.ds(q0, 2 * TQ)]]
        for c in range(k):
            scores.append(_dot_nt(ks_ref[TQ + c * ck:TQ + (c + 1) * ck, :], q_far))
            values.append(vst_ref[:, TQ + c * ck:TQ + (c + 1) * ck])
        m = jnp.max(scores[0], axis=0, keepdims=True)
        for s in scores[1:]:
            m = jnp.maximum(m, jnp.max(s, axis=0, keepdims=True))
        acc = _dot(values[0], jnp.exp2(scores[0] - m).astype(BF16))
        for s, v in zip(scores[1:], values[1:]):
            acc = acc + _dot(v, jnp.exp2(s - m).astype(BF16))
        sel_ref[...] = acc

    for k in range((t - 2 * TQ) // ck + 1):
        pl.when(n_far == k)(functools.partial(select_branch, k))
    acc_sel = sel_ref[...]

    gl_t = jax.nn.sigmoid(gate_ref[0].astype(F32)).T
    outs = []
    for p in range(hpg):
        sl = slice(p * TQ, (p + 1) * TQ)
        g_c = gl_t[p:p + 1, :]
        g_s = gl_t[hpg + p:hpg + p + 1, :] / acc_sel[0:1, sl]
        g_w = gl_t[2 * hpg + p:2 * hpg + p + 1, :] / acc_win[0:1, sl]
        outs.append(g_c * o_cmp[dh:, sl] + g_s * acc_sel[dh:, sl] + g_w * acc_win[dh:, sl])
    o_ref[0] = jnp.concatenate(outs, axis=0).T.astype(o_ref.dtype)


def _nsa_attention(h3, kvc, qgain, kgain, bias_nd, bias_w, bias_cmp, overlap_t, kflag):
    b, t, _ = h3.shape
    g, hpg = NSA_GROUPS, NSA_HPG
    n_blk = t // CMP_STRIDE
    qw = hpg * NSA_DH
    sel0, win0, gate0 = NSA_KV0 + g, NSA_KV0 + 2 * g, NSA_KV0 + 3 * g
    return pl.pallas_call(
        functools.partial(_nsa_attn_kernel, t=t),
        grid=(g, b, t // TQ),
        in_specs=[
            pl.BlockSpec((1, TQ, qw), lambda gi, bi, qi: (bi, qi, gi)),
            pl.BlockSpec((1, 1, n_blk, LANES), lambda gi, bi, qi: (bi, gi, 0, 0)),
            pl.BlockSpec((1, t, LANES), lambda gi, bi, qi: (bi, 0, sel0 + gi)),
            pl.BlockSpec((1, t, LANES), lambda gi, bi, qi: (bi, 0, win0 + gi)),
            pl.BlockSpec((1, TQ, LANES), lambda gi, bi, qi: (bi, qi, gate0 + gi)),
            pl.BlockSpec((1, qw), lambda gi, bi, qi: (0, 0)),
            pl.BlockSpec((2, LANES), lambda gi, bi, qi: (0, 0)),
            pl.BlockSpec((1, hpg, 2 * TQ, TQ), lambda gi, bi, qi: (gi, 0, 0, 0)),
            pl.BlockSpec((1, hpg, WINDOW + TQ, TQ), lambda gi, bi, qi: (gi, 0, 0, 0)),
            pl.BlockSpec((1, hpg, 2 * LANES, TQ), lambda gi, bi, qi: (gi, 0, 0, 0)),
            pl.BlockSpec((LANES, LANES), lambda gi, bi, qi: (0, 0)),
            pl.BlockSpec((t, LANES), lambda gi, bi, qi: (0, 0)),
        ],
        out_specs=pl.BlockSpec((1, TQ, qw), lambda gi, bi, qi: (bi, qi, gi)),
        out_shape=jax.ShapeDtypeStruct((b, t, NSA_HEADS * NSA_DH), BF16),
        scratch_shapes=[pltpu.VMEM((TQ + t, LANES), BF16), pltpu.VMEM((WINDOW + t, LANES), BF16),
                        pltpu.VMEM((LANES, TQ + t), BF16), pltpu.VMEM((LANES, WINDOW + t), BF16),
                        pltpu.VMEM((LANES, hpg * TQ), F32)],
        compiler_params=_cparams(("arbitrary", "arbitrary", "arbitrary")),
    )(h3, kvc, h3, h3, h3, qgain, kgain, bias_nd, bias_w, bias_cmp, overlap_t, kflag)


def _bias_kernel(tbl_ref, nd_ref, w_ref, bc_ref, ond_ref, ow_ref, obc_ref):
    h = pl.program_id(0)
    far = tbl_ref[NUM_BUCKETS - 1, h]
    nd = nd_ref[...]
    w = w_ref[...]
    bc = bc_ref[...]
    o_nd = jnp.full(nd.shape, NEG, F32)
    o_w = jnp.full(w.shape, NEG, F32)
    o_bc = jnp.zeros(bc.shape, F32)
    for bucket in range(NUM_BUCKETS):
        v = tbl_ref[bucket, h]
        o_nd = jnp.where(nd == bucket, (v - far) * LOG2E, o_nd)
        o_w = jnp.where(w == bucket, (v - far) * LOG2E, o_w)
        o_bc = jnp.where(bc == bucket, v * LOG2E, o_bc)
    ond_ref[0] = o_nd
    ow_ref[0] = o_w
    obc_ref[0] = o_bc


def _bias_tiles(rel_bias, nd_bucket, w_bucket, bc_bucket):
    n_heads = rel_bias.shape[1]
    tables = (nd_bucket, w_bucket, bc_bucket)
    return pl.pallas_call(
        _bias_kernel,
        grid=(n_heads,),
        in_specs=[pl.BlockSpec(memory_space=pltpu.SMEM)] + [_resident(a.shape) for a in tables],
        out_specs=[pl.BlockSpec((1,) + a.shape, lambda h: (h, 0, 0)) for a in tables],
        out_shape=[jax.ShapeDtypeStruct((n_heads,) + a.shape, F32) for a in tables],
        compiler_params=_cparams(("parallel",)),
    )(rel_bias.astype(F32), *[jnp.asarray(a, jnp.int32) for a in tables])


def _t5_bucket_np(dist):
    n = np.maximum(dist, 0)
    max_exact = NUM_BUCKETS // 2
    nf = np.maximum(n, max_exact).astype(np.float32)
    large = max_exact + (np.log(nf / max_exact) / np.float32(np.log(MAX_DISTANCE / max_exact))
                         * (NUM_BUCKETS - max_exact)).astype(np.int32)
    large = np.minimum(large, NUM_BUCKETS - 1)
    return np.where(n < max_exact, n, large)


def _nsa_constants(t):
    n_cmp = (t - CMP_LEN) // CMP_STRIDE + 1
    n_slc = t // SLC_LEN
    n_blk = t // CMP_STRIDE
    per_tile = TQ // CMP_STRIDE
    assert n_blk <= LANES and per_tile * (t // TQ - 1) <= LANES - per_tile
    iq = np.arange(TQ)
    rel = iq[None, :] - iq[:, None]
    assert _t5_bucket_np(np.array([TQ + 1]))[0] == NUM_BUCKETS - 1
    dist_nd = np.concatenate([TQ + rel, rel], axis=0)
    nd_bucket = np.where(dist_nd >= 0, _t5_bucket_np(dist_nd), -1)
    dist_w = np.arange(TQ)[None, :] + WINDOW - np.arange(WINDOW + TQ)[:, None]
    w_bucket = np.where((dist_w >= 0) & (dist_w < WINDOW), _t5_bucket_np(dist_w), -1)
    blk_off = np.arange(LANES) - (LANES - per_tile)
    bc_bucket = _t5_bucket_np(iq[None, :] - CMP_STRIDE * blk_off[:, None] - (CMP_LEN - 1))
    bc_bucket = np.concatenate([bc_bucket, bc_bucket], axis=0)
    ci = np.arange(n_blk) * CMP_STRIDE
    sj = np.arange(n_slc) * SLC_LEN
    ov = ((ci[:, None] < sj[None, :] + SLC_LEN) & (ci[:, None] + CMP_LEN > sj[None, :])).astype(np.float32)
    ov[n_cmp:, :] = 0.0
    overlap_t = np.zeros((LANES, LANES), np.float32)
    overlap_t[:n_slc, :n_blk] = ov.T
    assert FLAG0 + n_slc <= PAD_LANE
    kflag = np.zeros((t, LANES), np.float32)
    kflag[np.arange(t), FLAG0 + np.arange(t) // SLC_LEN] = 1.0
    return nd_bucket, w_bucket, bc_bucket, overlap_t, kflag


def _nsa_w_in_columns():
    kv0 = NSA_HEADS * NSA_DH
    kvw = NSA_GROUPS * NSA_DH
    cols = list(range(kv0))
    for br in range(3):
        for g in range(NSA_GROUPS):
            k_src = kv0 + (2 * br) * kvw + g * NSA_DH
            v_src = kv0 + (2 * br + 1) * kvw + g * NSA_DH
            cols += list(range(k_src, k_src + NSA_DH)) + list(range(v_src, v_src + NSA_DH))
    gate0 = kv0 + 6 * kvw
    for g in range(NSA_GROUPS):
        blk = [-1] * LANES
        for br in range(3):
            for p in range(NSA_HPG):
                blk[br * NSA_HPG + p] = gate0 + br * NSA_HEADS + g * NSA_HPG + p
        cols += blk
    return np.asarray(cols, np.int32)


def _retention_layer(x2, b, t, norm_g, w_in, w_out):
    h = _norm_matmul(x2, norm_g, w_in.astype(BF16), tn=1024)
    o = _retention_core(h.reshape(b, t, -1))
    return o.reshape(b * t, -1), w_out.astype(BF16)


def _nsa_layer(x2, b, t, norm_g, w_in, w_out, q_gain, k_gain, cmp_pos, cmp_w1, cmp_w2, bias_nd, bias_w,
               bias_cmp, overlap_t, kflag):
    cols = _nsa_w_in_columns()
    w_k = jnp.where(cols[None, :] >= 0, jnp.take(w_in, np.maximum(cols, 0), axis=1), 0.0).astype(BF16)
    h3 = _norm_matmul(x2, norm_g, w_k, tn=1024).reshape(b, t, NSA_COLS)

    dh = NSA_DH
    w1 = cmp_w1.reshape(2, CMP_LEN, dh, CMP_HID)
    zero1 = jnp.zeros((CMP_LEN, dh, CMP_HID), F32)
    w1cat = jnp.concatenate([jnp.concatenate([w1[0], zero1], axis=-1),
                             jnp.concatenate([zero1, w1[1]], axis=-1)], axis=1).astype(BF16)
    zero2 = jnp.zeros((CMP_HID, dh), F32)
    w2cat = jnp.concatenate([jnp.concatenate([cmp_w2[0], zero2], axis=-1),
                             jnp.concatenate([zero2, cmp_w2[1]], axis=-1)], axis=0).astype(BF16)
    pos = jnp.concatenate([cmp_pos[0], cmp_pos[1]], axis=-1)
    ones = jnp.ones((dh,), F32)
    kvc = _compress(h3, pos, w1cat, w2cat, jnp.concatenate([k_gain[0], ones])[None, :])

    qgain = jnp.tile(q_gain * (dh ** -0.5 * LOG2E), NSA_HPG)[None, :]
    kgain = jnp.stack([jnp.concatenate([k_gain[1], ones]), jnp.concatenate([k_gain[2], ones])])
    o = _nsa_attention(h3, kvc, qgain, kgain, bias_nd, bias_w, bias_cmp, overlap_t, kflag)
    return o.reshape(b * t, -1), w_out.astype(BF16)


def kernel(x, ffn1_norm, ffn1_w_gu, ffn1_w_down, mix_norm, ffn2_norm, ffn2_w_gu, ffn2_w_down,
           ret_w_in, ret_w_out, nsa_w_in, nsa_w_out, nsa_q_gain, nsa_k_gain,
           nsa_cmp_pos, nsa_cmp_w1, nsa_cmp_w2, rel_bias):
    b, t, d = x.shape
    depth = ffn1_norm.shape[0]

    def ffn_weights(w_gu, w_down):
        return w_gu.astype(BF16), w_down.astype(BF16)

    nd_bucket, w_bucket, bc_bucket, overlap_t, kflag = _nsa_constants(t)
    bias_nd, bias_w, bias_cmp = _bias_tiles(rel_bias, nd_bucket, w_bucket, bc_bucket)
    bias_nd = bias_nd.reshape(NSA_GROUPS, NSA_HPG, 2 * TQ, TQ)
    bias_w = bias_w.reshape(NSA_GROUPS, NSA_HPG, WINDOW + TQ, TQ)
    bias_cmp = bias_cmp.reshape(NSA_GROUPS, NSA_HPG, 2 * LANES, TQ)
    overlap_t = jnp.asarray(overlap_t, BF16)
    kflag = jnp.asarray(kflag, BF16)

    x2 = x.reshape(b * t, d)
    for layer in range(depth):
        x2 = _ffn(x2, ffn1_norm[layer][None, :], *ffn_weights(ffn1_w_gu[layer], ffn1_w_down[layer]))
        j = layer // 2
        if layer % 2 == 0:
            o2, w_out = _retention_layer(x2, b, t, mix_norm[layer][None, :], ret_w_in[j], ret_w_out[j])
        else:
            o2, w_out = _nsa_layer(x2, b, t, mix_norm[layer][None, :], nsa_w_in[j], nsa_w_out[j], nsa_q_gain[j],
                                   nsa_k_gain[j], nsa_cmp_pos[j], nsa_cmp_w1[j], nsa_cmp_w2[j], bias_nd, bias_w,
                                   bias_cmp, overlap_t, kflag)
        x2 = _proj_ffn(o2, w_out, x2, ffn2_norm[layer][None, :],
                       *ffn_weights(ffn2_w_gu[layer], ffn2_w_down[layer]))
    return x2.reshape(b, t, d)
```

```python
import functools

import numpy as np
import jax
import jax.numpy as jnp
from jax import lax
from jax.experimental import pallas as pl
from jax.experimental.pallas import tpu as pltpu

F32 = jnp.float32
BF16 = jnp.bfloat16

RMS_EPS = 1e-6
NEG = -1e30
SEL_BIG = 1e9
LOG2E = 1.4426950408889634

LANES = 128
VMEM_LIMIT = 56 * 1024 * 1024

RET_HEADS = 4
RET_DK = 256
RET_DV = 512
RET_CHUNK = 128
RET_ROPE_BASE = 10000.0

NSA_HEADS = 16
NSA_GROUPS = 4
NSA_HPG = 4
NSA_DH = 64
CMP_LEN = 32
CMP_STRIDE = 16
CMP_HID = 256
SLC_LEN = 64
N_SEL = 8
WINDOW = 512
NUM_BUCKETS = 32
MAX_DISTANCE = 128
TQ = 256
CK = 512
FLAG0 = 64
PAD_LANE = 96
NSA_COLS = 1024 + 3 * 4 * 128 + 4 * 128
NSA_KV0 = 1024 // 128


def _dot(a, b):
    return jnp.dot(a, b, preferred_element_type=F32)


def _dot_nt(a, b):
    return lax.dot_general(a, b, (((1,), (1,)), ((), ())), preferred_element_type=F32)


def _dot_tn(a, b):
    return lax.dot_general(a, b, (((0,), (0,)), ((), ())), preferred_element_type=F32)


def _rms(x, g):
    ms = jnp.mean(x * x, axis=-1, keepdims=True)
    return x * lax.rsqrt(ms + RMS_EPS) * g


def _cparams(sem):
    return pltpu.CompilerParams(dimension_semantics=sem, vmem_limit_bytes=VMEM_LIMIT)


def _resident(shape):
    n = len(shape)
    return pl.BlockSpec(shape, lambda *_: (0,) * n, pipeline_mode=pl.Buffered(1))


def _ffn_kernel(x_ref, g_ref, wgu_ref, wd_ref, o_ref):
    d_ff = wd_ref.shape[0]
    x = x_ref[...]
    xn = _rms(x, g_ref[...]).astype(BF16)
    a = _dot(xn, wgu_ref[:, :d_ff])
    b = _dot(xn, wgu_ref[:, d_ff:])
    h = (a * jax.nn.sigmoid(a) * b).astype(BF16)
    o_ref[...] = x + 0.5 * _dot(h, wd_ref[...])


def _ffn(x2, g, wgu, wd, tm=512):
    n, d = x2.shape
    return pl.pallas_call(
        _ffn_kernel,
        grid=(n // tm,),
        in_specs=[
            pl.BlockSpec((tm, d), lambda i: (i, 0)),
            _resident((1, d)),
            _resident(wgu.shape),
            _resident(wd.shape),
        ],
        out_specs=pl.BlockSpec((tm, d), lambda i: (i, 0)),
        out_shape=jax.ShapeDtypeStruct((n, d), F32),
        compiler_params=_cparams(("parallel",)),
    )(x2, g, wgu, wd)


def _proj_ffn_kernel(o_ref, wo_ref, x_ref, g_ref, wgu_ref, wd_ref, y_ref):
    d_ff = wd_ref.shape[0]
    x = x_ref[...] + _dot(o_ref[...], wo_ref[...])
    xn = _rms(x, g_ref[...]).astype(BF16)
    a = _dot(xn, wgu_ref[:, :d_ff])
    b = _dot(xn, wgu_ref[:, d_ff:])
    h = (a * jax.nn.sigmoid(a) * b).astype(BF16)
    y_ref[...] = x + 0.5 * _dot(h, wd_ref[...])


def _proj_ffn(o2, wo, x2, g, wgu, wd, tm=512):
    n, d = x2.shape
    k = o2.shape[1]
    return pl.pallas_call(
        _proj_ffn_kernel,
        grid=(n // tm,),
        in_specs=[
            pl.BlockSpec((tm, k), lambda i: (i, 0)),
            _resident(wo.shape),
            pl.BlockSpec((tm, d), lambda i: (i, 0)),
            _resident((1, d)),
            _resident(wgu.shape),
            _resident(wd.shape),
        ],
        out_specs=pl.BlockSpec((tm, d), lambda i: (i, 0)),
        out_shape=jax.ShapeDtypeStruct((n, d), F32),
        compiler_params=_cparams(("parallel",)),
    )(o2, wo, x2, g, wgu, wd)


def _norm_matmul_kernel(x_ref, g_ref, w_ref, o_ref, *, tn):
    xn = _rms(x_ref[...], g_ref[...]).astype(BF16)
    for j in range(w_ref.shape[1] // tn):
        o_ref[:, j * tn:(j + 1) * tn] = _dot(xn, w_ref[:, j * tn:(j + 1) * tn]).astype(o_ref.dtype)


def _norm_matmul(x2, g, w, tn, tm=512):
    n, d = x2.shape
    nout = w.shape[1]
    return pl.pallas_call(
        functools.partial(_norm_matmul_kernel, tn=tn),
        grid=(n // tm,),
        in_specs=[pl.BlockSpec((tm, d), lambda i: (i, 0)), _resident((1, d)), _resident((d, nout))],
        out_specs=pl.BlockSpec((tm, nout), lambda i: (i, 0)),
        out_shape=jax.ShapeDtypeStruct((n, nout), BF16),
        compiler_params=_cparams(("parallel",)),
    )(x2, g, w)


def _ret_kernel(cd_ref, q_ref, k_ref, v_ref, g_ref, cos_ref, sin_ref, intra_ref, qd_ref, kd_ref,
                o_ref, s_ref, *, n_chunks):
    c_len = RET_CHUNK
    half = RET_DK // 2
    cd = cd_ref[pl.program_id(1)]
    intra = intra_ref[0]
    qd = qd_ref[0]
    kd = kd_ref[0]
    s_ref[...] = jnp.zeros_like(s_ref)

    def chunk(c, carry):
        r0 = pl.multiple_of(c * c_len, c_len)
        cos = cos_ref[pl.ds(r0, c_len), :]
        sin = sin_ref[pl.ds(r0, c_len), :]

        def rot(x):
            x1, x2 = x[:, :half], x[:, half:]
            return jnp.concatenate([x1 * cos - x2 * sin, x1 * sin + x2 * cos], axis=-1)

        qr = rot(q_ref[0, pl.ds(r0, c_len), :].astype(F32))
        kr = rot(k_ref[0, pl.ds(r0, c_len), :].astype(F32)) * (RET_DK ** -0.5)
        vc = v_ref[0, pl.ds(r0, c_len), :]
        s = _dot_nt(qr.astype(BF16), kr.astype(BF16)) * intra
        state = s_ref[...]
        o = _dot(s.astype(BF16), vc) + _dot((qr * qd).astype(BF16), state.astype(BF16))
        s_ref[...] = cd * state + _dot_tn((kr * kd).astype(BF16), vc)
        o = o * lax.rsqrt(jnp.mean(o * o, axis=-1, keepdims=True) + RMS_EPS)
        gate = g_ref[0, pl.ds(r0, c_len), :].astype(F32)
        o_ref[0, pl.ds(r0, c_len), :] = (gate * jax.nn.sigmoid(gate) * o).astype(o_ref.dtype)
        return carry

    lax.fori_loop(0, n_chunks, chunk, 0, unroll=8)


def _retention_core(h3):
    b, t, _ = h3.shape
    hh, dk, dv, c = RET_HEADS, RET_DK, RET_DV, RET_CHUNK
    half = dk // 2
    pos = jnp.arange(t, dtype=F32)
    inv_freq = RET_ROPE_BASE ** (-jnp.arange(half, dtype=F32) / half)
    ang = pos[:, None] * inv_freq[None, :]
    cos, sin = jnp.cos(ang), jnp.sin(ang)
    log_gamma = jnp.log(1.0 - 2.0 ** (-5.0 - jnp.arange(hh, dtype=F32)))
    i = jnp.arange(c)
    diff = i[:, None] - i[None, :]
    intra = jnp.where(diff >= 0, jnp.exp(log_gamma[:, None, None] * jnp.maximum(diff, 0)), 0.0)
    q_dec = jnp.exp(log_gamma[:, None] * (i + 1))[:, :, None]
    k_dec = jnp.exp(log_gamma[:, None] * (c - 1 - i))[:, :, None]
    chunk_dec = jnp.exp(log_gamma * c)
    k0 = hh
    v0 = 2 * hh * dk // dv
    g0 = v0 + hh
    return pl.pallas_call(
        functools.partial(_ret_kernel, n_chunks=t // c),
        grid=(b, hh),
        in_specs=[
            pl.BlockSpec(memory_space=pltpu.SMEM),
            pl.BlockSpec((1, t, dk), lambda bi, hi: (bi, 0, hi)),
            pl.BlockSpec((1, t, dk), lambda bi, hi: (bi, 0, k0 + hi)),
            pl.BlockSpec((1, t, dv), lambda bi, hi: (bi, 0, v0 + hi)),
            pl.BlockSpec((1, t, dv), lambda bi, hi: (bi, 0, g0 + hi)),
            _resident((t, half)),
            _resident((t, half)),
            pl.BlockSpec((1, c, c), lambda bi, hi: (hi, 0, 0)),
            pl.BlockSpec((1, c, 1), lambda bi, hi: (hi, 0, 0)),
            pl.BlockSpec((1, c, 1), lambda bi, hi: (hi, 0, 0)),
        ],
        out_specs=pl.BlockSpec((1, t, dv), lambda bi, hi: (bi, 0, hi)),
        out_shape=jax.ShapeDtypeStruct((b, t, hh * dv), BF16),
        scratch_shapes=[pltpu.VMEM((dk, dv), F32)],
        compiler_params=_cparams(("parallel", "arbitrary")),
    )(chunk_dec, h3, h3, h3, h3, cos, sin, intra, q_dec, k_dec)


def _k_headnorm(x, gain_row):
    klane = lax.broadcasted_iota(jnp.int32, x.shape, 1) < NSA_DH
    ms = jnp.sum(jnp.where(klane, x * x, 0.0), axis=-1, keepdims=True) * (1.0 / NSA_DH)
    return jnp.where(klane, x * lax.rsqrt(ms + RMS_EPS) * gain_row, x)


def _cmp_kernel(a_ref, pos_ref, w1_ref, w2_ref, kg_ref, o_ref, af_ref, *, t):
    n_blk = t // CMP_STRIDE
    af_ref[0:t, :] = a_ref[0].astype(F32)
    af_ref[t:t + CMP_LEN, :] = jnp.zeros((CMP_LEN, LANES), F32)
    blocks = [(af_ref[pl.ds(l, n_blk, stride=CMP_STRIDE), :] + pos_ref[l:l + 1, :]).astype(BF16)
              for l in range(CMP_LEN)]
    hid = jax.nn.gelu(_dot(jnp.concatenate(blocks, axis=1), w1_ref[...]), approximate=True)
    out = _dot(hid.astype(BF16), w2_ref[...])
    o_ref[0, 0] = _k_headnorm(out, kg_ref[...]).astype(o_ref.dtype)


def _compress(h3, pos, w1cat, w2cat, kgain_row):
    b, t, _ = h3.shape
    n_blk = t // CMP_STRIDE
    return pl.pallas_call(
        functools.partial(_cmp_kernel, t=t),
        grid=(b, NSA_GROUPS),
        in_specs=[
            pl.BlockSpec((1, t, LANES), lambda bi, gi: (bi, 0, NSA_KV0 + gi)),
            _resident((CMP_LEN, LANES)),
            _resident((CMP_LEN * LANES, 2 * CMP_HID)),
            _resident((2 * CMP_HID, LANES)),
            _resident((1, LANES)),
        ],
        out_specs=pl.BlockSpec((1, 1, n_blk, LANES), lambda bi, gi: (bi, gi, 0, 0)),
        out_shape=jax.ShapeDtypeStruct((b, NSA_GROUPS, n_blk, LANES), BF16),
        scratch_shapes=[pltpu.VMEM((t + CMP_LEN, LANES), F32)],
        compiler_params=_cparams(("parallel", "parallel")),
    )(h3, pos, w1cat, w2cat, kgain_row)


def _nsa_attn_kernel(q_ref, kvc_ref, kvs_ref, kvw_ref, gate_ref, qg_ref, kg_ref, bnd_ref, bw_ref, bc_ref,
                     ovt_ref, kflag_ref, o_ref, ks_ref, kw_ref, vst_ref, vwt_ref, sel_ref, *, t):
    hpg = NSA_HPG
    dh = NSA_DH
    qt = pl.program_id(2)
    q0 = pl.multiple_of(qt * TQ, TQ)
    lane = lax.broadcasted_iota(jnp.int32, (1, LANES), 1)
    klane = lane < dh
    ones_rows = lax.broadcasted_iota(jnp.int32, (LANES, LANES), 0) < dh

    @pl.when(qt == 0)
    def _():
        pad_rows = jnp.where(lane == PAD_LANE, 1.0, 0.0).astype(BF16)
        ks_ref[0:TQ, :] = jnp.broadcast_to(pad_rows, (TQ, LANES))
        kw_ref[0:WINDOW, :] = jnp.broadcast_to(pad_rows, (WINDOW, LANES))
        pad_col = jnp.where(ones_rows, 1.0, 0.0).astype(BF16)
        for i in range(TQ // LANES):
            vst_ref[:, i * LANES:(i + 1) * LANES] = pad_col
        for i in range(WINDOW // LANES):
            vwt_ref[:, i * LANES:(i + 1) * LANES] = pad_col

        def norm_block(c, carry):
            r0 = pl.multiple_of(c * LANES, LANES)
            for src, dst, dst_t, pad, gi in ((kvs_ref, ks_ref, vst_ref, TQ, 0), (kvw_ref, kw_ref, vwt_ref, WINDOW, 1)):
                x = _k_headnorm(src[0, pl.ds(r0, LANES), :].astype(F32), kg_ref[gi:gi + 1, :])
                flags = kflag_ref[pl.ds(r0, LANES), :] if gi == 0 else jnp.zeros((LANES, LANES), BF16)
                dst[pl.ds(pad + r0, LANES), :] = jnp.where(klane, x.astype(BF16), flags)
                dst_t[:, pl.ds(pad + r0, LANES)] = jnp.where(ones_rows, 1.0, x.T).astype(BF16)
            return carry

        lax.fori_loop(0, t // LANES, norm_block, 0, unroll=4)

    xq = q_ref[0].astype(F32)
    lane_q = lax.broadcasted_iota(jnp.int32, (1, hpg * dh), 1)
    xq2 = xq * xq
    qz = []
    for p in range(hpg):
        seg = (lane_q >= dh * p) & (lane_q < dh * (p + 1))
        ms = jnp.sum(jnp.where(seg, xq2, 0.0), axis=-1, keepdims=True) * (1.0 / dh)
        c0 = LANES * (p // 2)
        half = xq[:, c0:c0 + LANES] * lax.rsqrt(ms + RMS_EPS) * qg_ref[:, c0:c0 + LANES]
        if p % 2 == 1:
            half = pltpu.roll(half, dh, 1)
        qz.append(jnp.where(klane, half, 0.0))

    def stacked_q(flag_lanes):
        return jnp.concatenate([(qp + flag_lanes).astype(BF16) for qp in qz], axis=0)

    ww = WINDOW + TQ
    s_w = _dot_nt(kw_ref[pl.ds(q0, ww), :], stacked_q(jnp.where(lane == PAD_LANE, NEG, 0.0)))
    s = jnp.concatenate([s_w[:, p * TQ:(p + 1) * TQ] + bw_ref[0, p] for p in range(hpg)], axis=1)
    m_win = jnp.max(s, axis=0, keepdims=True)
    acc_win = _dot(vwt_ref[:, pl.ds(q0, ww)], jnp.exp2(s - m_win).astype(BF16))

    per_tile = TQ // CMP_STRIDE
    off = pl.multiple_of((LANES - per_tile) - qt * per_tile, per_tile)
    blk_i = lax.broadcasted_iota(jnp.int32, (LANES, TQ), 0)
    valid_c = (CMP_STRIDE * blk_i + (CMP_LEN - 1)) <= (q0 + lax.broadcasted_iota(jnp.int32, (LANES, TQ), 1))
    kvc = kvc_ref[0, 0]
    s_c = _dot_nt(kvc, stacked_q(0.0))
    pcs = []
    for p in range(hpg):
        s = jnp.where(valid_c, s_c[:, p * TQ:(p + 1) * TQ] + bc_ref[0, p, pl.ds(off, LANES), :], NEG)
        m = jnp.max(s, axis=0, keepdims=True)
        e = jnp.where(valid_c, jnp.exp2(s - m), 0.0)
        l = jnp.sum(e, axis=0, keepdims=True)
        pcs.append(e / jnp.where(l > 0.0, l, 1.0))
    vc_t = jnp.where(ones_rows, 1.0, kvc.astype(F32).T).astype(BF16)
    o_cmp = _dot(vc_t, jnp.concatenate(pcs, axis=1).astype(BF16))
    psum = pcs[0]
    for p in range(1, hpg):
        psum = psum + pcs[p]
    p_hi = psum.astype(BF16)
    p_lo = (psum - p_hi.astype(F32)).astype(BF16)
    n_slc = t // SLC_LEN
    imp_t = (_dot(ovt_ref[...], p_hi) + _dot(ovt_ref[...], p_lo))[:n_slc, :]

    jrow = lax.broadcasted_iota(jnp.int32, (n_slc, TQ), 0)
    cur = (q0 + lax.broadcasted_iota(jnp.int32, (n_slc, TQ), 1)) // SLC_LEN
    forced = (jrow == 0) | (jrow == cur) | (jrow == cur - 1)
    vals = jnp.where(forced, SEL_BIG, jnp.where(jrow <= cur, imp_t, -SEL_BIG))
    terms = []
    for jp in range(n_slc):
        row = vals[jp:jp + 1, :]
        ahead = (row > vals) | ((row == vals) & (jrow > jp))
        terms.append(jnp.where(ahead, 1.0, 0.0))
    while len(terms) > 1:
        terms = [a + b for a, b in zip(terms[0::2], terms[1::2])]
    cnt = terms[0]

    far_end = (qt - 1) * TQ
    drop_nd = jnp.where(cnt < float(N_SEL), 0.0, NEG)
    drop_far = jnp.where(jrow * SLC_LEN < far_end, drop_nd, NEG)

    def flag_lanes(drop):
        rows = [jnp.zeros((FLAG0, TQ), F32), drop, jnp.full((PAD_LANE - FLAG0 - n_slc + 8, TQ), NEG, F32),
                jnp.zeros((LANES - PAD_LANE - 8, TQ), F32)]
        return jnp.concatenate(rows, axis=0).T

    ck = CK
    q_nd = stacked_q(flag_lanes(drop_nd))
    q_far = stacked_q(flag_lanes(drop_far))
    n_far = jnp.maximum(far_end + ck - 1, 0) // ck

    s_nd = _dot_nt(ks_ref[pl.ds(q0, 2 * TQ), :], q_nd)
    s_nd = jnp.concatenate([s_nd[:, p * TQ:(p + 1) * TQ] + bnd_ref[0, p] for p in range(hpg)], axis=1)

    def select_branch(k):
        scores = [s_nd]
        values = [vst_ref[:, pl.ds(q0, 2 * TQ)]]
        for c in range(k):
            scores.append(_dot_nt(ks_ref[TQ + c * ck:TQ + (c + 1) * ck, :], q_far))
            values.append(vst_ref[:, TQ + c * ck:TQ + (c + 1) * ck])
        m = jnp.max(scores[0], axis=0, keepdims=True)
        for s in scores[1:]:
            m = jnp.maximum(m, jnp.max(s, axis=0, keepdims=True))
        acc = _dot(values[0], jnp.exp2(scores[0] - m).astype(BF16))
        for s, v in zip(scores[1:], values[1:]):
            acc = acc + _dot(v, jnp.exp2(s - m).astype(BF16))
        sel_ref[...] = acc

    for k in range((t - 2 * TQ) // ck + 1):
        pl.when(n_far == k)(functools.partial(select_branch, k))
    acc_sel = sel_ref[...]

    gl_t = jax.nn.sigmoid(gate_ref[0].astype(F32)).T
    outs = []
    for p in range(hpg):
        sl = slice(p * TQ, (p + 1) * TQ)
        g_c = gl_t[p:p + 1, :]
        g_s = gl_t[hpg + p:hpg + p + 1, :] / acc_sel[0:1, sl]
        g_w = gl_t[2 * hpg + p:2 * hpg + p + 1, :] / acc_win[0:1, sl]
        outs.append(g_c * o_cmp[dh:, sl] + g_s * acc_sel[dh:, sl] + g_w * acc_win[dh:, sl])
    o_ref[0] = jnp.concatenate(outs, axis=0).T.astype(o_ref.dtype)


def _nsa_attention(h3, kvc, qgain, kgain, bias_nd, bias_w, bias_cmp, overlap_t, kflag):
    b, t, _ = h3.shape
    g, hpg = NSA_GROUPS, NSA_HPG
    n_blk = t // CMP_STRIDE
    qw = hpg * NSA_DH
    sel0, win0, gate0 = NSA_KV0 + g, NSA_KV0 + 2 * g, NSA_KV0 + 3 * g
    return pl.pallas_call(
        functools.partial(_nsa_attn_kernel, t=t),
        grid=(g, b, t // TQ),
        in_specs=[
            pl.BlockSpec((1, TQ, qw), lambda gi, bi, qi: (bi, qi, gi)),
            pl.BlockSpec((1, 1, n_blk, LANES), lambda gi, bi, qi: (bi, gi, 0, 0)),
            pl.BlockSpec((1, t, LANES), lambda gi, bi, qi: (bi, 0, sel0 + gi)),
            pl.BlockSpec((1, t, LANES), lambda gi, bi, qi: (bi, 0, win0 + gi)),
            pl.BlockSpec((1, TQ, LANES), lambda gi, bi, qi: (bi, qi, gate0 + gi)),
            pl.BlockSpec((1, qw), lambda gi, bi, qi: (0, 0)),
            pl.BlockSpec((2, LANES), lambda gi, bi, qi: (0, 0)),
            pl.BlockSpec((1, hpg, 2 * TQ, TQ), lambda gi, bi, qi: (gi, 0, 0, 0)),
            pl.BlockSpec((1, hpg, WINDOW + TQ, TQ), lambda gi, bi, qi: (gi, 0, 0, 0)),
            pl.BlockSpec((1, hpg, 2 * LANES, TQ), lambda gi, bi, qi: (gi, 0, 0, 0)),
            pl.BlockSpec((LANES, LANES), lambda gi, bi, qi: (0, 0)),
            pl.BlockSpec((t, LANES), lambda gi, bi, qi: (0, 0)),
        ],
        out_specs=pl.BlockSpec((1, TQ, qw), lambda gi, bi, qi: (bi, qi, gi)),
        out_shape=jax.ShapeDtypeStruct((b, t, NSA_HEADS * NSA_DH), BF16),
        scratch_shapes=[pltpu.VMEM((TQ + t, LANES), BF16), pltpu.VMEM((WINDOW + t, LANES), BF16),
                        pltpu.VMEM((LANES, TQ + t), BF16), pltpu.VMEM((LANES, WINDOW + t), BF16),
                        pltpu.VMEM((LANES, hpg * TQ), F32)],
        compiler_params=_cparams(("arbitrary", "arbitrary", "arbitrary")),
    )(h3, kvc, h3, h3, h3, qgain, kgain, bias_nd, bias_w, bias_cmp, overlap_t, kflag)


def _bias_kernel(tbl_ref, nd_ref, w_ref, bc_ref, ond_ref, ow_ref, obc_ref):
    h = pl.program_id(0)
    far = tbl_ref[NUM_BUCKETS - 1, h]
    nd = nd_ref[...]
    w = w_ref[...]
    bc = bc_ref[...]
    o_nd = jnp.full(nd.shape, NEG, F32)
    o_w = jnp.full(w.shape, NEG, F32)
    o_bc = jnp.zeros(bc.shape, F32)
    for bucket in range(NUM_BUCKETS):
        v = tbl_ref[bucket, h]
        o_nd = jnp.where(nd == bucket, (v - far) * LOG2E, o_nd)
        o_w = jnp.where(w == bucket, (v - far) * LOG2E, o_w)
        o_bc = jnp.where(bc == bucket, v * LOG2E, o_bc)
    ond_ref[0] = o_nd
    ow_ref[0] = o_w
    obc_ref[0] = o_bc


def _bias_tiles(rel_bias, nd_bucket, w_bucket, bc_bucket):
    n_heads = rel_bias.shape[1]
    tables = (nd_bucket, w_bucket, bc_bucket)
    return pl.pallas_call(
        _bias_kernel,
        grid=(n_heads,),
        in_specs=[pl.BlockSpec(memory_space=pltpu.SMEM)] + [_resident(a.shape) for a in tables],
        out_specs=[pl.BlockSpec((1,) + a.shape, lambda h: (h, 0, 0)) for a in tables],
        out_shape=[jax.ShapeDtypeStruct((n_heads,) + a.shape, F32) for a in tables],
        compiler_params=_cparams(("parallel",)),
    )(rel_bias.astype(F32), *[jnp.asarray(a, jnp.int32) for a in tables])


def _t5_bucket_np(dist):
    n = np.maximum(dist, 0)
    max_exact = NUM_BUCKETS // 2
    nf = np.maximum(n, max_exact).astype(np.float32)
    large = max_exact + (np.log(nf / max_exact) / np.float32(np.log(MAX_DISTANCE / max_exact))
                         * (NUM_BUCKETS - max_exact)).astype(np.int32)
    large = np.minimum(large, NUM_BUCKETS - 1)
    return np.where(n < max_exact, n, large)


def _nsa_constants(t):
    n_cmp = (t - CMP_LEN) // CMP_STRIDE + 1
    n_slc = t // SLC_LEN
    n_blk = t // CMP_STRIDE
    per_tile = TQ // CMP_STRIDE
    assert n_blk <= LANES and per_tile * (t // TQ - 1) <= LANES - per_tile
    iq = np.arange(TQ)
    rel = iq[None, :] - iq[:, None]
    assert _t5_bucket_np(np.array([TQ + 1]))[0] == NUM_BUCKETS - 1
    dist_nd = np.concatenate([TQ + rel, rel], axis=0)
    nd_bucket = np.where(dist_nd >= 0, _t5_bucket_np(dist_nd), -1)
    dist_w = np.arange(TQ)[None, :] + WINDOW - np.arange(WINDOW + TQ)[:, None]
    w_bucket = np.where((dist_w >= 0) & (dist_w < WINDOW), _t5_bucket_np(dist_w), -1)
    blk_off = np.arange(LANES) - (LANES - per_tile)
    bc_bucket = _t5_bucket_np(iq[None, :] - CMP_STRIDE * blk_off[:, None] - (CMP_LEN - 1))
    bc_bucket = np.concatenate([bc_bucket, bc_bucket], axis=0)
    ci = np.arange(n_blk) * CMP_STRIDE
    sj = np.arange(n_slc) * SLC_LEN
    ov = ((ci[:, None] < sj[None, :] + SLC_LEN) & (ci[:, None] + CMP_LEN > sj[None, :])).astype(np.float32)
    ov[n_cmp:, :] = 0.0
    overlap_t = np.zeros((LANES, LANES), np.float32)
    overlap_t[:n_slc, :n_blk] = ov.T
    assert FLAG0 + n_slc <= PAD_LANE
    kflag = np.zeros((t, LANES), np.float32)
    kflag[np.arange(t), FLAG0 + np.arange(t) // SLC_LEN] = 1.0
    return nd_bucket, w_bucket, bc_bucket, overlap_t, kflag


def _nsa_w_in_columns():
    kv0 = NSA_HEADS * NSA_DH
    kvw = NSA_GROUPS * NSA_DH
    cols = list(range(kv0))
    for br in range(3):
        for g in range(NSA_GROUPS):
            k_src = kv0 + (2 * br) * kvw + g * NSA_DH
            v_src = kv0 + (2 * br + 1) * kvw + g * NSA_DH
            cols += list(range(k_src, k_src + NSA_DH)) + list(range(v_src, v_src + NSA_DH))
    gate0 = kv0 + 6 * kvw
    for g in range(NSA_GROUPS):
        blk = [-1] * LANES
        for br in range(3):
            for p in range(NSA_HPG):
                blk[br * NSA_HPG + p] = gate0 + br * NSA_HEADS + g * NSA_HPG + p
        cols += blk
    return np.asarray(cols, np.int32)


def _retention_layer(x2, b, t, norm_g, w_in, w_out):
    h = _norm_matmul(x2, norm_g, w_in.astype(BF16), tn=1024)
    o = _retention_core(h.reshape(b, t, -1))
    return o.reshape(b * t, -1), w_out.astype(BF16)


def _nsa_layer(x2, b, t, norm_g, w_in, w_out, q_gain, k_gain, cmp_pos, cmp_w1, cmp_w2, bias_nd, bias_w,
               bias_cmp, overlap_t, kflag):
    cols = _nsa_w_in_columns()
    w_k = jnp.where(cols[None, :] >= 0, jnp.take(w_in, np.maximum(cols, 0), axis=1), 0.0).astype(BF16)
    h3 = _norm_matmul(x2, norm_g, w_k, tn=1024).reshape(b, t, NSA_COLS)

    dh = NSA_DH
    w1 = cmp_w1.reshape(2, CMP_LEN, dh, CMP_HID)
    zero1 = jnp.zeros((CMP_LEN, dh, CMP_HID), F32)
    w1cat = jnp.concatenate([jnp.concatenate([w1[0], zero1], axis=-1),
                             jnp.concatenate([zero1, w1[1]], axis=-1)], axis=1).astype(BF16)
    zero2 = jnp.zeros((CMP_HID, dh), F32)
    w2cat = jnp.concatenate([jnp.concatenate([cmp_w2[0], zero2], axis=-1),
                             jnp.concatenate([zero2, cmp_w2[1]], axis=-1)], axis=0).astype(BF16)
    pos = jnp.concatenate([cmp_pos[0], cmp_pos[1]], axis=-1)
    ones = jnp.ones((dh,), F32)
    kvc = _compress(h3, pos, w1cat.reshape(CMP_LEN * LANES, 2 * CMP_HID), w2cat,
                    jnp.concatenate([k_gain[0], ones])[None, :])

    qgain = jnp.tile(q_gain * (dh ** -0.5 * LOG2E), NSA_HPG)[None, :]
    kgain = jnp.stack([jnp.concatenate([k_gain[1], ones]), jnp.concatenate([k_gain[2], ones])])
    o = _nsa_attention(h3, kvc, qgain, kgain, bias_nd, bias_w, bias_cmp, overlap_t, kflag)
    return o.reshape(b * t, -1), w_out.astype(BF16)


def kernel(x, ffn1_norm, ffn1_w_gu, ffn1_w_down, mix_norm, ffn2_norm, ffn2_w_gu, ffn2_w_down,
           ret_w_in, ret_w_out, nsa_w_in, nsa_w_out, nsa_q_gain, nsa_k_gain,
           nsa_cmp_pos, nsa_cmp_w1, nsa_cmp_w2, rel_bias):
    b, t, d = x.shape
    depth = ffn1_norm.shape[0]

    def ffn_weights(w_gu, w_down):
        return w_gu.astype(BF16), w_down.astype(BF16)

    nd_bucket, w_bucket, bc_bucket, overlap_t, kflag = _nsa_constants(t)
    bias_nd, bias_w, bias_cmp = _bias_tiles(rel_bias, nd_bucket, w_bucket, bc_bucket)
    bias_nd = bias_nd.reshape(NSA_GROUPS, NSA_HPG, 2 * TQ, TQ)
    bias_w = bias_w.reshape(NSA_GROUPS, NSA_HPG, WINDOW + TQ, TQ)
    bias_cmp = bias_cmp.reshape(NSA_GROUPS, NSA_HPG, 2 * LANES, TQ)
    overlap_t = jnp.asarray(overlap_t, BF16)
    kflag = jnp.asarray(kflag, BF16)

    x2 = x.reshape(b * t, d)
    for layer in range(depth):
        x2 = _ffn(x2, ffn1_norm[layer][None, :], *ffn_weights(ffn1_w_gu[layer], ffn1_w_down[layer]))
        j = layer // 2
        if layer % 2 == 0:
            o2, w_out = _retention_layer(x2, b, t, mix_norm[layer][None, :], ret_w_in[j], ret_w_out[j])
        else:
            o2, w_out = _nsa_layer(x2, b, t, mix_norm[layer][None, :], nsa_w_in[j], nsa_w_out[j], nsa_q_gain[j],
                                   nsa_k_gain[j], nsa_cmp_pos[j], nsa_cmp_w1[j], nsa_cmp_w2[j], bias_nd, bias_w,
                                   bias_cmp, overlap_t, kflag)
        x2 = _proj_ffn(o2, w_out, x2, ffn2_norm[layer][None, :],
                       *ffn_weights(ffn2_w_gu[layer], ffn2_w_down[layer]))
    return x2.reshape(b, t, d)
```

```python
import functools

import numpy as np
import jax
import jax.numpy as jnp
from jax import lax
from jax.experimental import pallas as pl
from jax.experimental.pallas import tpu as pltpu

F32 = jnp.float32
BF16 = jnp.bfloat16

RMS_EPS = 1e-6
NEG = -1e30
SEL_BIG = 1e9
LOG2E = 1.4426950408889634

LANES = 128
VMEM_LIMIT = 56 * 1024 * 1024

RET_HEADS = 4
RET_DK = 256
RET_DV = 512
RET_CHUNK = 128
RET_ROPE_BASE = 10000.0

NSA_HEADS = 16
NSA_GROUPS = 4
NSA_HPG = 4
NSA_DH = 64
CMP_LEN = 32
CMP_STRIDE = 16
CMP_HID = 256
SLC_LEN = 64
N_SEL = 8
WINDOW = 512
NUM_BUCKETS = 32
MAX_DISTANCE = 128
TQ = 256
CK = 256
FLAG0 = 64
PAD_LANE = 96
NSA_COLS = 1024 + 3 * 4 * 128 + 4 * 128
NSA_KV0 = 1024 // 128


def _dot(a, b):
    return jnp.dot(a, b, preferred_element_type=F32)


def _dot_nt(a, b):
    return lax.dot_general(a, b, (((1,), (1,)), ((), ())), preferred_element_type=F32)


def _dot_tn(a, b):
    return lax.dot_general(a, b, (((0,), (0,)), ((), ())), preferred_element_type=F32)


def _rms(x, g):
    ms = jnp.mean(x * x, axis=-1, keepdims=True)
    return x * lax.rsqrt(ms + RMS_EPS) * g


def _cparams(sem):
    return pltpu.CompilerParams(dimension_semantics=sem, vmem_limit_bytes=VMEM_LIMIT)


def _resident(shape):
    n = len(shape)
    return pl.BlockSpec(shape, lambda *_: (0,) * n, pipeline_mode=pl.Buffered(1))


def _ffn_kernel(x_ref, g_ref, wgu_ref, wd_ref, o_ref):
    d_ff = wd_ref.shape[0]
    x = x_ref[...]
    xn = _rms(x, g_ref[...]).astype(BF16)
    a = _dot(xn, wgu_ref[:, :d_ff])
    b = _dot(xn, wgu_ref[:, d_ff:])
    h = (a * jax.nn.sigmoid(a) * b).astype(BF16)
    o_ref[...] = x + 0.5 * _dot(h, wd_ref[...])


def _ffn(x2, g, wgu, wd, tm=512):
    n, d = x2.shape
    return pl.pallas_call(
        _ffn_kernel,
        grid=(n // tm,),
        in_specs=[
            pl.BlockSpec((tm, d), lambda i: (i, 0)),
            _resident((1, d)),
            _resident(wgu.shape),
            _resident(wd.shape),
        ],
        out_specs=pl.BlockSpec((tm, d), lambda i: (i, 0)),
        out_shape=jax.ShapeDtypeStruct((n, d), F32),
        compiler_params=_cparams(("parallel",)),
    )(x2, g, wgu, wd)


def _proj_ffn_kernel(o_ref, wo_ref, x_ref, g_ref, wgu_ref, wd_ref, y_ref):
    d_ff = wd_ref.shape[0]
    x = x_ref[...] + _dot(o_ref[...], wo_ref[...])
    xn = _rms(x, g_ref[...]).astype(BF16)
    a = _dot(xn, wgu_ref[:, :d_ff])
    b = _dot(xn, wgu_ref[:, d_ff:])
    h = (a * jax.nn.sigmoid(a) * b).astype(BF16)
    y_ref[...] = x + 0.5 * _dot(h, wd_ref[...])


def _proj_ffn(o2, wo, x2, g, wgu, wd, tm=512):
    n, d = x2.shape
    k = o2.shape[1]
    return pl.pallas_call(
        _proj_ffn_kernel,
        grid=(n // tm,),
        in_specs=[
            pl.BlockSpec((tm, k), lambda i: (i, 0)),
            _resident(wo.shape),
            pl.BlockSpec((tm, d), lambda i: (i, 0)),
            _resident((1, d)),
            _resident(wgu.shape),
            _resident(wd.shape),
        ],
        out_specs=pl.BlockSpec((tm, d), lambda i: (i, 0)),
        out_shape=jax.ShapeDtypeStruct((n, d), F32),
        compiler_params=_cparams(("parallel",)),
    )(o2, wo, x2, g, wgu, wd)


def _norm_matmul_kernel(x_ref, g_ref, w_ref, o_ref, *, tn):
    xn = _rms(x_ref[...], g_ref[...]).astype(BF16)
    for j in range(w_ref.shape[1] // tn):
        o_ref[:, j * tn:(j + 1) * tn] = _dot(xn, w_ref[:, j * tn:(j + 1) * tn]).astype(o_ref.dtype)


def _norm_matmul(x2, g, w, tn, tm=512):
    n, d = x2.shape
    nout = w.shape[1]
    return pl.pallas_call(
        functools.partial(_norm_matmul_kernel, tn=tn),
        grid=(n // tm,),
        in_specs=[pl.BlockSpec((tm, d), lambda i: (i, 0)), _resident((1, d)), _resident((d, nout))],
        out_specs=pl.BlockSpec((tm, nout), lambda i: (i, 0)),
        out_shape=jax.ShapeDtypeStruct((n, nout), BF16),
        compiler_params=_cparams(("parallel",)),
    )(x2, g, w)


def _ret_kernel(cd_ref, q_ref, k_ref, v_ref, g_ref, cos_ref, sin_ref, intra_ref, qd_ref, kd_ref,
                o_ref, s_ref, *, n_chunks):
    c_len = RET_CHUNK
    half = RET_DK // 2
    cd = cd_ref[pl.program_id(1)]
    intra = intra_ref[0]
    qd = qd_ref[0]
    kd = kd_ref[0]
    s_ref[...] = jnp.zeros_like(s_ref)

    def chunk(c, carry):
        r0 = pl.multiple_of(c * c_len, c_len)
        cos = cos_ref[pl.ds(r0, c_len), :]
        sin = sin_ref[pl.ds(r0, c_len), :]

        def rot(x):
            x1, x2 = x[:, :half], x[:, half:]
            return jnp.concatenate([x1 * cos - x2 * sin, x1 * sin + x2 * cos], axis=-1)

        qr = rot(q_ref[0, pl.ds(r0, c_len), :].astype(F32))
        kr = rot(k_ref[0, pl.ds(r0, c_len), :].astype(F32)) * (RET_DK ** -0.5)
        vc = v_ref[0, pl.ds(r0, c_len), :]
        s = _dot_nt(qr.astype(BF16), kr.astype(BF16)) * intra
        state = s_ref[...]
        o = _dot(s.astype(BF16), vc) + _dot((qr * qd).astype(BF16), state.astype(BF16))
        s_ref[...] = cd * state + _dot_tn((kr * kd).astype(BF16), vc)
        o = o * lax.rsqrt(jnp.mean(o * o, axis=-1, keepdims=True) + RMS_EPS)
        gate = g_ref[0, pl.ds(r0, c_len), :].astype(F32)
        o_ref[0, pl.ds(r0, c_len), :] = (gate * jax.nn.sigmoid(gate) * o).astype(o_ref.dtype)
        return carry

    lax.fori_loop(0, n_chunks, chunk, 0, unroll=8)


def _retention_core(h3):
    b, t, _ = h3.shape
    hh, dk, dv, c = RET_HEADS, RET_DK, RET_DV, RET_CHUNK
    half = dk // 2
    pos = jnp.arange(t, dtype=F32)
    inv_freq = RET_ROPE_BASE ** (-jnp.arange(half, dtype=F32) / half)
    ang = pos[:, None] * inv_freq[None, :]
    cos, sin = jnp.cos(ang), jnp.sin(ang)
    log_gamma = jnp.log(1.0 - 2.0 ** (-5.0 - jnp.arange(hh, dtype=F32)))
    i = jnp.arange(c)
    diff = i[:, None] - i[None, :]
    intra = jnp.where(diff >= 0, jnp.exp(log_gamma[:, None, None] * jnp.maximum(diff, 0)), 0.0)
    q_dec = jnp.exp(log_gamma[:, None] * (i + 1))[:, :, None]
    k_dec = jnp.exp(log_gamma[:, None] * (c - 1 - i))[:, :, None]
    chunk_dec = jnp.exp(log_gamma * c)
    k0 = hh
    v0 = 2 * hh * dk // dv
    g0 = v0 + hh
    return pl.pallas_call(
        functools.partial(_ret_kernel, n_chunks=t // c),
        grid=(b, hh),
        in_specs=[
            pl.BlockSpec(memory_space=pltpu.SMEM),
            pl.BlockSpec((1, t, dk), lambda bi, hi: (bi, 0, hi)),
            pl.BlockSpec((1, t, dk), lambda bi, hi: (bi, 0, k0 + hi)),
            pl.BlockSpec((1, t, dv), lambda bi, hi: (bi, 0, v0 + hi)),
            pl.BlockSpec((1, t, dv), lambda bi, hi: (bi, 0, g0 + hi)),
            _resident((t, half)),
            _resident((t, half)),
            pl.BlockSpec((1, c, c), lambda bi, hi: (hi, 0, 0)),
            pl.BlockSpec((1, c, 1), lambda bi, hi: (hi, 0, 0)),
            pl.BlockSpec((1, c, 1), lambda bi, hi: (hi, 0, 0)),
        ],
        out_specs=pl.BlockSpec((1, t, dv), lambda bi, hi: (bi, 0, hi)),
        out_shape=jax.ShapeDtypeStruct((b, t, hh * dv), BF16),
        scratch_shapes=[pltpu.VMEM((dk, dv), F32)],
        compiler_params=_cparams(("parallel", "arbitrary")),
    )(chunk_dec, h3, h3, h3, h3, cos, sin, intra, q_dec, k_dec)


def _k_headnorm(x, gain_row):
    klane = lax.broadcasted_iota(jnp.int32, x.shape, 1) < NSA_DH
    ms = jnp.sum(jnp.where(klane, x * x, 0.0), axis=-1, keepdims=True) * (1.0 / NSA_DH)
    return jnp.where(klane, x * lax.rsqrt(ms + RMS_EPS) * gain_row, x)


def _cmp_kernel(a_ref, pos_ref, w1_ref, w2_ref, kg_ref, o_ref, af_ref, *, t):
    n_blk = t // CMP_STRIDE
    af_ref[0:t, :] = a_ref[0].astype(F32)
    af_ref[t:t + CMP_LEN, :] = jnp.zeros((CMP_LEN, LANES), F32)
    blocks = [(af_ref[pl.ds(l, n_blk, stride=CMP_STRIDE), :] + pos_ref[l:l + 1, :]).astype(BF16)
              for l in range(CMP_LEN)]
    hid = jax.nn.gelu(_dot(jnp.concatenate(blocks, axis=1), w1_ref[...]), approximate=True)
    out = _dot(hid.astype(BF16), w2_ref[...])
    o_ref[0, 0] = _k_headnorm(out, kg_ref[...]).astype(o_ref.dtype)


def _compress(h3, pos, w1cat, w2cat, kgain_row):
    b, t, _ = h3.shape
    n_blk = t // CMP_STRIDE
    return pl.pallas_call(
        functools.partial(_cmp_kernel, t=t),
        grid=(b, NSA_GROUPS),
        in_specs=[
            pl.BlockSpec((1, t, LANES), lambda bi, gi: (bi, 0, NSA_KV0 + gi)),
            _resident((CMP_LEN, LANES)),
            _resident((CMP_LEN * LANES, 2 * CMP_HID)),
            _resident((2 * CMP_HID, LANES)),
            _resident((1, LANES)),
        ],
        out_specs=pl.BlockSpec((1, 1, n_blk, LANES), lambda bi, gi: (bi, gi, 0, 0)),
        out_shape=jax.ShapeDtypeStruct((b, NSA_GROUPS, n_blk, LANES), BF16),
        scratch_shapes=[pltpu.VMEM((t + CMP_LEN, LANES), F32)],
        compiler_params=_cparams(("parallel", "parallel")),
    )(h3, pos, w1cat, w2cat, kgain_row)


def _nsa_attn_kernel(q_ref, kvc_ref, kvs_ref, kvw_ref, gate_ref, qg_ref, kg_ref, bnd_ref, bw_ref, bc_ref,
                     ovt_ref, kflag_ref, o_ref, ks_ref, kw_ref, vst_ref, vwt_ref, sel_ref, *, t):
    hpg = NSA_HPG
    dh = NSA_DH
    qt = pl.program_id(2)
    q0 = pl.multiple_of(qt * TQ, TQ)
    lane = lax.broadcasted_iota(jnp.int32, (1, LANES), 1)
    klane = lane < dh
    ones_rows = lax.broadcasted_iota(jnp.int32, (LANES, LANES), 0) < dh

    @pl.when(qt == 0)
    def _():
        pad_rows = jnp.where(lane == PAD_LANE, 1.0, 0.0).astype(BF16)
        ks_ref[0:TQ, :] = jnp.broadcast_to(pad_rows, (TQ, LANES))
        kw_ref[0:WINDOW, :] = jnp.broadcast_to(pad_rows, (WINDOW, LANES))
        pad_col = jnp.where(ones_rows, 1.0, 0.0).astype(BF16)
        for i in range(TQ // LANES):
            vst_ref[:, i * LANES:(i + 1) * LANES] = pad_col
        for i in range(WINDOW // LANES):
            vwt_ref[:, i * LANES:(i + 1) * LANES] = pad_col

        def norm_block(c, carry):
            r0 = pl.multiple_of(c * LANES, LANES)
            for src, dst, dst_t, pad, gi in ((kvs_ref, ks_ref, vst_ref, TQ, 0), (kvw_ref, kw_ref, vwt_ref, WINDOW, 1)):
                x = _k_headnorm(src[0, pl.ds(r0, LANES), :].astype(F32), kg_ref[gi:gi + 1, :])
                flags = kflag_ref[pl.ds(r0, LANES), :] if gi == 0 else jnp.zeros((LANES, LANES), BF16)
                dst[pl.ds(pad + r0, LANES), :] = jnp.where(klane, x.astype(BF16), flags)
                dst_t[:, pl.ds(pad + r0, LANES)] = jnp.where(ones_rows, 1.0, x.T).astype(BF16)
            return carry

        lax.fori_loop(0, t // LANES, norm_block, 0, unroll=4)

    xq = q_ref[0].astype(F32)
    lane_q = lax.broadcasted_iota(jnp.int32, (1, hpg * dh), 1)
    xq2 = xq * xq
    qz = []
    for p in range(hpg):
        seg = (lane_q >= dh * p) & (lane_q < dh * (p + 1))
        ms = jnp.sum(jnp.where(seg, xq2, 0.0), axis=-1, keepdims=True) * (1.0 / dh)
        c0 = LANES * (p // 2)
        half = xq[:, c0:c0 + LANES] * lax.rsqrt(ms + RMS_EPS) * qg_ref[:, c0:c0 + LANES]
        if p % 2 == 1:
            half = pltpu.roll(half, dh, 1)
        qz.append(jnp.where(klane, half, 0.0))

    def stacked_q(flag_lanes):
        return jnp.concatenate([(qp + flag_lanes).astype(BF16) for qp in qz], axis=0)

    ww = WINDOW + TQ
    s_w = _dot_nt(kw_ref[pl.ds(q0, ww), :], stacked_q(jnp.where(lane == PAD_LANE, NEG, 0.0)))
    s = jnp.concatenate([s_w[:, p * TQ:(p + 1) * TQ] + bw_ref[0, p] for p in range(hpg)], axis=1)
    m_win = jnp.max(s, axis=0, keepdims=True)
    acc_win = _dot(vwt_ref[:, pl.ds(q0, ww)], jnp.exp2(s - m_win).astype(BF16))

    per_tile = TQ // CMP_STRIDE
    off = pl.multiple_of((LANES - per_tile) - qt * per_tile, per_tile)
    blk_i = lax.broadcasted_iota(jnp.int32, (LANES, TQ), 0)
    valid_c = (CMP_STRIDE * blk_i + (CMP_LEN - 1)) <= (q0 + lax.broadcasted_iota(jnp.int32, (LANES, TQ), 1))
    kvc = kvc_ref[0, 0]
    s_c = _dot_nt(kvc, stacked_q(0.0))
    pcs = []
    for p in range(hpg):
        s = jnp.where(valid_c, s_c[:, p * TQ:(p + 1) * TQ] + bc_ref[0, p, pl.ds(off, LANES), :], NEG)
        m = jnp.max(s, axis=0, keepdims=True)
        e = jnp.where(valid_c, jnp.exp2(s - m), 0.0)
        l = jnp.sum(e, axis=0, keepdims=True)
        pcs.append(e / jnp.where(l > 0.0, l, 1.0))
    vc_t = jnp.where(ones_rows, 1.0, kvc.astype(F32).T).astype(BF16)
    o_cmp = _dot(vc_t, jnp.concatenate(pcs, axis=1).astype(BF16))
    psum = pcs[0]
    for p in range(1, hpg):
        psum = psum + pcs[p]
    p_hi = psum.astype(BF16)
    p_lo = (psum - p_hi.astype(F32)).astype(BF16)
    n_slc = t // SLC_LEN
    imp_t = (_dot(ovt_ref[...], p_hi) + _dot(ovt_ref[...], p_lo))[:n_slc, :]

    jrow = lax.broadcasted_iota(jnp.int32, (n_slc, TQ), 0)
    cur = (q0 + lax.broadcasted_iota(jnp.int32, (n_slc, TQ), 1)) // SLC_LEN
    forced = (jrow == 0) | (jrow == cur) | (jrow == cur - 1)
    vals = jnp.where(forced, SEL_BIG, jnp.where(jrow <= cur, imp_t, -SEL_BIG))
    terms = []
    for jp in range(n_slc):
        row = vals[jp:jp + 1, :]
        ahead = (row > vals) | ((row == vals) & (jrow > jp))
        terms.append(jnp.where(ahead, 1.0, 0.0))
    while len(terms) > 1:
        terms = [a + b for a, b in zip(terms[0::2], terms[1::2])]
    cnt = terms[0]

    far_end = (qt - 1) * TQ
    drop_nd = jnp.where(cnt < float(N_SEL), 0.0, NEG)
    drop_far = jnp.where(jrow * SLC_LEN < far_end, drop_nd, NEG)

    def flag_lanes(drop):
        rows = [jnp.zeros((FLAG0, TQ), F32), drop, jnp.full((PAD_LANE - FLAG0 - n_slc + 8, TQ), NEG, F32),
                jnp.zeros((LANES - PAD_LANE - 8, TQ), F32)]
        return jnp.concatenate(rows, axis=0).T

    ck = CK
    q_nd = stacked_q(flag_lanes(drop_nd))
    q_far = stacked_q(flag_lanes(drop_far))
    n_far = jnp.maximum(far_end + ck - 1, 0) // ck

    s_nd = _dot_nt(ks_ref[pl.ds(q0, 2 * TQ), :], q_nd)
    s_nd = jnp.concatenate([s_nd[:, p * TQ:(p + 1) * TQ] + bnd_ref[0, p] for p in range(hpg)], axis=1)

    def select_branch(k):
        scores = [s_nd]
        values = [vst_ref[:, pl.ds(q0, 2 * TQ)]]
        for c in range(k):
            scores.append(_dot_nt(ks_ref[TQ + c * ck:TQ + (c + 1) * ck, :], q_far))
            values.append(vst_ref[:, TQ + c * ck:TQ + (c + 1) * ck])
        m = jnp.max(scores[0], axis=0, keepdims=True)
        for s in scores[1:]:
            m = jnp.maximum(m, jnp.max(s, axis=0, keepdims=True))
        acc = _dot(values[0], jnp.exp2(scores[0] - m).astype(BF16))
        for s, v in zip(scores[1:], values[1:]):
            acc = acc + _dot(v, jnp.exp2(s - m).astype(BF16))
        sel_ref[...] = acc

    for k in range((t - 2 * TQ) // ck + 1):
        pl.when(n_far == k)(functools.partial(select_branch, k))
    acc_sel = sel_ref[...]

    gl_t = jax.nn.sigmoid(gate_ref[0].astype(F32)).T
    outs = []
    for p in range(hpg):
        sl = slice(p * TQ, (p + 1) * TQ)
        g_c = gl_t[p:p + 1, :]
        g_s = gl_t[hpg + p:hpg + p + 1, :] / acc_sel[0:1, sl]
        g_w = gl_t[2 * hpg + p:2 * hpg + p + 1, :] / acc_win[0:1, sl]
        outs.append(g_c * o_cmp[dh:, sl] + g_s * acc_sel[dh:, sl] + g_w * acc_win[dh:, sl])
    o_ref[0] = jnp.concatenate(outs, axis=0).T.astype(o_ref.dtype)


def _nsa_attention(h3, kvc, qgain, kgain, bias_nd, bias_w, bias_cmp, overlap_t, kflag):
    b, t, _ = h3.shape
    g, hpg = NSA_GROUPS, NSA_HPG
    n_blk = t // CMP_STRIDE
    qw = hpg * NSA_DH
    sel0, win0, gate0 = NSA_KV0 + g, NSA_KV0 + 2 * g, NSA_KV0 + 3 * g
    return pl.pallas_call(
        functools.partial(_nsa_attn_kernel, t=t),
        grid=(g, b, t // TQ),
        in_specs=[
            pl.BlockSpec((1, TQ, qw), lambda gi, bi, qi: (bi, qi, gi)),
            pl.BlockSpec((1, 1, n_blk, LANES), lambda gi, bi, qi: (bi, gi, 0, 0)),
            pl.BlockSpec((1, t, LANES), lambda gi, bi, qi: (bi, 0, sel0 + gi)),
            pl.BlockSpec((1, t, LANES), lambda gi, bi, qi: (bi, 0, win0 + gi)),
            pl.BlockSpec((1, TQ, LANES), lambda gi, bi, qi: (bi, qi, gate0 + gi)),
            pl.BlockSpec((1, qw), lambda gi, bi, qi: (0, 0)),
            pl.BlockSpec((2, LANES), lambda gi, bi, qi: (0, 0)),
            pl.BlockSpec((1, hpg, 2 * TQ, TQ), lambda gi, bi, qi: (gi, 0, 0, 0)),
            pl.BlockSpec((1, hpg, WINDOW + TQ, TQ), lambda gi, bi, qi: (gi, 0, 0, 0)),
            pl.BlockSpec((1, hpg, 2 * LANES, TQ), lambda gi, bi, qi: (gi, 0, 0, 0)),
            pl.BlockSpec((LANES, LANES), lambda gi, bi, qi: (0, 0)),
            pl.BlockSpec((t, LANES), lambda gi, bi, qi: (0, 0)),
        ],
        out_specs=pl.BlockSpec((1, TQ, qw), lambda gi, bi, qi: (bi, qi, gi)),
        out_shape=jax.ShapeDtypeStruct((b, t, NSA_HEADS * NSA_DH), BF16),
        scratch_shapes=[pltpu.VMEM((TQ + t, LANES), BF16), pltpu.VMEM((WINDOW + t, LANES), BF16),
                        pltpu.VMEM((LANES, TQ + t), BF16), pltpu.VMEM((LANES, WINDOW + t), BF16),
                        pltpu.VMEM((LANES, hpg * TQ), F32)],
        compiler_params=_cparams(("arbitrary", "arbitrary", "arbitrary")),
    )(h3, kvc, h3, h3, h3, qgain, kgain, bias_nd, bias_w, bias_cmp, overlap_t, kflag)


def _bias_kernel(tbl_ref, nd_ref, w_ref, bc_ref, ond_ref, ow_ref, obc_ref):
    h = pl.program_id(0)
    far = tbl_ref[NUM_BUCKETS - 1, h]
    nd = nd_ref[...]
    w = w_ref[...]
    bc = bc_ref[...]
    o_nd = jnp.full(nd.shape, NEG, F32)
    o_w = jnp.full(w.shape, NEG, F32)
    o_bc = jnp.zeros(bc.shape, F32)
    for bucket in range(NUM_BUCKETS):
        v = tbl_ref[bucket, h]
        o_nd = jnp.where(nd == bucket, (v - far) * LOG2E, o_nd)
        o_w = jnp.where(w == bucket, (v - far) * LOG2E, o_w)
        o_bc = jnp.where(bc == bucket, v * LOG2E, o_bc)
    ond_ref[0] = o_nd
    ow_ref[0] = o_w
    obc_ref[0] = o_bc


def _bias_tiles(rel_bias, nd_bucket, w_bucket, bc_bucket):
    n_heads = rel_bias.shape[1]
    tables = (nd_bucket, w_bucket, bc_bucket)
    return pl.pallas_call(
        _bias_kernel,
        grid=(n_heads,),
        in_specs=[pl.BlockSpec(memory_space=pltpu.SMEM)] + [_resident(a.shape) for a in tables],
        out_specs=[pl.BlockSpec((1,) + a.shape, lambda h: (h, 0, 0)) for a in tables],
        out_shape=[jax.ShapeDtypeStruct((n_heads,) + a.shape, F32) for a in tables],
        compiler_params=_cparams(("parallel",)),
    )(rel_bias.astype(F32), *[jnp.asarray(a, jnp.int32) for a in tables])


def _t5_bucket_np(dist):
    n = np.maximum(dist, 0)
    max_exact = NUM_BUCKETS // 2
    nf = np.maximum(n, max_exact).astype(np.float32)
    large = max_exact + (np.log(nf / max_exact) / np.float32(np.log(MAX_DISTANCE / max_exact))
                         * (NUM_BUCKETS - max_exact)).astype(np.int32)
    large = np.minimum(large, NUM_BUCKETS - 1)
    return np.where(n < max_exact, n, large)


def _nsa_constants(t):
    n_cmp = (t - CMP_LEN) // CMP_STRIDE + 1
    n_slc = t // SLC_LEN
    n_blk = t // CMP_STRIDE
    per_tile = TQ // CMP_STRIDE
    assert n_blk <= LANES and per_tile * (t // TQ - 1) <= LANES - per_tile
    iq = np.arange(TQ)
    rel = iq[None, :] - iq[:, None]
    assert _t5_bucket_np(np.array([TQ + 1]))[0] == NUM_BUCKETS - 1
    dist_nd = np.concatenate([TQ + rel, rel], axis=0)
    nd_bucket = np.where(dist_nd >= 0, _t5_bucket_np(dist_nd), -1)
    dist_w = np.arange(TQ)[None, :] + WINDOW - np.arange(WINDOW + TQ)[:, None]
    w_bucket = np.where((dist_w >= 0) & (dist_w < WINDOW), _t5_bucket_np(dist_w), -1)
    blk_off = np.arange(LANES) - (LANES - per_tile)
    bc_bucket = _t5_bucket_np(iq[None, :] - CMP_STRIDE * blk_off[:, None] - (CMP_LEN - 1))
    bc_bucket = np.concatenate([bc_bucket, bc_bucket], axis=0)
    ci = np.arange(n_blk) * CMP_STRIDE
    sj = np.arange(n_slc) * SLC_LEN
    ov = ((ci[:, None] < sj[None, :] + SLC_LEN) & (ci[:, None] + CMP_LEN > sj[None, :])).astype(np.float32)
    ov[n_cmp:, :] = 0.0
    overlap_t = np.zeros((LANES, LANES), np.float32)
    overlap_t[:n_slc, :n_blk] = ov.T
    assert FLAG0 + n_slc <= PAD_LANE
    kflag = np.zeros((t, LANES), np.float32)
    kflag[np.arange(t), FLAG0 + np.arange(t) // SLC_LEN] = 1.0
    return nd_bucket, w_bucket, bc_bucket, overlap_t, kflag


def _nsa_w_in_columns():
    kv0 = NSA_HEADS * NSA_DH
    kvw = NSA_GROUPS * NSA_DH
    cols = list(range(kv0))
    for br in range(3):
        for g in range(NSA_GROUPS):
            k_src = kv0 + (2 * br) * kvw + g * NSA_DH
            v_src = kv0 + (2 * br + 1) * kvw + g * NSA_DH
            cols += list(range(k_src, k_src + NSA_DH)) + list(range(v_src, v_src + NSA_DH))
    gate0 = kv0 + 6 * kvw
    for g in range(NSA_GROUPS):
        blk = [-1] * LANES
        for br in range(3):
            for p in range(NSA_HPG):
                blk[br * NSA_HPG + p] = gate0 + br * NSA_HEADS + g * NSA_HPG + p
        cols += blk
    return np.asarray(cols, np.int32)


def _retention_layer(x2, b, t, norm_g, w_in, w_out):
    h = _norm_matmul(x2, norm_g, w_in.astype(BF16), tn=1024)
    o = _retention_core(h.reshape(b, t, -1))
    return o.reshape(b * t, -1), w_out.astype(BF16)


def _nsa_layer(x2, b, t, norm_g, w_in, w_out, q_gain, k_gain, cmp_pos, cmp_w1, cmp_w2, bias_nd, bias_w,
               bias_cmp, overlap_t, kflag):
    cols = _nsa_w_in_columns()
    w_k = jnp.where(cols[None, :] >= 0, jnp.take(w_in, np.maximum(cols, 0), axis=1), 0.0).astype(BF16)
    h3 = _norm_matmul(x2, norm_g, w_k, tn=1024).reshape(b, t, NSA_COLS)

    dh = NSA_DH
    w1 = cmp_w1.reshape(2, CMP_LEN, dh, CMP_HID)
    zero1 = jnp.zeros((CMP_LEN, dh, CMP_HID), F32)
    w1cat = jnp.concatenate([jnp.concatenate([w1[0], zero1], axis=-1),
                             jnp.concatenate([zero1, w1[1]], axis=-1)], axis=1).astype(BF16)
    zero2 = jnp.zeros((CMP_HID, dh), F32)
    w2cat = jnp.concatenate([jnp.concatenate([cmp_w2[0], zero2], axis=-1),
                             jnp.concatenate([zero2, cmp_w2[1]], axis=-1)], axis=0).astype(BF16)
    pos = jnp.concatenate([cmp_pos[0], cmp_pos[1]], axis=-1)
    ones = jnp.ones((dh,), F32)
    kvc = _compress(h3, pos, w1cat.reshape(CMP_LEN * LANES, 2 * CMP_HID), w2cat,
                    jnp.concatenate([k_gain[0], ones])[None, :])

    qgain = jnp.tile(q_gain * (dh ** -0.5 * LOG2E), NSA_HPG)[None, :]
    kgain = jnp.stack([jnp.concatenate([k_gain[1], ones]), jnp.concatenate([k_gain[2], ones])])
    o = _nsa_attention(h3, kvc, qgain, kgain, bias_nd, bias_w, bias_cmp, overlap_t, kflag)
    return o.reshape(b * t, -1), w_out.astype(BF16)


def kernel(x, ffn1_norm, ffn1_w_gu, ffn1_w_down, mix_norm, ffn2_norm, ffn2_w_gu, ffn2_w_down,
           ret_w_in, ret_w_out, nsa_w_in, nsa_w_out, nsa_q_gain, nsa_k_gain,
           nsa_cmp_pos, nsa_cmp_w1, nsa_cmp_w2, rel_bias):
    b, t, d = x.shape
    depth = ffn1_norm.shape[0]

    def ffn_weights(w_gu, w_down):
        return w_gu.astype(BF16), w_down.astype(BF16)

    nd_bucket, w_bucket, bc_bucket, overlap_t, kflag = _nsa_constants(t)
    bias_nd, bias_w, bias_cmp = _bias_tiles(rel_bias, nd_bucket, w_bucket, bc_bucket)
    bias_nd = bias_nd.reshape(NSA_GROUPS, NSA_HPG, 2 * TQ, TQ)
    bias_w = bias_w.reshape(NSA_GROUPS, NSA_HPG, WINDOW + TQ, TQ)
    bias_cmp = bias_cmp.reshape(NSA_GROUPS, NSA_HPG, 2 * LANES, TQ)
    overlap_t = jnp.asarray(overlap_t, BF16)
    kflag = jnp.asarray(kflag, BF16)

    x2 = x.reshape(b * t, d)
    for layer in range(depth):
        x2 = _ffn(x2, ffn1_norm[layer][None, :], *ffn_weights(ffn1_w_gu[layer], ffn1_w_down[layer]))
        j = layer // 2
        if layer % 2 == 0:
            o2, w_out = _retention_layer(x2, b, t, mix_norm[layer][None, :], ret_w_in[j], ret_w_out[j])
        else:
            o2, w_out = _nsa_layer(x2, b, t, mix_norm[layer][None, :], nsa_w_in[j], nsa_w_out[j], nsa_q_gain[j],
                                   nsa_k_gain[j], nsa_cmp_pos[j], nsa_cmp_w1[j], nsa_cmp_w2[j], bias_nd, bias_w,
                                   bias_cmp, overlap_t, kflag)
        x2 = _proj_ffn(o2, w_out, x2, ffn2_norm[layer][None, :],
                       *ffn_weights(ffn2_w_gu[layer], ffn2_w_down[layer]))
    return x2.reshape(b, t, d)
```

```python
import functools

import numpy as np
import jax
import jax.numpy as jnp
from jax import lax
from jax.experimental import pallas as pl
from jax.experimental.pallas import tpu as pltpu

F32 = jnp.float32
BF16 = jnp.bfloat16

RMS_EPS = 1e-6
NEG = -1e30
SEL_BIG = 1e9
LOG2E = 1.4426950408889634

LANES = 128
VMEM_LIMIT = 56 * 1024 * 1024

RET_HEADS = 4
RET_DK = 256
RET_DV = 512
RET_CHUNK = 128
RET_ROPE_BASE = 10000.0

NSA_HEADS = 16
NSA_GROUPS = 4
NSA_HPG = 4
NSA_DH = 64
CMP_LEN = 32
CMP_STRIDE = 16
CMP_HID = 256
SLC_LEN = 64
N_SEL = 8
WINDOW = 512
NUM_BUCKETS = 32
MAX_DISTANCE = 128
TQ = 256
CK = 256
FLAG0 = 64
PAD_LANE = 96
NSA_COLS = 1024 + 3 * 4 * 128 + 4 * 128
NSA_KV0 = 1024 // 128


def _dot(a, b):
    return jnp.dot(a, b, preferred_element_type=F32)


def _dot_nt(a, b):
    return lax.dot_general(a, b, (((1,), (1,)), ((), ())), preferred_element_type=F32)


def _dot_tn(a, b):
    return lax.dot_general(a, b, (((0,), (0,)), ((), ())), preferred_element_type=F32)


def _rms(x, g):
    ms = jnp.mean(x * x, axis=-1, keepdims=True)
    return x * lax.rsqrt(ms + RMS_EPS) * g


def _cparams(sem):
    return pltpu.CompilerParams(dimension_semantics=sem, vmem_limit_bytes=VMEM_LIMIT)


def _resident(shape):
    n = len(shape)
    return pl.BlockSpec(shape, lambda *_: (0,) * n, pipeline_mode=pl.Buffered(1))


def _ffn_kernel(x_ref, g_ref, wgu_ref, wd_ref, o_ref):
    d_ff = wd_ref.shape[0]
    x = x_ref[...]
    xn = _rms(x, g_ref[...]).astype(BF16)
    a = _dot(xn, wgu_ref[:, :d_ff])
    b = _dot(xn, wgu_ref[:, d_ff:])
    h = (a * jax.nn.sigmoid(a) * b).astype(BF16)
    o_ref[...] = x + 0.5 * _dot(h, wd_ref[...])


def _ffn(x2, g, wgu, wd, tm=512):
    n, d = x2.shape
    return pl.pallas_call(
        _ffn_kernel,
        grid=(n // tm,),
        in_specs=[
            pl.BlockSpec((tm, d), lambda i: (i, 0)),
            _resident((1, d)),
            _resident(wgu.shape),
            _resident(wd.shape),
        ],
        out_specs=pl.BlockSpec((tm, d), lambda i: (i, 0)),
        out_shape=jax.ShapeDtypeStruct((n, d), F32),
        compiler_params=_cparams(("parallel",)),
    )(x2, g, wgu, wd)


def _proj_ffn_kernel(o_ref, wo_ref, x_ref, g_ref, wgu_ref, wd_ref, y_ref):
    d_ff = wd_ref.shape[0]
    x = x_ref[...] + _dot(o_ref[...], wo_ref[...])
    xn = _rms(x, g_ref[...]).astype(BF16)
    a = _dot(xn, wgu_ref[:, :d_ff])
    b = _dot(xn, wgu_ref[:, d_ff:])
    h = (a * jax.nn.sigmoid(a) * b).astype(BF16)
    y_ref[...] = x + 0.5 * _dot(h, wd_ref[...])


def _proj_ffn(o2, wo, x2, g, wgu, wd, tm=512):
    n, d = x2.shape
    k = o2.shape[1]
    return pl.pallas_call(
        _proj_ffn_kernel,
        grid=(n // tm,),
        in_specs=[
            pl.BlockSpec((tm, k), lambda i: (i, 0)),
            _resident(wo.shape),
            pl.BlockSpec((tm, d), lambda i: (i, 0)),
            _resident((1, d)),
            _resident(wgu.shape),
            _resident(wd.shape),
        ],
        out_specs=pl.BlockSpec((tm, d), lambda i: (i, 0)),
        out_shape=jax.ShapeDtypeStruct((n, d), F32),
        compiler_params=_cparams(("parallel",)),
    )(o2, wo, x2, g, wgu, wd)


def _norm_matmul_kernel(x_ref, g_ref, w_ref, o_ref, *, tn):
    xn = _rms(x_ref[...], g_ref[...]).astype(BF16)
    for j in range(w_ref.shape[1] // tn):
        o_ref[:, j * tn:(j + 1) * tn] = _dot(xn, w_ref[:, j * tn:(j + 1) * tn]).astype(o_ref.dtype)


def _norm_matmul(x2, g, w, tn, tm=512):
    n, d = x2.shape
    nout = w.shape[1]
    return pl.pallas_call(
        functools.partial(_norm_matmul_kernel, tn=tn),
        grid=(n // tm,),
        in_specs=[pl.BlockSpec((tm, d), lambda i: (i, 0)), _resident((1, d)), _resident((d, nout))],
        out_specs=pl.BlockSpec((tm, nout), lambda i: (i, 0)),
        out_shape=jax.ShapeDtypeStruct((n, nout), BF16),
        compiler_params=_cparams(("parallel",)),
    )(x2, g, w)


def _ret_kernel(cd_ref, q_ref, k_ref, v_ref, g_ref, cos_ref, sin_ref, intra_ref, qd_ref, kd_ref,
                o_ref, s_ref, *, n_chunks):
    c_len = RET_CHUNK
    half = RET_DK // 2
    cd = cd_ref[pl.program_id(1)]
    intra = intra_ref[0]
    qd = qd_ref[0]
    kd = kd_ref[0]
    s_ref[...] = jnp.zeros_like(s_ref)

    def chunk(c, carry):
        r0 = pl.multiple_of(c * c_len, c_len)
        cos = cos_ref[pl.ds(r0, c_len), :]
        sin = sin_ref[pl.ds(r0, c_len), :]

        def rot(x):
            x1, x2 = x[:, :half], x[:, half:]
            return jnp.concatenate([x1 * cos - x2 * sin, x1 * sin + x2 * cos], axis=-1)

        qr = rot(q_ref[0, pl.ds(r0, c_len), :].astype(F32))
        kr = rot(k_ref[0, pl.ds(r0, c_len), :].astype(F32)) * (RET_DK ** -0.5)
        vc = v_ref[0, pl.ds(r0, c_len), :]
        s = _dot_nt(qr.astype(BF16), kr.astype(BF16)) * intra
        state = s_ref[...]
        o = _dot(s.astype(BF16), vc) + _dot((qr * qd).astype(BF16), state.astype(BF16))
        s_ref[...] = cd * state + _dot_tn((kr * kd).astype(BF16), vc)
        o = o * lax.rsqrt(jnp.mean(o * o, axis=-1, keepdims=True) + RMS_EPS)
        gate = g_ref[0, pl.ds(r0, c_len), :].astype(F32)
        o_ref[0, pl.ds(r0, c_len), :] = (gate * jax.nn.sigmoid(gate) * o).astype(o_ref.dtype)
        return carry

    lax.fori_loop(0, n_chunks, chunk, 0, unroll=8)


def _retention_core(h3):
    b, t, _ = h3.shape
    hh, dk, dv, c = RET_HEADS, RET_DK, RET_DV, RET_CHUNK
    half = dk // 2
    pos = jnp.arange(t, dtype=F32)
    inv_freq = RET_ROPE_BASE ** (-jnp.arange(half, dtype=F32) / half)
    ang = pos[:, None] * inv_freq[None, :]
    cos, sin = jnp.cos(ang), jnp.sin(ang)
    log_gamma = jnp.log(1.0 - 2.0 ** (-5.0 - jnp.arange(hh, dtype=F32)))
    i = jnp.arange(c)
    diff = i[:, None] - i[None, :]
    intra = jnp.where(diff >= 0, jnp.exp(log_gamma[:, None, None] * jnp.maximum(diff, 0)), 0.0)
    q_dec = jnp.exp(log_gamma[:, None] * (i + 1))[:, :, None]
    k_dec = jnp.exp(log_gamma[:, None] * (c - 1 - i))[:, :, None]
    chunk_dec = jnp.exp(log_gamma * c)
    k0 = hh
    v0 = 2 * hh * dk // dv
    g0 = v0 + hh
    return pl.pallas_call(
        functools.partial(_ret_kernel, n_chunks=t // c),
        grid=(b, hh),
        in_specs=[
            pl.BlockSpec(memory_space=pltpu.SMEM),
            pl.BlockSpec((1, t, dk), lambda bi, hi: (bi, 0, hi)),
            pl.BlockSpec((1, t, dk), lambda bi, hi: (bi, 0, k0 + hi)),
            pl.BlockSpec((1, t, dv), lambda bi, hi: (bi, 0, v0 + hi)),
            pl.BlockSpec((1, t, dv), lambda bi, hi: (bi, 0, g0 + hi)),
            _resident((t, half)),
            _resident((t, half)),
            pl.BlockSpec((1, c, c), lambda bi, hi: (hi, 0, 0)),
            pl.BlockSpec((1, c, 1), lambda bi, hi: (hi, 0, 0)),
            pl.BlockSpec((1, c, 1), lambda bi, hi: (hi, 0, 0)),
        ],
        out_specs=pl.BlockSpec((1, t, dv), lambda bi, hi: (bi, 0, hi)),
        out_shape=jax.ShapeDtypeStruct((b, t, hh * dv), BF16),
        scratch_shapes=[pltpu.VMEM((dk, dv), F32)],
        compiler_params=_cparams(("parallel", "arbitrary")),
    )(chunk_dec, h3, h3, h3, h3, cos, sin, intra, q_dec, k_dec)


def _k_headnorm(x, gain_row):
    klane = lax.broadcasted_iota(jnp.int32, x.shape, 1) < NSA_DH
    ms = jnp.sum(jnp.where(klane, x * x, 0.0), axis=-1, keepdims=True) * (1.0 / NSA_DH)
    return jnp.where(klane, x * lax.rsqrt(ms + RMS_EPS) * gain_row, x)


def _cmp_kernel(a_ref, pos_ref, w1_ref, w2_ref, kg_ref, o_ref, af_ref, *, t):
    n_blk = t // CMP_STRIDE
    af_ref[0:t, :] = a_ref[0].astype(F32)
    af_ref[t:t + CMP_LEN, :] = jnp.zeros((CMP_LEN, LANES), F32)
    blocks = [(af_ref[pl.ds(l, n_blk, stride=CMP_STRIDE), :] + pos_ref[l:l + 1, :]).astype(BF16)
              for l in range(CMP_LEN)]
    hid = jax.nn.gelu(_dot(jnp.concatenate(blocks, axis=1), w1_ref[...]), approximate=True)
    out = _dot(hid.astype(BF16), w2_ref[...])
    o_ref[0, 0] = _k_headnorm(out, kg_ref[...]).astype(o_ref.dtype)


def _compress(h3, pos, w1cat, w2cat, kgain_row):
    b, t, _ = h3.shape
    n_blk = t // CMP_STRIDE
    return pl.pallas_call(
        functools.partial(_cmp_kernel, t=t),
        grid=(b, NSA_GROUPS),
        in_specs=[
            pl.BlockSpec((1, t, LANES), lambda bi, gi: (bi, 0, NSA_KV0 + gi)),
            _resident((CMP_LEN, LANES)),
            _resident((CMP_LEN * LANES, 2 * CMP_HID)),
            _resident((2 * CMP_HID, LANES)),
            _resident((1, LANES)),
        ],
        out_specs=pl.BlockSpec((1, 1, n_blk, LANES), lambda bi, gi: (bi, gi, 0, 0)),
        out_shape=jax.ShapeDtypeStruct((b, NSA_GROUPS, n_blk, LANES), BF16),
        scratch_shapes=[pltpu.VMEM((t + CMP_LEN, LANES), F32)],
        compiler_params=_cparams(("parallel", "parallel")),
    )(h3, pos, w1cat, w2cat, kgain_row)


def _nsa_attn_kernel(q_ref, kvc_ref, kvs_ref, kvw_ref, gate_ref, qg_ref, kg_ref, bnd_ref, bw_ref, bc_ref,
                     ovt_ref, kflag_ref, o_ref, ks_ref, kw_ref, vst_ref, vwt_ref, sel_ref, *, t):
    hpg = NSA_HPG
    dh = NSA_DH
    qt = pl.program_id(2)
    q0 = pl.multiple_of(qt * TQ, TQ)
    lane = lax.broadcasted_iota(jnp.int32, (1, LANES), 1)
    klane = lane < dh
    ones_rows = lax.broadcasted_iota(jnp.int32, (LANES, LANES), 0) < dh

    @pl.when(qt == 0)
    def _():
        pad_rows = jnp.where(lane == PAD_LANE, 1.0, 0.0).astype(BF16)
        ks_ref[0:TQ, :] = jnp.broadcast_to(pad_rows, (TQ, LANES))
        kw_ref[0:WINDOW, :] = jnp.broadcast_to(pad_rows, (WINDOW, LANES))
        pad_col = jnp.where(ones_rows, 1.0, 0.0).astype(BF16)
        for i in range(TQ // LANES):
            vst_ref[:, i * LANES:(i + 1) * LANES] = pad_col
        for i in range(WINDOW // LANES):
            vwt_ref[:, i * LANES:(i + 1) * LANES] = pad_col

        def norm_block(c, carry):
            r0 = pl.multiple_of(c * LANES, LANES)
            for src, dst, dst_t, pad, gi in ((kvs_ref, ks_ref, vst_ref, TQ, 0), (kvw_ref, kw_ref, vwt_ref, WINDOW, 1)):
                x = _k_headnorm(src[0, pl.ds(r0, LANES), :].astype(F32), kg_ref[gi:gi + 1, :])
                flags = kflag_ref[pl.ds(r0, LANES), :] if gi == 0 else jnp.zeros((LANES, LANES), BF16)
                dst[pl.ds(pad + r0, LANES), :] = jnp.where(klane, x.astype(BF16), flags)
                dst_t[:, pl.ds(pad + r0, LANES)] = jnp.where(ones_rows, 1.0, x.T).astype(BF16)
            return carry

        lax.fori_loop(0, t // LANES, norm_block, 0, unroll=8)

    xq = q_ref[0].astype(F32)
    lane_q = lax.broadcasted_iota(jnp.int32, (1, hpg * dh), 1)
    xq2 = xq * xq
    qz = []
    for p in range(hpg):
        seg = (lane_q >= dh * p) & (lane_q < dh * (p + 1))
        ms = jnp.sum(jnp.where(seg, xq2, 0.0), axis=-1, keepdims=True) * (1.0 / dh)
        c0 = LANES * (p // 2)
        half = xq[:, c0:c0 + LANES] * lax.rsqrt(ms + RMS_EPS) * qg_ref[:, c0:c0 + LANES]
        if p % 2 == 1:
            half = pltpu.roll(half, dh, 1)
        qz.append(jnp.where(klane, half, 0.0))

    def stacked_q(flag_lanes):
        return jnp.concatenate([(qp + flag_lanes).astype(BF16) for qp in qz], axis=0)

    ww = WINDOW + TQ
    s_w = _dot_nt(kw_ref[pl.ds(q0, ww), :], stacked_q(jnp.where(lane == PAD_LANE, NEG, 0.0)))
    s = jnp.concatenate([s_w[:, p * TQ:(p + 1) * TQ] + bw_ref[0, p] for p in range(hpg)], axis=1)
    m_win = jnp.max(s, axis=0, keepdims=True)
    acc_win = _dot(vwt_ref[:, pl.ds(q0, ww)], jnp.exp2(s - m_win).astype(BF16))

    per_tile = TQ // CMP_STRIDE
    off = pl.multiple_of((LANES - per_tile) - qt * per_tile, per_tile)
    blk_i = lax.broadcasted_iota(jnp.int32, (LANES, TQ), 0)
    valid_c = (CMP_STRIDE * blk_i + (CMP_LEN - 1)) <= (q0 + lax.broadcasted_iota(jnp.int32, (LANES, TQ), 1))
    kvc = kvc_ref[0, 0]
    s_c = _dot_nt(kvc, stacked_q(0.0))
    pcs = []
    for p in range(hpg):
        s = jnp.where(valid_c, s_c[:, p * TQ:(p + 1) * TQ] + bc_ref[0, p, pl.ds(off, LANES), :], NEG)
        m = jnp.max(s, axis=0, keepdims=True)
        e = jnp.where(valid_c, jnp.exp2(s - m), 0.0)
        l = jnp.sum(e, axis=0, keepdims=True)
        pcs.append(e / jnp.where(l > 0.0, l, 1.0))
    vc_t = jnp.where(ones_rows, 1.0, kvc.astype(F32).T).astype(BF16)
    o_cmp = _dot(vc_t, jnp.concatenate(pcs, axis=1).astype(BF16))
    psum = pcs[0]
    for p in range(1, hpg):
        psum = psum + pcs[p]
    p_hi = psum.astype(BF16)
    p_lo = (psum - p_hi.astype(F32)).astype(BF16)
    n_slc = t // SLC_LEN
    imp_t = (_dot(ovt_ref[...], p_hi) + _dot(ovt_ref[...], p_lo))[:n_slc, :]

    jrow = lax.broadcasted_iota(jnp.int32, (n_slc, TQ), 0)
    cur = (q0 + lax.broadcasted_iota(jnp.int32, (n_slc, TQ), 1)) // SLC_LEN
    forced = (jrow == 0) | (jrow == cur) | (jrow == cur - 1)
    vals = jnp.where(forced, SEL_BIG, jnp.where(jrow <= cur, imp_t, -SEL_BIG))
    terms = []
    for jp in range(n_slc):
        row = vals[jp:jp + 1, :]
        ahead = (row > vals) | ((row == vals) & (jrow > jp))
        terms.append(jnp.where(ahead, 1.0, 0.0))
    while len(terms) > 1:
        terms = [a + b for a, b in zip(terms[0::2], terms[1::2])]
    cnt = terms[0]

    far_end = (qt - 1) * TQ
    drop_nd = jnp.where(cnt < float(N_SEL), 0.0, NEG)
    drop_far = jnp.where(jrow * SLC_LEN < far_end, drop_nd, NEG)

    def flag_lanes(drop):
        rows = [jnp.zeros((FLAG0, TQ), F32), drop, jnp.full((PAD_LANE - FLAG0 - n_slc + 8, TQ), NEG, F32),
                jnp.zeros((LANES - PAD_LANE - 8, TQ), F32)]
        return jnp.concatenate(rows, axis=0).T

    ck = CK
    q_nd = stacked_q(flag_lanes(drop_nd))
    q_far = stacked_q(flag_lanes(drop_far))
    n_far = jnp.maximum(far_end + ck - 1, 0) // ck

    s_nd = _dot_nt(ks_ref[pl.ds(q0, 2 * TQ), :], q_nd)
    s_nd = jnp.concatenate([s_nd[:, p * TQ:(p + 1) * TQ] + bnd_ref[0, p] for p in range(hpg)], axis=1)

    def select_branch(k):
        scores = [s_nd]
        values = [vst_ref[:, pl.ds(q0, 2 * TQ)]]
        for c in range(k):
            scores.append(_dot_nt(ks_ref[TQ + c * ck:TQ + (c + 1) * ck, :], q_far))
            values.append(vst_ref[:, TQ + c * ck:TQ + (c + 1) * ck])
        m = jnp.max(scores[0], axis=0, keepdims=True)
        for s in scores[1:]:
            m = jnp.maximum(m, jnp.max(s, axis=0, keepdims=True))
        acc = _dot(values[0], jnp.exp2(scores[0] - m).astype(BF16))
        for s, v in zip(scores[1:], values[1:]):
            acc = acc + _dot(v, jnp.exp2(s - m).astype(BF16))
        sel_ref[...] = acc

    for k in range((t - 2 * TQ) // ck + 1):
        pl.when(n_far == k)(functools.partial(select_branch, k))
    acc_sel = sel_ref[...]

    gl_t = jax.nn.sigmoid(gate_ref[0].astype(F32)).T
    outs = []
    for p in range(hpg):
        sl = slice(p * TQ, (p + 1) * TQ)
        g_c = gl_t[p:p + 1, :]
        g_s = gl_t[hpg + p:hpg + p + 1, :] / acc_sel[0:1, sl]
        g_w = gl_t[2 * hpg + p:2 * hpg + p + 1, :] / acc_win[0:1, sl]
        outs.append(g_c * o_cmp[dh:, sl] + g_s * acc_sel[dh:, sl] + g_w * acc_win[dh:, sl])
    o_ref[0] = jnp.concatenate(outs, axis=0).T.astype(o_ref.dtype)


def _nsa_attention(h3, kvc, qgain, kgain, bias_nd, bias_w, bias_cmp, overlap_t, kflag):
    b, t, _ = h3.shape
    g, hpg = NSA_GROUPS, NSA_HPG
    n_blk = t // CMP_STRIDE
    qw = hpg * NSA_DH
    sel0, win0, gate0 = NSA_KV0 + g, NSA_KV0 + 2 * g, NSA_KV0 + 3 * g
    return pl.pallas_call(
        functools.partial(_nsa_attn_kernel, t=t),
        grid=(g, b, t // TQ),
        in_specs=[
            pl.BlockSpec((1, TQ, qw), lambda gi, bi, qi: (bi, qi, gi)),
            pl.BlockSpec((1, 1, n_blk, LANES), lambda gi, bi, qi: (bi, gi, 0, 0)),
            pl.BlockSpec((1, t, LANES), lambda gi, bi, qi: (bi, 0, sel0 + gi)),
            pl.BlockSpec((1, t, LANES), lambda gi, bi, qi: (bi, 0, win0 + gi)),
            pl.BlockSpec((1, TQ, LANES), lambda gi, bi, qi: (bi, qi, gate0 + gi)),
            pl.BlockSpec((1, qw), lambda gi, bi, qi: (0, 0)),
            pl.BlockSpec((2, LANES), lambda gi, bi, qi: (0, 0)),
            pl.BlockSpec((1, hpg, 2 * TQ, TQ), lambda gi, bi, qi: (gi, 0, 0, 0)),
            pl.BlockSpec((1, hpg, WINDOW + TQ, TQ), lambda gi, bi, qi: (gi, 0, 0, 0)),
            pl.BlockSpec((1, hpg, 2 * LANES, TQ), lambda gi, bi, qi: (gi, 0, 0, 0)),
            pl.BlockSpec((LANES, LANES), lambda gi, bi, qi: (0, 0)),
            pl.BlockSpec((t, LANES), lambda gi, bi, qi: (0, 0)),
        ],
        out_specs=pl.BlockSpec((1, TQ, qw), lambda gi, bi, qi: (bi, qi, gi)),
        out_shape=jax.ShapeDtypeStruct((b, t, NSA_HEADS * NSA_DH), BF16),
        scratch_shapes=[pltpu.VMEM((TQ + t, LANES), BF16), pltpu.VMEM((WINDOW + t, LANES), BF16),
                        pltpu.VMEM((LANES, TQ + t), BF16), pltpu.VMEM((LANES, WINDOW + t), BF16),
                        pltpu.VMEM((LANES, hpg * TQ), F32)],
        compiler_params=_cparams(("arbitrary", "arbitrary", "arbitrary")),
    )(h3, kvc, h3, h3, h3, qgain, kgain, bias_nd, bias_w, bias_cmp, overlap_t, kflag)


def _bias_kernel(tbl_ref, nd_ref, w_ref, bc_ref, ond_ref, ow_ref, obc_ref):
    h = pl.program_id(0)
    far = tbl_ref[NUM_BUCKETS - 1, h]
    nd = nd_ref[...]
    w = w_ref[...]
    bc = bc_ref[...]
    o_nd = jnp.full(nd.shape, NEG, F32)
    o_w = jnp.full(w.shape, NEG, F32)
    o_bc = jnp.zeros(bc.shape, F32)
    for bucket in range(NUM_BUCKETS):
        v = tbl_ref[bucket, h]
        o_nd = jnp.where(nd == bucket, (v - far) * LOG2E, o_nd)
        o_w = jnp.where(w == bucket, (v - far) * LOG2E, o_w)
        o_bc = jnp.where(bc == bucket, v * LOG2E, o_bc)
    ond_ref[0] = o_nd
    ow_ref[0] = o_w
    obc_ref[0] = o_bc


def _bias_tiles(rel_bias, nd_bucket, w_bucket, bc_bucket):
    n_heads = rel_bias.shape[1]
    tables = (nd_bucket, w_bucket, bc_bucket)
    return pl.pallas_call(
        _bias_kernel,
        grid=(n_heads,),
        in_specs=[pl.BlockSpec(memory_space=pltpu.SMEM)] + [_resident(a.shape) for a in tables],
        out_specs=[pl.BlockSpec((1,) + a.shape, lambda h: (h, 0, 0)) for a in tables],
        out_shape=[jax.ShapeDtypeStruct((n_heads,) + a.shape, F32) for a in tables],
        compiler_params=_cparams(("parallel",)),
    )(rel_bias.astype(F32), *[jnp.asarray(a, jnp.int32) for a in tables])


def _t5_bucket_np(dist):
    n = np.maximum(dist, 0)
    max_exact = NUM_BUCKETS // 2
    nf = np.maximum(n, max_exact).astype(np.float32)
    large = max_exact + (np.log(nf / max_exact) / np.float32(np.log(MAX_DISTANCE / max_exact))
                         * (NUM_BUCKETS - max_exact)).astype(np.int32)
    large = np.minimum(large, NUM_BUCKETS - 1)
    return np.where(n < max_exact, n, large)


def _nsa_constants(t):
    n_cmp = (t - CMP_LEN) // CMP_STRIDE + 1
    n_slc = t // SLC_LEN
    n_blk = t // CMP_STRIDE
    per_tile = TQ // CMP_STRIDE
    assert n_blk <= LANES and per_tile * (t // TQ - 1) <= LANES - per_tile
    iq = np.arange(TQ)
    rel = iq[None, :] - iq[:, None]
    assert _t5_bucket_np(np.array([TQ + 1]))[0] == NUM_BUCKETS - 1
    dist_nd = np.concatenate([TQ + rel, rel], axis=0)
    nd_bucket = np.where(dist_nd >= 0, _t5_bucket_np(dist_nd), -1)
    dist_w = np.arange(TQ)[None, :] + WINDOW - np.arange(WINDOW + TQ)[:, None]
    w_bucket = np.where((dist_w >= 0) & (dist_w < WINDOW), _t5_bucket_np(dist_w), -1)
    blk_off = np.arange(LANES) - (LANES - per_tile)
    bc_bucket = _t5_bucket_np(iq[None, :] - CMP_STRIDE * blk_off[:, None] - (CMP_LEN - 1))
    bc_bucket = np.concatenate([bc_bucket, bc_bucket], axis=0)
    ci = np.arange(n_blk) * CMP_STRIDE
    sj = np.arange(n_slc) * SLC_LEN
    ov = ((ci[:, None] < sj[None, :] + SLC_LEN) & (ci[:, None] + CMP_LEN > sj[None, :])).astype(np.float32)
    ov[n_cmp:, :] = 0.0
    overlap_t = np.zeros((LANES, LANES), np.float32)
    overlap_t[:n_slc, :n_blk] = ov.T
    assert FLAG0 + n_slc <= PAD_LANE
    kflag = np.zeros((t, LANES), np.float32)
    kflag[np.arange(t), FLAG0 + np.arange(t) // SLC_LEN] = 1.0
    return nd_bucket, w_bucket, bc_bucket, overlap_t, kflag


def _nsa_w_in_columns():
    kv0 = NSA_HEADS * NSA_DH
    kvw = NSA_GROUPS * NSA_DH
    cols = list(range(kv0))
    for br in range(3):
        for g in range(NSA_GROUPS):
            k_src = kv0 + (2 * br) * kvw + g * NSA_DH
            v_src = kv0 + (2 * br + 1) * kvw + g * NSA_DH
            cols += list(range(k_src, k_src + NSA_DH)) + list(range(v_src, v_src + NSA_DH))
    gate0 = kv0 + 6 * kvw
    for g in range(NSA_GROUPS):
        blk = [-1] * LANES
        for br in range(3):
            for p in range(NSA_HPG):
                blk[br * NSA_HPG + p] = gate0 + br * NSA_HEADS + g * NSA_HPG + p
        cols += blk
    return np.asarray(cols, np.int32)


def _retention_layer(x2, b, t, norm_g, w_in, w_out):
    h = _norm_matmul(x2, norm_g, w_in.astype(BF16), tn=1024)
    o = _retention_core(h.reshape(b, t, -1))
    return o.reshape(b * t, -1), w_out.astype(BF16)


def _nsa_layer(x2, b, t, norm_g, w_in, w_out, q_gain, k_gain, cmp_pos, cmp_w1, cmp_w2, bias_nd, bias_w,
               bias_cmp, overlap_t, kflag):
    cols = _nsa_w_in_columns()
    w_k = jnp.where(cols[None, :] >= 0, jnp.take(w_in, np.maximum(cols, 0), axis=1), 0.0).astype(BF16)
    h3 = _norm_matmul(x2, norm_g, w_k, tn=1024).reshape(b, t, NSA_COLS)

    dh = NSA_DH
    w1 = cmp_w1.reshape(2, CMP_LEN, dh, CMP_HID)
    zero1 = jnp.zeros((CMP_LEN, dh, CMP_HID), F32)
    w1cat = jnp.concatenate([jnp.concatenate([w1[0], zero1], axis=-1),
                             jnp.concatenate([zero1, w1[1]], axis=-1)], axis=1).astype(BF16)
    zero2 = jnp.zeros((CMP_HID, dh), F32)
    w2cat = jnp.concatenate([jnp.concatenate([cmp_w2[0], zero2], axis=-1),
                             jnp.concatenate([zero2, cmp_w2[1]], axis=-1)], axis=0).astype(BF16)
    pos = jnp.concatenate([cmp_pos[0], cmp_pos[1]], axis=-1)
    ones = jnp.ones((dh,), F32)
    kvc = _compress(h3, pos, w1cat.reshape(CMP_LEN * LANES, 2 * CMP_HID), w2cat,
                    jnp.concatenate([k_gain[0], ones])[None, :])

    qgain = jnp.tile(q_gain * (dh ** -0.5 * LOG2E), NSA_HPG)[None, :]
    kgain = jnp.stack([jnp.concatenate([k_gain[1], ones]), jnp.concatenate([k_gain[2], ones])])
    o = _nsa_attention(h3, kvc, qgain, kgain, bias_nd, bias_w, bias_cmp, overlap_t, kflag)
    return o.reshape(b * t, -1), w_out.astype(BF16)


def kernel(x, ffn1_norm, ffn1_w_gu, ffn1_w_down, mix_norm, ffn2_norm, ffn2_w_gu, ffn2_w_down,
           ret_w_in, ret_w_out, nsa_w_in, nsa_w_out, nsa_q_gain, nsa_k_gain,
           nsa_cmp_pos, nsa_cmp_w1, nsa_cmp_w2, rel_bias):
    b, t, d = x.shape
    depth = ffn1_norm.shape[0]

    def ffn_weights(w_gu, w_down):
        return w_gu.astype(BF16), w_down.astype(BF16)

    nd_bucket, w_bucket, bc_bucket, overlap_t, kflag = _nsa_constants(t)
    bias_nd, bias_w, bias_cmp = _bias_tiles(rel_bias, nd_bucket, w_bucket, bc_bucket)
    bias_nd = bias_nd.reshape(NSA_GROUPS, NSA_HPG, 2 * TQ, TQ)
    bias_w = bias_w.reshape(NSA_GROUPS, NSA_HPG, WINDOW + TQ, TQ)
    bias_cmp = bias_cmp.reshape(NSA_GROUPS, NSA_HPG, 2 * LANES, TQ)
    overlap_t = jnp.asarray(overlap_t, BF16)
    kflag = jnp.asarray(kflag, BF16)

    x2 = x.reshape(b * t, d)
    for layer in range(depth):
        x2 = _ffn(x2, ffn1_norm[layer][None, :], *ffn_weights(ffn1_w_gu[layer], ffn1_w_down[layer]))
        j = layer // 2
        if layer % 2 == 0:
            o2, w_out = _retention_layer(x2, b, t, mix_norm[layer][None, :], ret_w_in[j], ret_w_out[j])
        else:
            o2, w_out = _nsa_layer(x2, b, t, mix_norm[layer][None, :], nsa_w_in[j], nsa_w_out[j], nsa_q_gain[j],
                                   nsa_k_gain[j], nsa_cmp_pos[j], nsa_cmp_w1[j], nsa_cmp_w2[j], bias_nd, bias_w,
                                   bias_cmp, overlap_t, kflag)
        x2 = _proj_ffn(o2, w_out, x2, ffn2_norm[layer][None, :],
                       *ffn_weights(ffn2_w_gu[layer], ffn2_w_down[layer]))
    return x2.reshape(b, t, d)
```

```python
import functools

import numpy as np
import jax
import jax.numpy as jnp
from jax import lax
from jax.experimental import pallas as pl
from jax.experimental.pallas import tpu as pltpu

F32 = jnp.float32
BF16 = jnp.bfloat16

RMS_EPS = 1e-6
NEG = -1e30
SEL_BIG = 1e9
LOG2E = 1.4426950408889634

LANES = 128
VMEM_LIMIT = 56 * 1024 * 1024

RET_HEADS = 4
RET_DK = 256
RET_DV = 512
RET_CHUNK = 128
RET_ROPE_BASE = 10000.0

NSA_HEADS = 16
NSA_GROUPS = 4
NSA_HPG = 4
NSA_DH = 64
CMP_LEN = 32
CMP_STRIDE = 16
CMP_HID = 256
SLC_LEN = 64
N_SEL = 8
WINDOW = 512
NUM_BUCKETS = 32
MAX_DISTANCE = 128
TQ = 256
CK = 256
FLAG0 = 64
PAD_LANE = 96
NSA_COLS = 1024 + 3 * 4 * 128 + 4 * 128
NSA_KV0 = 1024 // 128


def _dot(a, b):
    return jnp.dot(a, b, preferred_element_type=F32)


def _dot_nt(a, b):
    return lax.dot_general(a, b, (((1,), (1,)), ((), ())), preferred_element_type=F32)


def _dot_tn(a, b):
    return lax.dot_general(a, b, (((0,), (0,)), ((), ())), preferred_element_type=F32)


def _rms(x, g):
    ms = jnp.mean(x * x, axis=-1, keepdims=True)
    return x * lax.rsqrt(ms + RMS_EPS) * g


def _cparams(sem):
    return pltpu.CompilerParams(dimension_semantics=sem, vmem_limit_bytes=VMEM_LIMIT)


def _resident(shape):
    n = len(shape)
    return pl.BlockSpec(shape, lambda *_: (0,) * n, pipeline_mode=pl.Buffered(1))


def _ffn_kernel(x_ref, g_ref, wgu_ref, wd_ref, o_ref):
    d_ff = wd_ref.shape[0]
    x = x_ref[...]
    xn = _rms(x, g_ref[...]).astype(BF16)
    a = _dot(xn, wgu_ref[:, :d_ff])
    b = _dot(xn, wgu_ref[:, d_ff:])
    h = (a * jax.nn.sigmoid(a) * b).astype(BF16)
    o_ref[...] = x + 0.5 * _dot(h, wd_ref[...])


def _ffn(x2, g, wgu, wd, tm=512):
    n, d = x2.shape
    return pl.pallas_call(
        _ffn_kernel,
        grid=(n // tm,),
        in_specs=[
            pl.BlockSpec((tm, d), lambda i: (i, 0)),
            _resident((1, d)),
            _resident(wgu.shape),
            _resident(wd.shape),
        ],
        out_specs=pl.BlockSpec((tm, d), lambda i: (i, 0)),
        out_shape=jax.ShapeDtypeStruct((n, d), F32),
        compiler_params=_cparams(("parallel",)),
    )(x2, g, wgu, wd)


def _ffn_proj_kernel(x_ref, g_ref, wgu_ref, wd_ref, g2_ref, w2_ref, y_ref, h_ref, *, tn):
    d_ff = wd_ref.shape[0]
    x = x_ref[...]
    xn = _rms(x, g_ref[...]).astype(BF16)
    a = _dot(xn, wgu_ref[:, :d_ff])
    b = _dot(xn, wgu_ref[:, d_ff:])
    h = (a * jax.nn.sigmoid(a) * b).astype(BF16)
    y = x + 0.5 * _dot(h, wd_ref[...])
    y_ref[...] = y
    yn = _rms(y, g2_ref[...]).astype(BF16)
    for j in range(w2_ref.shape[1] // tn):
        h_ref[:, j * tn:(j + 1) * tn] = _dot(yn, w2_ref[:, j * tn:(j + 1) * tn]).astype(h_ref.dtype)


def _ffn_proj(x2, g, wgu, wd, g2, w2, tn, tm=512):
    n, d = x2.shape
    nout = w2.shape[1]
    return pl.pallas_call(
        functools.partial(_ffn_proj_kernel, tn=tn),
        grid=(n // tm,),
        in_specs=[
            pl.BlockSpec((tm, d), lambda i: (i, 0)),
            _resident((1, d)),
            _resident(wgu.shape),
            _resident(wd.shape),
            _resident((1, d)),
            _resident(w2.shape),
        ],
        out_specs=[pl.BlockSpec((tm, d), lambda i: (i, 0)), pl.BlockSpec((tm, nout), lambda i: (i, 0))],
        out_shape=[jax.ShapeDtypeStruct((n, d), F32), jax.ShapeDtypeStruct((n, nout), BF16)],
        compiler_params=_cparams(("parallel",)),
    )(x2, g, wgu, wd, g2, w2)


def _proj_ffn_kernel(o_ref, wo_ref, x_ref, g_ref, wgu_ref, wd_ref, y_ref):
    d_ff = wd_ref.shape[0]
    x = x_ref[...] + _dot(o_ref[...], wo_ref[...])
    xn = _rms(x, g_ref[...]).astype(BF16)
    a = _dot(xn, wgu_ref[:, :d_ff])
    b = _dot(xn, wgu_ref[:, d_ff:])
    h = (a * jax.nn.sigmoid(a) * b).astype(BF16)
    y_ref[...] = x + 0.5 * _dot(h, wd_ref[...])


def _proj_ffn(o2, wo, x2, g, wgu, wd, tm=512):
    n, d = x2.shape
    k = o2.shape[1]
    return pl.pallas_call(
        _proj_ffn_kernel,
        grid=(n // tm,),
        in_specs=[
            pl.BlockSpec((tm, k), lambda i: (i, 0)),
            _resident(wo.shape),
            pl.BlockSpec((tm, d), lambda i: (i, 0)),
            _resident((1, d)),
            _resident(wgu.shape),
            _resident(wd.shape),
        ],
        out_specs=pl.BlockSpec((tm, d), lambda i: (i, 0)),
        out_shape=jax.ShapeDtypeStruct((n, d), F32),
        compiler_params=_cparams(("parallel",)),
    )(o2, wo, x2, g, wgu, wd)


def _norm_matmul_kernel(x_ref, g_ref, w_ref, o_ref, *, tn):
    xn = _rms(x_ref[...], g_ref[...]).astype(BF16)
    for j in range(w_ref.shape[1] // tn):
        o_ref[:, j * tn:(j + 1) * tn] = _dot(xn, w_ref[:, j * tn:(j + 1) * tn]).astype(o_ref.dtype)


def _norm_matmul(x2, g, w, tn, tm=512):
    n, d = x2.shape
    nout = w.shape[1]
    return pl.pallas_call(
        functools.partial(_norm_matmul_kernel, tn=tn),
        grid=(n // tm,),
        in_specs=[pl.BlockSpec((tm, d), lambda i: (i, 0)), _resident((1, d)), _resident((d, nout))],
        out_specs=pl.BlockSpec((tm, nout), lambda i: (i, 0)),
        out_shape=jax.ShapeDtypeStruct((n, nout), BF16),
        compiler_params=_cparams(("parallel",)),
    )(x2, g, w)


def _ret_kernel(cd_ref, q_ref, k_ref, v_ref, g_ref, cos_ref, sin_ref, intra_ref, qd_ref, kd_ref,
                o_ref, s_ref, *, n_chunks):
    c_len = RET_CHUNK
    half = RET_DK // 2
    cd = cd_ref[pl.program_id(1)]
    intra = intra_ref[0]
    qd = qd_ref[0]
    kd = kd_ref[0]
    s_ref[...] = jnp.zeros_like(s_ref)

    def chunk(c, carry):
        r0 = pl.multiple_of(c * c_len, c_len)
        cos = cos_ref[pl.ds(r0, c_len), :]
        sin = sin_ref[pl.ds(r0, c_len), :]

        def rot(x):
            x1, x2 = x[:, :half], x[:, half:]
            return jnp.concatenate([x1 * cos - x2 * sin, x1 * sin + x2 * cos], axis=-1)

        qr = rot(q_ref[0, pl.ds(r0, c_len), :].astype(F32))
        kr = rot(k_ref[0, pl.ds(r0, c_len), :].astype(F32)) * (RET_DK ** -0.5)
        vc = v_ref[0, pl.ds(r0, c_len), :]
        s = _dot_nt(qr.astype(BF16), kr.astype(BF16)) * intra
        state = s_ref[...]
        o = _dot(s.astype(BF16), vc) + _dot((qr * qd).astype(BF16), state.astype(BF16))
        s_ref[...] = cd * state + _dot_tn((kr * kd).astype(BF16), vc)
        o = o * lax.rsqrt(jnp.mean(o * o, axis=-1, keepdims=True) + RMS_EPS)
        gate = g_ref[0, pl.ds(r0, c_len), :].astype(F32)
        o_ref[0, pl.ds(r0, c_len), :] = (gate * jax.nn.sigmoid(gate) * o).astype(o_ref.dtype)
        return carry

    lax.fori_loop(0, n_chunks, chunk, 0, unroll=8)


def _retention_core(h3):
    b, t, _ = h3.shape
    hh, dk, dv, c = RET_HEADS, RET_DK, RET_DV, RET_CHUNK
    half = dk // 2
    pos = jnp.arange(t, dtype=F32)
    inv_freq = RET_ROPE_BASE ** (-jnp.arange(half, dtype=F32) / half)
    ang = pos[:, None] * inv_freq[None, :]
    cos, sin = jnp.cos(ang), jnp.sin(ang)
    log_gamma = jnp.log(1.0 - 2.0 ** (-5.0 - jnp.arange(hh, dtype=F32)))
    i = jnp.arange(c)
    diff = i[:, None] - i[None, :]
    intra = jnp.where(diff >= 0, jnp.exp(log_gamma[:, None, None] * jnp.maximum(diff, 0)), 0.0)
    q_dec = jnp.exp(log_gamma[:, None] * (i + 1))[:, :, None]
    k_dec = jnp.exp(log_gamma[:, None] * (c - 1 - i))[:, :, None]
    chunk_dec = jnp.exp(log_gamma * c)
    k0 = hh
    v0 = 2 * hh * dk // dv
    g0 = v0 + hh
    return pl.pallas_call(
        functools.partial(_ret_kernel, n_chunks=t // c),
        grid=(b, hh),
        in_specs=[
            pl.BlockSpec(memory_space=pltpu.SMEM),
            pl.BlockSpec((1, t, dk), lambda bi, hi: (bi, 0, hi)),
            pl.BlockSpec((1, t, dk), lambda bi, hi: (bi, 0, k0 + hi)),
            pl.BlockSpec((1, t, dv), lambda bi, hi: (bi, 0, v0 + hi)),
            pl.BlockSpec((1, t, dv), lambda bi, hi: (bi, 0, g0 + hi)),
            _resident((t, half)),
            _resident((t, half)),
            pl.BlockSpec((1, c, c), lambda bi, hi: (hi, 0, 0)),
            pl.BlockSpec((1, c, 1), lambda bi, hi: (hi, 0, 0)),
            pl.BlockSpec((1, c, 1), lambda bi, hi: (hi, 0, 0)),
        ],
        out_specs=pl.BlockSpec((1, t, dv), lambda bi, hi: (bi, 0, hi)),
        out_shape=jax.ShapeDtypeStruct((b, t, hh * dv), BF16),
        scratch_shapes=[pltpu.VMEM((dk, dv), F32)],
        compiler_params=_cparams(("parallel", "arbitrary")),
    )(chunk_dec, h3, h3, h3, h3, cos, sin, intra, q_dec, k_dec)


def _k_headnorm(x, gain_row):
    klane = lax.broadcasted_iota(jnp.int32, x.shape, 1) < NSA_DH
    ms = jnp.sum(jnp.where(klane, x * x, 0.0), axis=-1, keepdims=True) * (1.0 / NSA_DH)
    return jnp.where(klane, x * lax.rsqrt(ms + RMS_EPS) * gain_row, x)


def _cmp_kernel(a_ref, pos_ref, w1_ref, w2_ref, kg_ref, o_ref, af_ref, *, t):
    n_blk = t // CMP_STRIDE
    af_ref[0:t, :] = a_ref[0].astype(F32)
    af_ref[t:t + CMP_LEN, :] = jnp.zeros((CMP_LEN, LANES), F32)
    blocks = [(af_ref[pl.ds(l, n_blk, stride=CMP_STRIDE), :] + pos_ref[l:l + 1, :]).astype(BF16)
              for l in range(CMP_LEN)]
    hid = jax.nn.gelu(_dot(jnp.concatenate(blocks, axis=1), w1_ref[...]), approximate=True)
    out = _dot(hid.astype(BF16), w2_ref[...])
    o_ref[0, 0] = _k_headnorm(out, kg_ref[...]).astype(o_ref.dtype)


def _compress(h3, pos, w1cat, w2cat, kgain_row):
    b, t, _ = h3.shape
    n_blk = t // CMP_STRIDE
    return pl.pallas_call(
        functools.partial(_cmp_kernel, t=t),
        grid=(b, NSA_GROUPS),
        in_specs=[
            pl.BlockSpec((1, t, LANES), lambda bi, gi: (bi, 0, NSA_KV0 + gi)),
            _resident((CMP_LEN, LANES)),
            _resident((CMP_LEN * LANES, 2 * CMP_HID)),
            _resident((2 * CMP_HID, LANES)),
            _resident((1, LANES)),
        ],
        out_specs=pl.BlockSpec((1, 1, n_blk, LANES), lambda bi, gi: (bi, gi, 0, 0)),
        out_shape=jax.ShapeDtypeStruct((b, NSA_GROUPS, n_blk, LANES), BF16),
        scratch_shapes=[pltpu.VMEM((t + CMP_LEN, LANES), F32)],
        compiler_params=_cparams(("parallel", "parallel")),
    )(h3, pos, w1cat, w2cat, kgain_row)


def _nsa_attn_kernel(q_ref, kvc_ref, kvs_ref, kvw_ref, gate_ref, qg_ref, kg_ref, bnd_ref, bw_ref, bc_ref,
                     ovt_ref, kflag_ref, o_ref, ks_ref, kw_ref, vst_ref, vwt_ref, sel_ref, *, t):
    hpg = NSA_HPG
    dh = NSA_DH
    qt = pl.program_id(2)
    q0 = pl.multiple_of(qt * TQ, TQ)
    lane = lax.broadcasted_iota(jnp.int32, (1, LANES), 1)
    klane = lane < dh
    ones_rows = lax.broadcasted_iota(jnp.int32, (LANES, LANES), 0) < dh

    @pl.when(qt == 0)
    def _():
        pad_rows = jnp.where(lane == PAD_LANE, 1.0, 0.0).astype(BF16)
        ks_ref[0:TQ, :] = jnp.broadcast_to(pad_rows, (TQ, LANES))
        kw_ref[0:WINDOW, :] = jnp.broadcast_to(pad_rows, (WINDOW, LANES))
        pad_col = jnp.where(ones_rows, 1.0, 0.0).astype(BF16)
        for i in range(TQ // LANES):
            vst_ref[:, i * LANES:(i + 1) * LANES] = pad_col
        for i in range(WINDOW // LANES):
            vwt_ref[:, i * LANES:(i + 1) * LANES] = pad_col

        def norm_block(c, carry):
            r0 = pl.multiple_of(c * LANES, LANES)
            for src, dst, dst_t, pad, gi in ((kvs_ref, ks_ref, vst_ref, TQ, 0), (kvw_ref, kw_ref, vwt_ref, WINDOW, 1)):
                x = _k_headnorm(src[0, pl.ds(r0, LANES), :].astype(F32), kg_ref[gi:gi + 1, :])
                flags = kflag_ref[pl.ds(r0, LANES), :] if gi == 0 else jnp.zeros((LANES, LANES), BF16)
                dst[pl.ds(pad + r0, LANES), :] = jnp.where(klane, x.astype(BF16), flags)
                dst_t[:, pl.ds(pad + r0, LANES)] = jnp.where(ones_rows, 1.0, x.T).astype(BF16)
            return carry

        lax.fori_loop(0, t // LANES, norm_block, 0, unroll=8)

    xq = q_ref[0].astype(F32)
    lane_q = lax.broadcasted_iota(jnp.int32, (1, hpg * dh), 1)
    xq2 = xq * xq
    qz = []
    for p in range(hpg):
        seg = (lane_q >= dh * p) & (lane_q < dh * (p + 1))
        ms = jnp.sum(jnp.where(seg, xq2, 0.0), axis=-1, keepdims=True) * (1.0 / dh)
        c0 = LANES * (p // 2)
        half = xq[:, c0:c0 + LANES] * lax.rsqrt(ms + RMS_EPS) * qg_ref[:, c0:c0 + LANES]
        if p % 2 == 1:
            half = pltpu.roll(half, dh, 1)
        qz.append(jnp.where(klane, half, 0.0))

    def stacked_q(flag_lanes):
        return jnp.concatenate([(qp + flag_lanes).astype(BF16) for qp in qz], axis=0)

    ww = WINDOW + TQ
    s_w = _dot_nt(kw_ref[pl.ds(q0, ww), :], stacked_q(jnp.where(lane == PAD_LANE, NEG, 0.0)))
    s = jnp.concatenate([s_w[:, p * TQ:(p + 1) * TQ] + bw_ref[0, p] for p in range(hpg)], axis=1)
    m_win = jnp.max(s, axis=0, keepdims=True)
    acc_win = _dot(vwt_ref[:, pl.ds(q0, ww)], jnp.exp2(s - m_win).astype(BF16))

    per_tile = TQ // CMP_STRIDE
    off = pl.multiple_of((LANES - per_tile) - qt * per_tile, per_tile)
    blk_i = lax.broadcasted_iota(jnp.int32, (LANES, TQ), 0)
    valid_c = (CMP_STRIDE * blk_i + (CMP_LEN - 1)) <= (q0 + lax.broadcasted_iota(jnp.int32, (LANES, TQ), 1))
    kvc = kvc_ref[0, 0]
    s_c = _dot_nt(kvc, stacked_q(0.0))
    pcs = []
    for p in range(hpg):
        s = jnp.where(valid_c, s_c[:, p * TQ:(p + 1) * TQ] + bc_ref[0, p, pl.ds(off, LANES), :], NEG)
        m = jnp.max(s, axis=0, keepdims=True)
        e = jnp.where(valid_c, jnp.exp2(s - m), 0.0)
        l = jnp.sum(e, axis=0, keepdims=True)
        pcs.append(e / jnp.where(l > 0.0, l, 1.0))
    vc_t = jnp.where(ones_rows, 1.0, kvc.astype(F32).T).astype(BF16)
    o_cmp = _dot(vc_t, jnp.concatenate(pcs, axis=1).astype(BF16))
    psum = pcs[0]
    for p in range(1, hpg):
        psum = psum + pcs[p]
    p_hi = psum.astype(BF16)
    p_lo = (psum - p_hi.astype(F32)).astype(BF16)
    n_slc = t // SLC_LEN
    imp_t = (_dot(ovt_ref[...], p_hi) + _dot(ovt_ref[...], p_lo))[:n_slc, :]

    jrow = lax.broadcasted_iota(jnp.int32, (n_slc, TQ), 0)
    cur = (q0 + lax.broadcasted_iota(jnp.int32, (n_slc, TQ), 1)) // SLC_LEN
    forced = (jrow == 0) | (jrow == cur) | (jrow == cur - 1)
    vals = jnp.where(forced, SEL_BIG, jnp.where(jrow <= cur, imp_t, -SEL_BIG))
    terms = []
    for jp in range(n_slc):
        row = vals[jp:jp + 1, :]
        ahead = (row > vals) | ((row == vals) & (jrow > jp))
        terms.append(jnp.where(ahead, 1.0, 0.0))
    while len(terms) > 1:
        terms = [a + b for a, b in zip(terms[0::2], terms[1::2])]
    cnt = terms[0]

    far_end = (qt - 1) * TQ
    drop_nd = jnp.where(cnt < float(N_SEL), 0.0, NEG)
    drop_far = jnp.where(jrow * SLC_LEN < far_end, drop_nd, NEG)

    def flag_lanes(drop):
        rows = [jnp.zeros((FLAG0, TQ), F32), drop, jnp.full((PAD_LANE - FLAG0 - n_slc + 8, TQ), NEG, F32),
                jnp.zeros((LANES - PAD_LANE - 8, TQ), F32)]
        return jnp.concatenate(rows, axis=0).T

    ck = CK
    q_nd = stacked_q(flag_lanes(drop_nd))
    q_far = stacked_q(flag_lanes(drop_far))
    n_far = jnp.maximum(far_end + ck - 1, 0) // ck

    s_nd = _dot_nt(ks_ref[pl.ds(q0, 2 * TQ), :], q_nd)
    s_nd = jnp.concatenate([s_nd[:, p * TQ:(p + 1) * TQ] + bnd_ref[0, p] for p in range(hpg)], axis=1)

    def select_branch(k):
        scores = [s_nd]
        values = [vst_ref[:, pl.ds(q0, 2 * TQ)]]
        for c in range(k):
            scores.append(_dot_nt(ks_ref[TQ + c * ck:TQ + (c + 1) * ck, :], q_far))
            values.append(vst_ref[:, TQ + c * ck:TQ + (c + 1) * ck])
        m = jnp.max(scores[0], axis=0, keepdims=True)
        for s in scores[1:]:
            m = jnp.maximum(m, jnp.max(s, axis=0, keepdims=True))
        acc = _dot(values[0], jnp.exp2(scores[0] - m).astype(BF16))
        for s, v in zip(scores[1:], values[1:]):
            acc = acc + _dot(v, jnp.exp2(s - m).astype(BF16))
        sel_ref[...] = acc

    for k in range((t - 2 * TQ) // ck + 1):
        pl.when(n_far == k)(functools.partial(select_branch, k))
    acc_sel = sel_ref[...]

    gl_t = jax.nn.sigmoid(gate_ref[0].astype(F32)).T
    outs = []
    for p in range(hpg):
        sl = slice(p * TQ, (p + 1) * TQ)
        g_c = gl_t[p:p + 1, :]
        g_s = gl_t[hpg + p:hpg + p + 1, :] / acc_sel[0:1, sl]
        g_w = gl_t[2 * hpg + p:2 * hpg + p + 1, :] / acc_win[0:1, sl]
        outs.append(g_c * o_cmp[dh:, sl] + g_s * acc_sel[dh:, sl] + g_w * acc_win[dh:, sl])
    o_ref[0] = jnp.concatenate(outs, axis=0).T.astype(o_ref.dtype)


def _nsa_attention(h3, kvc, qgain, kgain, bias_nd, bias_w, bias_cmp, overlap_t, kflag):
    b, t, _ = h3.shape
    g, hpg = NSA_GROUPS, NSA_HPG
    n_blk = t // CMP_STRIDE
    qw = hpg * NSA_DH
    sel0, win0, gate0 = NSA_KV0 + g, NSA_KV0 + 2 * g, NSA_KV0 + 3 * g
    return pl.pallas_call(
        functools.partial(_nsa_attn_kernel, t=t),
        grid=(g, b, t // TQ),
        in_specs=[
            pl.BlockSpec((1, TQ, qw), lambda gi, bi, qi: (bi, qi, gi)),
            pl.BlockSpec((1, 1, n_blk, LANES), lambda gi, bi, qi: (bi, gi, 0, 0)),
            pl.BlockSpec((1, t, LANES), lambda gi, bi, qi: (bi, 0, sel0 + gi)),
            pl.BlockSpec((1, t, LANES), lambda gi, bi, qi: (bi, 0, win0 + gi)),
            pl.BlockSpec((1, TQ, LANES), lambda gi, bi, qi: (bi, qi, gate0 + gi)),
            pl.BlockSpec((1, qw), lambda gi, bi, qi: (0, 0)),
            pl.BlockSpec((2, LANES), lambda gi, bi, qi: (0, 0)),
            pl.BlockSpec((1, hpg, 2 * TQ, TQ), lambda gi, bi, qi: (gi, 0, 0, 0)),
            pl.BlockSpec((1, hpg, WINDOW + TQ, TQ), lambda gi, bi, qi: (gi, 0, 0, 0)),
            pl.BlockSpec((1, hpg, 2 * LANES, TQ), lambda gi, bi, qi: (gi, 0, 0, 0)),
            pl.BlockSpec((LANES, LANES), lambda gi, bi, qi: (0, 0)),
            pl.BlockSpec((t, LANES), lambda gi, bi, qi: (0, 0)),
        ],
        out_specs=pl.BlockSpec((1, TQ, qw), lambda gi, bi, qi: (bi, qi, gi)),
        out_shape=jax.ShapeDtypeStruct((b, t, NSA_HEADS * NSA_DH), BF16),
        scratch_shapes=[pltpu.VMEM((TQ + t, LANES), BF16), pltpu.VMEM((WINDOW + t, LANES), BF16),
                        pltpu.VMEM((LANES, TQ + t), BF16), pltpu.VMEM((LANES, WINDOW + t), BF16),
                        pltpu.VMEM((LANES, hpg * TQ), F32)],
        compiler_params=_cparams(("arbitrary", "arbitrary", "arbitrary")),
    )(h3, kvc, h3, h3, h3, qgain, kgain, bias_nd, bias_w, bias_cmp, overlap_t, kflag)


def _bias_kernel(tbl_ref, nd_ref, w_ref, bc_ref, ond_ref, ow_ref, obc_ref):
    h = pl.program_id(0)
    far = tbl_ref[NUM_BUCKETS - 1, h]
    nd = nd_ref[...]
    w = w_ref[...]
    bc = bc_ref[...]
    o_nd = jnp.full(nd.shape, NEG, F32)
    o_w = jnp.full(w.shape, NEG, F32)
    o_bc = jnp.zeros(bc.shape, F32)
    for bucket in range(NUM_BUCKETS):
        v = tbl_ref[bucket, h]
        o_nd = jnp.where(nd == bucket, (v - far) * LOG2E, o_nd)
        o_w = jnp.where(w == bucket, (v - far) * LOG2E, o_w)
        o_bc = jnp.where(bc == bucket, v * LOG2E, o_bc)
    ond_ref[0] = o_nd
    ow_ref[0] = o_w
    obc_ref[0] = o_bc


def _bias_tiles(rel_bias, nd_bucket, w_bucket, bc_bucket):
    n_heads = rel_bias.shape[1]
    tables = (nd_bucket, w_bucket, bc_bucket)
    return pl.pallas_call(
        _bias_kernel,
        grid=(n_heads,),
        in_specs=[pl.BlockSpec(memory_space=pltpu.SMEM)] + [_resident(a.shape) for a in tables],
        out_specs=[pl.BlockSpec((1,) + a.shape, lambda h: (h, 0, 0)) for a in tables],
        out_shape=[jax.ShapeDtypeStruct((n_heads,) + a.shape, F32) for a in tables],
        compiler_params=_cparams(("parallel",)),
    )(rel_bias.astype(F32), *[jnp.asarray(a, jnp.int32) for a in tables])


def _t5_bucket_np(dist):
    n = np.maximum(dist, 0)
    max_exact = NUM_BUCKETS // 2
    nf = np.maximum(n, max_exact).astype(np.float32)
    large = max_exact + (np.log(nf / max_exact) / np.float32(np.log(MAX_DISTANCE / max_exact))
                         * (NUM_BUCKETS - max_exact)).astype(np.int32)
    large = np.minimum(large, NUM_BUCKETS - 1)
    return np.where(n < max_exact, n, large)


def _nsa_constants(t):
    n_cmp = (t - CMP_LEN) // CMP_STRIDE + 1
    n_slc = t // SLC_LEN
    n_blk = t // CMP_STRIDE
    per_tile = TQ // CMP_STRIDE
    assert n_blk <= LANES and per_tile * (t // TQ - 1) <= LANES - per_tile
    iq = np.arange(TQ)
    rel = iq[None, :] - iq[:, None]
    assert _t5_bucket_np(np.array([TQ + 1]))[0] == NUM_BUCKETS - 1
    dist_nd = np.concatenate([TQ + rel, rel], axis=0)
    nd_bucket = np.where(dist_nd >= 0, _t5_bucket_np(dist_nd), -1)
    dist_w = np.arange(TQ)[None, :] + WINDOW - np.arange(WINDOW + TQ)[:, None]
    w_bucket = np.where((dist_w >= 0) & (dist_w < WINDOW), _t5_bucket_np(dist_w), -1)
    blk_off = np.arange(LANES) - (LANES - per_tile)
    bc_bucket = _t5_bucket_np(iq[None, :] - CMP_STRIDE * blk_off[:, None] - (CMP_LEN - 1))
    bc_bucket = np.concatenate([bc_bucket, bc_bucket], axis=0)
    ci = np.arange(n_blk) * CMP_STRIDE
    sj = np.arange(n_slc) * SLC_LEN
    ov = ((ci[:, None] < sj[None, :] + SLC_LEN) & (ci[:, None] + CMP_LEN > sj[None, :])).astype(np.float32)
    ov[n_cmp:, :] = 0.0
    overlap_t = np.zeros((LANES, LANES), np.float32)
    overlap_t[:n_slc, :n_blk] = ov.T
    assert FLAG0 + n_slc <= PAD_LANE
    kflag = np.zeros((t, LANES), np.float32)
    kflag[np.arange(t), FLAG0 + np.arange(t) // SLC_LEN] = 1.0
    return nd_bucket, w_bucket, bc_bucket, overlap_t, kflag


def _nsa_w_in_columns():
    kv0 = NSA_HEADS * NSA_DH
    kvw = NSA_GROUPS * NSA_DH
    cols = list(range(kv0))
    for br in range(3):
        for g in range(NSA_GROUPS):
            k_src = kv0 + (2 * br) * kvw + g * NSA_DH
            v_src = kv0 + (2 * br + 1) * kvw + g * NSA_DH
            cols += list(range(k_src, k_src + NSA_DH)) + list(range(v_src, v_src + NSA_DH))
    gate0 = kv0 + 6 * kvw
    for g in range(NSA_GROUPS):
        blk = [-1] * LANES
        for br in range(3):
            for p in range(NSA_HPG):
                blk[br * NSA_HPG + p] = gate0 + br * NSA_HEADS + g * NSA_HPG + p
        cols += blk
    return np.asarray(cols, np.int32)


def _retention_layer(x2, b, t, norm_g, w_in, w_out):
    h = _norm_matmul(x2, norm_g, w_in.astype(BF16), tn=1024)
    o = _retention_core(h.reshape(b, t, -1))
    return o.reshape(b * t, -1), w_out.astype(BF16)


def _nsa_w_in(w_in):
    cols = _nsa_w_in_columns()
    return jnp.where(cols[None, :] >= 0, jnp.take(w_in, np.maximum(cols, 0), axis=1), 0.0).astype(BF16)


def _nsa_layer(h, b, t, w_out, q_gain, k_gain, cmp_pos, cmp_w1, cmp_w2, bias_nd, bias_w,
               bias_cmp, overlap_t, kflag):
    h3 = h.reshape(b, t, NSA_COLS)

    dh = NSA_DH
    w1 = cmp_w1.reshape(2, CMP_LEN, dh, CMP_HID)
    zero1 = jnp.zeros((CMP_LEN, dh, CMP_HID), F32)
    w1cat = jnp.concatenate([jnp.concatenate([w1[0], zero1], axis=-1),
                             jnp.concatenate([zero1, w1[1]], axis=-1)], axis=1).astype(BF16)
    zero2 = jnp.zeros((CMP_HID, dh), F32)
    w2cat = jnp.concatenate([jnp.concatenate([cmp_w2[0], zero2], axis=-1),
                             jnp.concatenate([zero2, cmp_w2[1]], axis=-1)], axis=0).astype(BF16)
    pos = jnp.concatenate([cmp_pos[0], cmp_pos[1]], axis=-1)
    ones = jnp.ones((dh,), F32)
    kvc = _compress(h3, pos, w1cat.reshape(CMP_LEN * LANES, 2 * CMP_HID), w2cat,
                    jnp.concatenate([k_gain[0], ones])[None, :])

    qgain = jnp.tile(q_gain * (dh ** -0.5 * LOG2E), NSA_HPG)[None, :]
    kgain = jnp.stack([jnp.concatenate([k_gain[1], ones]), jnp.concatenate([k_gain[2], ones])])
    o = _nsa_attention(h3, kvc, qgain, kgain, bias_nd, bias_w, bias_cmp, overlap_t, kflag)
    return o.reshape(b * t, -1), w_out.astype(BF16)


def kernel(x, ffn1_norm, ffn1_w_gu, ffn1_w_down, mix_norm, ffn2_norm, ffn2_w_gu, ffn2_w_down,
           ret_w_in, ret_w_out, nsa_w_in, nsa_w_out, nsa_q_gain, nsa_k_gain,
           nsa_cmp_pos, nsa_cmp_w1, nsa_cmp_w2, rel_bias):
    b, t, d = x.shape
    depth = ffn1_norm.shape[0]

    def ffn_weights(w_gu, w_down):
        return w_gu.astype(BF16), w_down.astype(BF16)

    nd_bucket, w_bucket, bc_bucket, overlap_t, kflag = _nsa_constants(t)
    bias_nd, bias_w, bias_cmp = _bias_tiles(rel_bias, nd_bucket, w_bucket, bc_bucket)
    bias_nd = bias_nd.reshape(NSA_GROUPS, NSA_HPG, 2 * TQ, TQ)
    bias_w = bias_w.reshape(NSA_GROUPS, NSA_HPG, WINDOW + TQ, TQ)
    bias_cmp = bias_cmp.reshape(NSA_GROUPS, NSA_HPG, 2 * LANES, TQ)
    overlap_t = jnp.asarray(overlap_t, BF16)
    kflag = jnp.asarray(kflag, BF16)

    x2 = x.reshape(b * t, d)
    for layer in range(depth):
        j = layer // 2
        if layer % 2 == 0:
            x2 = _ffn(x2, ffn1_norm[layer][None, :], *ffn_weights(ffn1_w_gu[layer], ffn1_w_down[layer]))
            o2, w_out = _retention_layer(x2, b, t, mix_norm[layer][None, :], ret_w_in[j], ret_w_out[j])
        else:
            x2, h = _ffn_proj(x2, ffn1_norm[layer][None, :], *ffn_weights(ffn1_w_gu[layer], ffn1_w_down[layer]),
                              mix_norm[layer][None, :], _nsa_w_in(nsa_w_in[j]), tn=1024)
            o2, w_out = _nsa_layer(h, b, t, nsa_w_out[j], nsa_q_gain[j],
                                   nsa_k_gain[j], nsa_cmp_pos[j], nsa_cmp_w1[j], nsa_cmp_w2[j], bias_nd, bias_w,
                                   bias_cmp, overlap_t, kflag)
        x2 = _proj_ffn(o2, w_out, x2, ffn2_norm[layer][None, :],
                       *ffn_weights(ffn2_w_gu[layer], ffn2_w_down[layer]))
    return x2.reshape(b, t, d)
```
